```python
import math
import jax
import jax.numpy as jnp
from jax import lax
import numpy as np

D_MODEL = 1024
BATCH = 4
SEQ = 4096
DEPTH = 2
DEC_BATCH = 128
DEC_SEQ = 4
PAST_LEN = 2048
PAGE_SIZE = 128

HEAD_DIM = 64
MIX_WIDTH = D_MODEL
N_GDN_HEADS = (MIX_WIDTH // 2) // HEAD_DIM
GDN_W = N_GDN_HEADS * HEAD_DIM
N_NSA_HEADS = (MIX_WIDTH - GDN_W) // HEAD_DIM
NSA_W = N_NSA_HEADS * HEAD_DIM
N_KV_HEADS = max(1, N_NSA_HEADS // 4)
GQA_GROUP = N_NSA_HEADS // N_KV_HEADS
KV_W = N_KV_HEADS * HEAD_DIM
CONV_W = 4
GDN_CHUNK = 64
CMP_BLOCK = 32
SEL_BLOCK = 64
TOP_K = 16
WINDOW = 512
NSA_QBLOCK = 64
WIN_QBLOCK = 128
D_FF = 4 * D_MODEL
ROPE_THETA = 10000.0
NORM_EPS = 1e-6
PROJ_WIDTHS = (3 * GDN_W, GDN_W, N_GDN_HEADS, N_GDN_HEADS, NSA_W, 6 * KV_W, 3 * N_NSA_HEADS)
D_PROJ = sum(PROJ_WIDTHS)

kernel_name = 'hymba_gdn_nsa_step'


def rmsnorm(x, g):
    xf = x.astype(jnp.float32)
    y = xf * lax.rsqrt(jnp.mean(xf * xf, axis=-1, keepdims=True) + NORM_EPS)
    return (y * g.astype(jnp.float32)).astype(x.dtype)


def l2norm(x):
    return x * lax.rsqrt(jnp.sum(x * x, axis=-1, keepdims=True) + NORM_EPS)


def rope(x, pos):
    half = HEAD_DIM // 2
    inv_freq = ROPE_THETA ** (-jnp.arange(half, dtype=jnp.float32) / half)
    ang = pos.astype(jnp.float32)[:, None] * inv_freq[None, :]
    cos = jnp.cos(ang)[None, :, None, :]
    sin = jnp.sin(ang)[None, :, None, :]
    xf = x.astype(jnp.float32)
    x1, x2 = xf[..., :half], xf[..., half:]
    return jnp.concatenate([x1 * cos - x2 * sin, x2 * cos + x1 * sin], axis=-1).astype(x.dtype)


def masked_softmax(s, mask):
    s = jnp.where(mask, s, -jnp.inf)
    m = jnp.max(s, axis=-1, keepdims=True)
    m = jnp.where(jnp.isfinite(m), m, 0.0)
    e = jnp.exp(s - m)
    return e / jnp.maximum(jnp.sum(e, axis=-1, keepdims=True), 1e-30)


def project(x, norm_g, w_in):
    b, t, _ = x.shape
    h = rmsnorm(x, norm_g)
    cuts = np.cumsum(PROJ_WIDTHS)[:-1].tolist()
    qkv_pre, z, a, beta_logit, q, kv6, gate_logit = jnp.split(h @ w_in, cuts, axis=-1)
    return (qkv_pre, z, a, beta_logit,
            q.reshape(b, t, N_NSA_HEADS, HEAD_DIM),
            kv6.reshape(b, t, 6, N_KV_HEADS, HEAD_DIM),
            gate_logit.reshape(b, t, N_NSA_HEADS, 3))


def gated_delta(q, k, v, g, beta, s0):
    b, t, h, dk = q.shape
    dv = v.shape[-1]
    c = GDN_CHUNK
    n = -(-t // c)
    pad = n * c - t

    def chunks(a):
        a = jnp.pad(a.astype(jnp.float32), [(0, 0), (0, pad)] + [(0, 0)] * (a.ndim - 2))
        a = a.reshape((b, n, c) + a.shape[2:])
        return jnp.moveaxis(a, 3, 2).swapaxes(0, 1)

    qc, kc, vc, gc, bc = chunks(q), chunks(k), chunks(v), chunks(g), chunks(beta)
    gcum = jnp.cumsum(gc, axis=-1)
    incl = jnp.tril(jnp.ones((c, c), dtype=bool))
    strict = jnp.tril(jnp.ones((c, c), dtype=bool), -1)
    decay = jnp.exp(jnp.where(incl, gcum[..., :, None] - gcum[..., None, :], -jnp.inf))
    a_mat = jnp.where(strict, bc[..., :, None] * jnp.einsum('nbhid,nbhjd->nbhij', kc, kc) * decay, 0.0)
    eye = jnp.broadcast_to(jnp.eye(c, dtype=jnp.float32), a_mat.shape)
    t_inv = lax.linalg.triangular_solve(a_mat + eye, eye, left_side=True, lower=True)
    w = jnp.einsum('nbhij,nbhjd->nbhid', t_inv, kc * (bc * jnp.exp(gcum))[..., None])
    u0 = jnp.einsum('nbhij,nbhjd->nbhid', t_inv, vc * bc[..., None])
    qk = jnp.einsum('nbhid,nbhjd->nbhij', qc, kc) * decay

    def step(s, inp):
        q_n, k_n, g_n, w_n, u0_n, qk_n = inp
        u = u0_n - jnp.einsum('bhcd,bhde->bhce', w_n, s)
        o = (jnp.einsum('bhcd,bhde->bhce', q_n * jnp.exp(g_n)[..., None], s)
             + jnp.einsum('bhij,bhje->bhie', qk_n, u))
        g_last = g_n[..., -1:]
        s = s * jnp.exp(g_last)[..., None] + jnp.einsum(
            'bhcd,bhce->bhde', k_n * jnp.exp(g_last - g_n)[..., None], u)
        return s, o

    s_fin, o = lax.scan(step, s0.astype(jnp.float32), (qc, kc, gcum, w, u0, qk))
    o = jnp.moveaxis(o.swapaxes(0, 1), 2, 3).reshape(b, n * c, h, dv)[:, :t]
    return o, s_fin


def gdn_mixer(qkv_pre, z, a, beta_logit, conv_buf, s0, conv_w, a_log, dt_bias, norm_g):
    b, t, _ = qkv_pre.shape
    xp = jnp.concatenate([conv_buf.astype(qkv_pre.dtype), qkv_pre], axis=1)
    conv = sum(xp[:, i:i + t] * conv_w[i] for i in range(CONV_W))
    new_buf = xp[:, t:]
    qkv = jax.nn.silu(conv.astype(jnp.float32)).reshape(b, t, 3, N_GDN_HEADS, HEAD_DIM)
    q = l2norm(qkv[:, :, 0]) * HEAD_DIM ** -0.5
    k = l2norm(qkv[:, :, 1])
    v = qkv[:, :, 2]
    g = -jnp.exp(a_log.astype(jnp.float32)) * jax.nn.softplus(a.astype(jnp.float32) + dt_bias.astype(jnp.float32))
    beta = jax.nn.sigmoid(beta_logit.astype(jnp.float32))
    o, s_new = gated_delta(q, k, v, g, beta, s0)
    o = rmsnorm(o, norm_g) * jax.nn.silu(z.astype(jnp.float32).reshape(b, t, N_GDN_HEADS, HEAD_DIM))
    return o.reshape(b, t, GDN_W), new_buf, s_new


def compress_kv(k, v, lp):
    b, l = k.shape[:2]
    nc = l // CMP_BLOCK

    def phi(r, pe, w):
        blocks = r[:, :nc * CMP_BLOCK].reshape(b, nc, CMP_BLOCK, N_KV_HEADS, HEAD_DIM)
        return jnp.einsum('bnckd,cde->bnke', blocks + pe[:, None, :], w)

    end_pos = (jnp.arange(nc) + 1) * CMP_BLOCK - 1
    return rope(phi(k, lp['cmp_pe_k'], lp['cmp_w_k']), end_pos), phi(v, lp['cmp_pe_v'], lp['cmp_w_v'])


def to_blocks(k, v):
    b, l = k.shape[:2]
    n = -(-l // SEL_BLOCK)
    rows = jnp.stack([k, v], axis=2)
    rows = jnp.pad(rows, [(0, 0), (0, n * SEL_BLOCK - l), (0, 0), (0, 0), (0, 0)])
    return rows.reshape(b, n, SEL_BLOCK, 2, N_KV_HEADS, HEAD_DIM).transpose(0, 4, 1, 2, 3, 5)


def kv_rows(k_cmp, v_cmp, k_sel, v_sel):
    return jnp.stack([k_cmp, v_cmp, k_sel, v_sel], axis=3).transpose(0, 2, 1, 3, 4)


def nsa_cmp_sel(q, pos0, ck, cv, fetch_sel, n_sel):
    b, t = q.shape[:2]
    nc = ck.shape[1]
    qb = math.gcd(t, NSA_QBLOCK)
    ratio = SEL_BLOCK // CMP_BLOCK
    k_top = min(TOP_K, n_sel)
    scale = HEAD_DIM ** -0.5
    f32 = jnp.float32
    cmp_end = (jnp.arange(nc) + 1) * CMP_BLOCK - 1
    blk = jnp.arange(n_sel)

    def one_block(i):
        qi = lax.dynamic_slice_in_dim(q, i * qb, qb, axis=1).reshape(b, qb, N_KV_HEADS, GQA_GROUP, HEAD_DIM)
        qpos = pos0 + i * qb + jnp.arange(qb)
        s_c = jnp.einsum('btkgd,bnkd->btkgn', qi, ck, preferred_element_type=f32) * scale
        p_c = masked_softmax(s_c, (cmp_end[None, :] <= qpos[:, None])[None, :, None, None, :])
        o_c = jnp.einsum('btkgn,bnkd->btkgd', p_c, cv.astype(f32))
        imp = jnp.pad(p_c.sum(axis=3), [(0, 0), (0, 0), (0, 0), (0, n_sel * ratio - nc)])
        imp = imp.reshape(b, qb, N_KV_HEADS, n_sel, ratio).sum(-1)
        cur = qpos // SEL_BLOCK
        valid = blk[None, :] * SEL_BLOCK <= qpos[:, None]
        forced = (blk[None, :] == 0) | (blk[None, :] == cur[:, None]) | (blk[None, :] == cur[:, None] - 1)
        score = jnp.where(forced[None, :, None, :], jnp.inf,
                          jnp.where(valid[None, :, None, :], imp, -jnp.inf))
        _, idx = lax.top_k(score, k_top)
        kv = fetch_sel(idx)
        tok = idx[..., None] * SEL_BLOCK + jnp.arange(SEL_BLOCK)
        mask_s = (tok <= qpos[None, :, None, None, None]).reshape(b, qb, N_KV_HEADS, 1, k_top * SEL_BLOCK)
        s_s = jnp.einsum('btkgd,btknsd->btkgns', qi, kv[..., 0, :], preferred_element_type=f32) * scale
        p_s = masked_softmax(s_s.reshape(b, qb, N_KV_HEADS, GQA_GROUP, k_top * SEL_BLOCK), mask_s)
        p_s = p_s.reshape(b, qb, N_KV_HEADS, GQA_GROUP, k_top, SEL_BLOCK)
        o_s = jnp.einsum('btkgns,btknsd->btkgd', p_s, kv[..., 1, :].astype(f32))
        return o_c, o_s

    o_c, o_s = lax.map(one_block, jnp.arange(t // qb))
    merge = lambda o: jnp.moveaxis(o, 0, 1).reshape(b, t, N_NSA_HEADS, HEAD_DIM)
    return merge(o_c), merge(o_s)


def window_attn(q, pos0, kw, vw):
    b, t = q.shape[:2]
    qb = math.gcd(t, WIN_QBLOCK)
    span = WINDOW + qb
    scale = HEAD_DIM ** -0.5

    def one_block(i):
        qi = lax.dynamic_slice_in_dim(q, i * qb, qb, axis=1).reshape(b, qb, N_KV_HEADS, GQA_GROUP, HEAD_DIM)
        ki = lax.dynamic_slice_in_dim(kw, i * qb, span, axis=1)
        vi = lax.dynamic_slice_in_dim(vw, i * qb, span, axis=1)
        qpos = pos0 + i * qb + jnp.arange(qb)
        kpos = pos0 - WINDOW + i * qb + jnp.arange(span)
        mask = ((kpos[None, :] >= 0) & (kpos[None, :] <= qpos[:, None])
                & (kpos[None, :] >= qpos[:, None] - WINDOW))
        s = jnp.einsum('btkgd,bskd->btkgs', qi, ki, preferred_element_type=jnp.float32) * scale
        p = masked_softmax(s, mask[None, :, None, None, :])
        return jnp.einsum('btkgs,bskd->btkgd', p, vi.astype(jnp.float32))

    o = lax.map(one_block, jnp.arange(t // qb))
    return jnp.moveaxis(o, 0, 1).reshape(b, t, N_NSA_HEADS, HEAD_DIM)


def layer_front(x, pos, lp, conv_buf, s0):
    qkv_pre, z, a, beta_logit, q, kv6, gate_logit = project(x, lp['norm_mix'], lp['w_in'])
    o_gdn, conv_new, s_new = gdn_mixer(qkv_pre, z, a, beta_logit, conv_buf, s0, lp['conv_w'],
                                       lp['a_log'], lp['dt_bias'], lp['gdn_norm'])
    q = rope(q, pos)
    k_cmp, v_cmp = kv6[:, :, 0], kv6[:, :, 1]
    k_sel, v_sel = rope(kv6[:, :, 2], pos), kv6[:, :, 3]
    k_win, v_win = rope(kv6[:, :, 4], pos), kv6[:, :, 5]
    return o_gdn, conv_new, s_new, q, k_cmp, v_cmp, k_sel, v_sel, k_win, v_win, gate_logit


def layer_back(x, o_gdn, gate_logit, o_c, o_s, o_w, lp):
    b, t, _ = x.shape
    gates = jax.nn.sigmoid(gate_logit.astype(jnp.float32))
    o_nsa = (gates[..., 0:1] * o_c + gates[..., 1:2] * o_s + gates[..., 2:3] * o_w).reshape(b, t, NSA_W)
    mixed = jnp.concatenate([o_gdn, o_nsa], axis=-1).astype(x.dtype)
    x = x + mixed @ lp['w_out']
    h = rmsnorm(x, lp['norm_mlp'])
    return x + jnp.square(jax.nn.relu(h @ lp['w_up'])) @ lp['w_down']


def prompt_layer(x, lp):
    b, t, _ = x.shape
    pos = jnp.arange(t)
    conv0 = jnp.zeros((b, CONV_W - 1, 3 * GDN_W), x.dtype)
    s0 = jnp.zeros((b, N_GDN_HEADS, HEAD_DIM, HEAD_DIM), jnp.float32)
    (o_gdn, conv_new, s_new, q, k_cmp, v_cmp, k_sel, v_sel, k_win, v_win,
     gate_logit) = layer_front(x, pos, lp, conv0, s0)
    ck, cv = compress_kv(k_cmp, v_cmp, lp)
    blocks = to_blocks(k_sel, v_sel)
    bi = jnp.arange(b)[:, None, None, None]
    hi = jnp.arange(N_KV_HEADS)[None, None, :, None]
    o_c, o_s = nsa_cmp_sel(q, 0, ck, cv, lambda idx: blocks[bi, hi, idx], blocks.shape[2])
    pad = [(0, 0), (WINDOW, 0), (0, 0), (0, 0)]
    o_w = window_attn(q, 0, jnp.pad(k_win, pad), jnp.pad(v_win, pad))
    y = layer_back(x, o_gdn, gate_logit, o_c, o_s, o_w, lp)
    win_new = jnp.stack([k_win, v_win], axis=2)[:, t - min(WINDOW, t):]
    return y, kv_rows(k_cmp, v_cmp, k_sel, v_sel), win_new, s_new, conv_new


def sample_layer(x, lp, pool, page_table, win_buf, s0, conv_buf):
    b, t, _ = x.shape
    past = page_table.shape[1] * PAGE_SIZE
    pos = past + jnp.arange(t)
    (o_gdn, conv_new, s_new, q, k_cmp, v_cmp, k_sel, v_sel, k_win, v_win,
     gate_logit) = layer_front(x, pos, lp, conv_buf, s0)
    past_cmp = pool[page_table, :, :, :2].transpose(0, 1, 3, 4, 2, 5).reshape(b, past, 2, N_KV_HEADS, HEAD_DIM)
    ck, cv = compress_kv(jnp.concatenate([past_cmp[:, :, 0], k_cmp], axis=1),
                         jnp.concatenate([past_cmp[:, :, 1], v_cmp], axis=1), lp)
    new_blocks = to_blocks(k_sel, v_sel)
    n_new = new_blocks.shape[2]
    n_past = past // SEL_BLOCK
    per_page = PAGE_SIZE // SEL_BLOCK
    pool_blocks = pool.reshape(pool.shape[0], N_KV_HEADS, per_page, SEL_BLOCK, 4, HEAD_DIM)
    bi = jnp.arange(b)[:, None, None, None]
    hi = jnp.arange(N_KV_HEADS)[None, None, :, None]

    def fetch(idx):
        pidx = jnp.minimum(idx, n_past - 1)
        phys = page_table[bi, pidx // per_page]
        old = pool_blocks[phys, hi, pidx % per_page, :, 2:]
        new = new_blocks[bi, hi, jnp.clip(idx - n_past, 0, n_new - 1)]
        return jnp.where((idx < n_past)[..., None, None, None], old, new)

    o_c, o_s = nsa_cmp_sel(q, past, ck, cv, fetch, n_past + n_new)
    wb = win_buf.shape[1]
    win_rows = jnp.concatenate([win_buf.astype(x.dtype), jnp.stack([k_win, v_win], axis=2)], axis=1)
    win_pad = jnp.pad(win_rows, [(0, 0), (WINDOW - wb, 0), (0, 0), (0, 0), (0, 0)])
    o_w = window_attn(q, past, win_pad[:, :, 0], win_pad[:, :, 1])
    y = layer_back(x, o_gdn, gate_logit, o_c, o_s, o_w, lp)
    return y, kv_rows(k_cmp, v_cmp, k_sel, v_sel), win_rows[:, t:], s_new, conv_new


def setup_inputs(seed: int = 0) -> dict:
    key = jax.random.key(seed)
    ks = jax.random.split(key, 24)
    f32 = jnp.float32
    n_pages = PAST_LEN // PAGE_SIZE
    n_pool = (DEC_BATCH * n_pages * 5) // 4
    win_buf = min(WINDOW, PAST_LEN)

    def nrm(k, shape, scale):
        return jax.random.normal(k, shape, f32) * scale

    page_table = jax.random.permutation(ks[3], n_pool)[:DEC_BATCH * n_pages].reshape(DEC_BATCH, n_pages).astype(jnp.int32)
    dt = jnp.exp(jax.random.uniform(ks[11], (DEPTH, N_GDN_HEADS), f32, math.log(1e-3), math.log(1e-1)))
    return {
        'x_prompt': nrm(ks[0], (BATCH, SEQ, D_MODEL), 1.0),
        'x_sample': nrm(ks[1], (DEC_BATCH, DEC_SEQ, D_MODEL), 1.0),
        'cache_kv': nrm(ks[2], (DEPTH, n_pool, N_KV_HEADS, PAGE_SIZE, 4, HEAD_DIM), 1.0),
        'page_table': page_table,
        'state_win': nrm(ks[4], (DEPTH, DEC_BATCH, win_buf, 2, N_KV_HEADS, HEAD_DIM), 1.0),
        'state_gdn': nrm(ks[5], (DEPTH, DEC_BATCH, N_GDN_HEADS, HEAD_DIM, HEAD_DIM), 0.1),
        'state_conv': nrm(ks[6], (DEPTH, DEC_BATCH, CONV_W - 1, 3 * GDN_W), 1.0),
        'norm_mix': 1.0 + nrm(ks[7], (DEPTH, D_MODEL), 0.02),
        'w_in': nrm(ks[8], (DEPTH, D_MODEL, D_PROJ), D_MODEL ** -0.5),
        'conv_w': nrm(ks[9], (DEPTH, CONV_W, 3 * GDN_W), CONV_W ** -0.5),
        'a_log': jnp.log(jax.random.uniform(ks[10], (DEPTH, N_GDN_HEADS), f32, 1.0, 16.0)),
        'dt_bias': dt + jnp.log(-jnp.expm1(-dt)),
        'gdn_norm': 1.0 + nrm(ks[12], (DEPTH, HEAD_DIM), 0.02),
        'cmp_pe_k': nrm(ks[13], (DEPTH, CMP_BLOCK, HEAD_DIM), 0.2),
        'cmp_w_k': nrm(ks[14], (DEPTH, CMP_BLOCK, HEAD_DIM, HEAD_DIM), (CMP_BLOCK * HEAD_DIM) ** -0.5),
        'cmp_pe_v': nrm(ks[15], (DEPTH, CMP_BLOCK, HEAD_DIM), 0.2),
        'cmp_w_v': nrm(ks[16], (DEPTH, CMP_BLOCK, HEAD_DIM, HEAD_DIM), (CMP_BLOCK * HEAD_DIM) ** -0.5),
        'w_out': nrm(ks[17], (DEPTH, MIX_WIDTH, D_MODEL), MIX_WIDTH ** -0.5),
        'norm_mlp': 1.0 + nrm(ks[18], (DEPTH, D_MODEL), 0.02),
        'w_up': nrm(ks[19], (DEPTH, D_MODEL, D_FF), D_MODEL ** -0.5),
        'w_down': nrm(ks[20], (DEPTH, D_FF, D_MODEL), D_FF ** -0.5),
        'norm_final': 1.0 + nrm(ks[21], (D_MODEL,), 0.02),
    }


def reference(x_prompt, x_sample, cache_kv, page_table, state_win, state_gdn, state_conv,
              norm_mix, w_in, conv_w, a_log, dt_bias, gdn_norm, cmp_pe_k, cmp_w_k, cmp_pe_v, cmp_w_v,
              w_out, norm_mlp, w_up, w_down, norm_final):
    xp, xs = x_prompt, x_sample
    rows_p, rows_s, win_p, win_s, gdn_p, gdn_s, conv_p, conv_s = [], [], [], [], [], [], [], []
    for l in range(DEPTH):
        lp = {'norm_mix': norm_mix[l], 'w_in': w_in[l], 'conv_w': conv_w[l], 'a_log': a_log[l],
              'dt_bias': dt_bias[l], 'gdn_norm': gdn_norm[l], 'cmp_pe_k': cmp_pe_k[l], 'cmp_w_k': cmp_w_k[l],
              'cmp_pe_v': cmp_pe_v[l], 'cmp_w_v': cmp_w_v[l], 'w_out': w_out[l], 'norm_mlp': norm_mlp[l],
              'w_up': w_up[l], 'w_down': w_down[l]}
        xp, r, w, s, c = prompt_layer(xp, lp)
        rows_p.append(r)
        win_p.append(w)
        gdn_p.append(s)
        conv_p.append(c)
        xs, r, w, s, c = sample_layer(xs, lp, cache_kv[l], page_table, state_win[l], state_gdn[l], state_conv[l])
        rows_s.append(r)
        win_s.append(w)
        gdn_s.append(s)
        conv_s.append(c)
    y_prompt = rmsnorm(xp, norm_final)
    y_sample = rmsnorm(xs, norm_final)
    return (y_prompt, y_sample, jnp.stack(rows_p), jnp.stack(rows_s), jnp.stack(win_p), jnp.stack(win_s),
            jnp.stack(gdn_p), jnp.stack(gdn_s), jnp.stack(conv_p), jnp.stack(conv_s))
```

```python
import functools
import math

import numpy as np
import jax
import jax.numpy as jnp
from jax import lax
from jax.experimental import pallas as pl
from jax.experimental.pallas import tpu as pltpu

F32 = jnp.float32
BF16 = jnp.bfloat16
HIGHEST = lax.Precision.HIGHEST

D_MODEL = 1024
HEAD_DIM = 64
N_GDN_HEADS = 8
GDN_W = N_GDN_HEADS * HEAD_DIM
N_NSA_HEADS = 8
NSA_W = N_NSA_HEADS * HEAD_DIM
N_KV_HEADS = 2
GQA_GROUP = N_NSA_HEADS // N_KV_HEADS
KV_W = N_KV_HEADS * HEAD_DIM
CONV_W = 4
GDN_CHUNK = 64
CMP_BLOCK = 32
SEL_BLOCK = 64
TOP_K = 16
WINDOW = 512
D_FF = 4 * D_MODEL
ROPE_THETA = 10000.0
NORM_EPS = 1e-6
PAGE_SIZE = 128
N_SCORE = 64
HEAD_SHIFT = 6
QK_SCALE = HEAD_DIM ** -0.5

LANES = 128
SUBLANES = 8
VMEM_LIMIT = 56 * 1024 * 1024

C_QKV = 0
C_Z = 3 * GDN_W
C_Q = C_Z + GDN_W
C_KV = C_Q + NSA_W
C_WIN = C_KV + N_KV_HEADS * 4 * HEAD_DIM
C_AB = C_WIN + 2 * KV_W
C_GATE = C_AB + LANES
N_PROJ = C_GATE + N_KV_HEADS * LANES
ROW_W = 4 * HEAD_DIM


def _cparams(sem):
    return pltpu.CompilerParams(dimension_semantics=sem, vmem_limit_bytes=VMEM_LIMIT)


def _sigmoid(x):
    return 1.0 / (1.0 + jnp.exp(-x))


def _dot(a, b):
    return jnp.dot(a, b, preferred_element_type=F32)


def _dot_nt(a, b):
    return lax.dot_general(a, b, (((1,), (1,)), ((), ())), preferred_element_type=F32)


def _dot_tn(a, b, precision=None):
    return lax.dot_general(a, b, (((0,), (0,)), ((), ())), preferred_element_type=F32, precision=precision)


def _rope128(v, cos, sin):
    lane = lax.broadcasted_iota(jnp.int32, v.shape, 1)
    first = (lane & (HEAD_DIM - 1)) < (HEAD_DIM // 2)
    swapped = jnp.where(first, pltpu.roll(v, LANES - HEAD_DIM // 2, 1), pltpu.roll(v, HEAD_DIM // 2, 1))
    return v * cos + swapped * sin


def _proj_kernel(x_ref, g_ref, w_ref, cf_ref, sf_ref, ch_ref, sh_ref,
                 qkv_ref, z_ref, q_ref, kc_ref, ks_ref, win_ref, ab_ref, gate_ref):
    x = x_ref[...]
    var = jnp.mean(x * x, axis=-1, keepdims=True)
    h = (x * lax.rsqrt(var + NORM_EPS) * g_ref[...]).astype(BF16)
    p = _dot(h, w_ref[...])
    qkv_ref[...] = p[:, C_QKV:C_Z]
    z_ref[...] = p[:, C_Z:C_Q]
    cf, sf, ch, sh = cf_ref[...], sf_ref[...], ch_ref[...], sh_ref[...]
    for j in range(NSA_W // LANES):
        q_ref[:, j * LANES:(j + 1) * LANES] = _rope128(p[:, C_Q + j * LANES:C_Q + (j + 1) * LANES], cf, sf) * QK_SCALE
    for kvh in range(N_KV_HEADS):
        base = C_KV + kvh * ROW_W
        kc_ref[kvh] = p[:, base:base + LANES]
        ks_ref[kvh] = _rope128(p[:, base + LANES:base + ROW_W], ch, sh)
    win_ref[:, 0:LANES] = _rope128(p[:, C_WIN:C_WIN + LANES], cf, sf)
    win_ref[:, LANES:ROW_W] = p[:, C_WIN + LANES:C_WIN + ROW_W]
    ab_ref[...] = p[:, C_AB:C_AB + LANES]
    gate_ref[...] = p[:, C_GATE:C_GATE + N_KV_HEADS * LANES]


def _proj(x, g, w, tabs, tm):
    m = x.shape[0]
    nt = tabs[0].shape[0] // tm
    row = lambda i: (i, 0)
    tab = lambda i: (i % nt, 0)
    fix = lambda i: (0, 0)
    out_shapes = (
        jax.ShapeDtypeStruct((m, 3 * GDN_W), F32),
        jax.ShapeDtypeStruct((m, GDN_W), F32),
        jax.ShapeDtypeStruct((m, NSA_W), F32),
        jax.ShapeDtypeStruct((N_KV_HEADS, m, LANES), F32),
        jax.ShapeDtypeStruct((N_KV_HEADS, m, LANES), F32),
        jax.ShapeDtypeStruct((m, ROW_W), F32),
        jax.ShapeDtypeStruct((m, LANES), F32),
        jax.ShapeDtypeStruct((m, N_KV_HEADS * LANES), F32),
    )
    return pl.pallas_call(
        _proj_kernel,
        grid=(m // tm,),
        in_specs=[pl.BlockSpec((tm, D_MODEL), row), pl.BlockSpec((1, D_MODEL), fix),
                  pl.BlockSpec((D_MODEL, N_PROJ), fix)] + [pl.BlockSpec((tm, LANES), tab)] * 4,
        out_specs=(pl.BlockSpec((tm, 3 * GDN_W), row), pl.BlockSpec((tm, GDN_W), row),
                   pl.BlockSpec((tm, NSA_W), row), pl.BlockSpec((N_KV_HEADS, tm, LANES), lambda i: (0, i, 0)),
                   pl.BlockSpec((N_KV_HEADS, tm, LANES), lambda i: (0, i, 0)),
                   pl.BlockSpec((tm, ROW_W), row), pl.BlockSpec((tm, LANES), row),
                   pl.BlockSpec((tm, N_KV_HEADS * LANES), row)),
        out_shape=out_shapes,
        compiler_params=_cparams(("parallel",)),
        name="proj",
    )(x, g, w, *tabs)


def _gdn_prep_kernel(x_ref, prev_ref, cb_ref, ab_ref, cw_ref, alog_ref, dtb_ref,
                     qkv_ref, gb_ref, xs_ref, *, tt, t_valid, t_out):
    i = pl.program_id(1)
    xs_ref[SUBLANES:SUBLANES + tt, :] = x_ref[0]

    @pl.when(i == 0)
    def _():
        xs_ref[0:SUBLANES, :] = cb_ref[0]

    @pl.when(i > 0)
    def _():
        xs_ref[0:SUBLANES, :] = prev_ref[0]

    conv = xs_ref[SUBLANES:SUBLANES + tt, :] * cw_ref[CONV_W - 1:CONV_W, :]
    for k in range(1, CONV_W):
        conv = conv + xs_ref[SUBLANES - k:SUBLANES - k + tt, :] * cw_ref[CONV_W - 1 - k:CONV_W - k, :]
    c = conv * _sigmoid(conv)

    rows = lax.broadcasted_iota(jnp.int32, (tt, LANES), 0) + i * tt
    live = rows < t_valid
    r_i = lax.broadcasted_iota(jnp.int32, (LANES, LANES), 0)
    c_i = lax.broadcasted_iota(jnp.int32, (LANES, LANES), 1)
    head_ones = ((r_i >> HEAD_SHIFT) == (c_i >> HEAD_SHIFT)).astype(BF16)

    if t_out != tt:
        qkv_ref[...] = jnp.zeros(qkv_ref.shape, F32)
        gb_ref[...] = jnp.zeros(gb_ref.shape, F32)
    for j in range(3 * GDN_W // LANES):
        blk = c[:, j * LANES:(j + 1) * LANES]
        if j < 2 * GDN_W // LANES:
            sq = blk * blk
            hi = sq.astype(BF16)
            lo = (sq - hi.astype(F32)).astype(BF16)
            ss = _dot(hi, head_ones) + _dot(lo, head_ones)
            blk = blk * lax.rsqrt(ss + NORM_EPS)
            if j < GDN_W // LANES:
                blk = blk * QK_SCALE
        qkv_ref[0, 0:tt, j * LANES:(j + 1) * LANES] = jnp.where(live, blk, 0.0)

    ab = ab_ref[0]
    za = ab + dtb_ref[...]
    softplus = jnp.maximum(za, 0.0) + jnp.log(1.0 + jnp.exp(-jnp.abs(za)))
    gdec = -jnp.exp(alog_ref[...]) * softplus
    lane = lax.broadcasted_iota(jnp.int32, (tt, LANES), 1)
    gb = jnp.where(lane < N_GDN_HEADS, gdec, jnp.where(lane < 2 * N_GDN_HEADS, _sigmoid(ab), 0.0))
    gb_ref[0, 0:tt, :] = jnp.where(live, gb, 0.0)


def _gdn_prep(x3, cb8, ab3, cw, alog_row, dtb_row, tt, t_valid, t_out):
    b, t_in, _ = x3.shape
    n = t_in // tt
    blocks8 = tt // SUBLANES
    w3 = 3 * GDN_W
    kern = functools.partial(_gdn_prep_kernel, tt=tt, t_valid=t_valid, t_out=t_out)
    return pl.pallas_call(
        kern,
        grid=(b, n),
        in_specs=[pl.BlockSpec((1, tt, w3), lambda bi, i: (bi, i, 0)),
                  pl.BlockSpec((1, SUBLANES, w3), lambda bi, i: (bi, jnp.maximum(i * blocks8 - 1, 0), 0)),
                  pl.BlockSpec((1, SUBLANES, w3), lambda bi, i: (bi, 0, 0)),
                  pl.BlockSpec((1, tt, LANES), lambda bi, i: (bi, i, 0)),
                  pl.BlockSpec((CONV_W, w3), lambda bi, i: (0, 0)),
                  pl.BlockSpec((1, LANES), lambda bi, i: (0, 0)),
                  pl.BlockSpec((1, LANES), lambda bi, i: (0, 0))],
        out_specs=(pl.BlockSpec((1, t_out, w3), lambda bi, i: (bi, i, 0)),
                   pl.BlockSpec((1, t_out, LANES), lambda bi, i: (bi, i, 0))),
        out_shape=(jax.ShapeDtypeStruct((b, n * t_out, w3), F32),
                   jax.ShapeDtypeStruct((b, n * t_out, LANES), F32)),
        scratch_shapes=[pltpu.VMEM((tt + SUBLANES, w3), F32)],
        compiler_params=_cparams(("parallel", "arbitrary")),
        name="gdn_prep",
    )(x3, x3, cb8, ab3, cw, alog_row, dtb_row)


def _gdn_scan_kernel(qkv_ref, gb_ref, z_ref, s0_ref, gn_ref, o_ref, s_ref, *, t_z):
    ci = pl.program_id(1)
    c = GDN_CHUNK

    @pl.when(ci == 0)
    def _():
        s_ref[...] = s0_ref[...]

    r_i = lax.broadcasted_iota(jnp.int32, (c, c), 0)
    c_i = lax.broadcasted_iota(jnp.int32, (c, c), 1)
    incl = r_i >= c_i
    strict = r_i > c_i
    eye = (r_i == c_i).astype(F32)
    gb = gb_ref[0]
    gcum = jnp.dot(incl.astype(F32), gb, preferred_element_type=F32, precision=HIGHEST)
    gcum_t = _dot_tn(gb, (r_i <= c_i).astype(F32), precision=HIGHEST)
    gnorm = gn_ref[...]

    for h in range(N_GDN_HEADS):
        sl = slice(h * HEAD_DIM, (h + 1) * HEAD_DIM)
        qh = qkv_ref[0, :, h * HEAD_DIM:(h + 1) * HEAD_DIM]
        kh = qkv_ref[0, :, GDN_W + h * HEAD_DIM:GDN_W + (h + 1) * HEAD_DIM]
        vh = qkv_ref[0, :, 2 * GDN_W + h * HEAD_DIM:2 * GDN_W + (h + 1) * HEAD_DIM]
        gc = gcum[:, h:h + 1]
        gr = gcum_t[h:h + 1, :]
        bt = gb[:, N_GDN_HEADS + h:N_GDN_HEADS + h + 1]
        decay = jnp.exp(jnp.where(incl, gc - gr, -jnp.inf))
        kb16 = kh.astype(BF16)
        qk_kk = _dot_nt(jnp.concatenate([qh, kh], axis=0).astype(BF16), kb16)
        qk = qk_kk[0:c] * decay
        a_mat = jnp.where(strict, bt * qk_kk[c:2 * c] * decay, 0.0)
        t_inv = eye - a_mat
        pw = a_mat
        for _ in range(int(math.log2(c)) - 1):
            pw16 = pw.astype(BF16)
            pw = _dot(pw16, pw16)
            t_inv = t_inv + _dot(t_inv.astype(BF16), pw.astype(BF16))
        eg = jnp.exp(gc)
        rhs = jnp.concatenate([kh * (bt * eg), vh * bt], axis=1).astype(BF16)
        wu = _dot(t_inv.astype(BF16), rhs)
        s = s_ref[0, h]
        ws_qs = _dot(jnp.concatenate([wu[:, 0:HEAD_DIM], qh * eg], axis=0).astype(BF16), s.astype(BF16))
        u = wu[:, HEAD_DIM:2 * HEAD_DIM] - ws_qs[0:c]
        u16 = u.astype(BF16)
        o = ws_qs[c:2 * c] + _dot(qk.astype(BF16), u16)
        g_last = gc[c - 1:c, :]
        kd = (kh * jnp.exp(g_last - gc)).astype(BF16)
        s_ref[0, h] = s * jnp.exp(g_last) + _dot_tn(kd, u16)
        on = o * lax.rsqrt(jnp.mean(o * o, axis=-1, keepdims=True) + NORM_EPS) * gnorm
        zh = z_ref[0, 0:t_z, sl]
        o_ref[0, 0:t_z, sl] = on[0:t_z] * (zh * _sigmoid(zh))
        if t_z != c:
            o_ref[0, t_z:c, sl] = jnp.zeros((c - t_z, HEAD_DIM), F32)


def _gdn_scan(qkv3, gb3, z3, s0, gnorm):
    b, tp, _ = qkv3.shape
    n = tp // GDN_CHUNK
    t_z = min(z3.shape[1], GDN_CHUNK)
    kern = functools.partial(_gdn_scan_kernel, t_z=t_z)
    state_spec = pl.BlockSpec((1, N_GDN_HEADS, HEAD_DIM, HEAD_DIM), lambda bi, i: (bi, 0, 0, 0))
    return pl.pallas_call(
        kern,
        grid=(b, n),
        in_specs=[pl.BlockSpec((1, GDN_CHUNK, 3 * GDN_W), lambda bi, i: (bi, i, 0)),
                  pl.BlockSpec((1, GDN_CHUNK, LANES), lambda bi, i: (bi, i, 0)),
                  pl.BlockSpec((1, t_z, GDN_W), lambda bi, i: (bi, i, 0)),
                  state_spec,
                  pl.BlockSpec((1, HEAD_DIM), lambda bi, i: (0, 0))],
        out_specs=(pl.BlockSpec((1, GDN_CHUNK, GDN_W), lambda bi, i: (bi, i, 0)), state_spec),
        out_shape=(jax.ShapeDtypeStruct((b, tp, GDN_W), F32),
                   jax.ShapeDtypeStruct((b, N_GDN_HEADS, HEAD_DIM, HEAD_DIM), F32)),
        compiler_params=_cparams(("parallel", "arbitrary")),
        name="gdn_scan",
    )(qkv3, gb3, z3, s0, gnorm)


def _compress(read_rows, n_half, pe_ref, wbd_ref, cc, cs):
    parts = []
    for parity in range(2):
        acc = jnp.zeros((n_half, LANES), F32)
        for c in range(CMP_BLOCK):
            rows = read_rows(parity * CMP_BLOCK + c, n_half, 2 * CMP_BLOCK)
            acc = acc + _dot((rows + pe_ref[c:c + 1, :]).astype(BF16), wbd_ref[c])
        parts.append(acc)
    ckv = jnp.concatenate(parts, axis=0)
    return _rope128(ckv, cc, cs)


def _softmax_rows(s, mask):
    s = jnp.where(mask, s, -jnp.inf)
    m = jnp.max(s, axis=-1, keepdims=True)
    m = jnp.where(m > -jnp.inf, m, 0.0)
    e = jnp.exp(s - m)
    return e / jnp.maximum(jnp.sum(e, axis=-1, keepdims=True), 1e-30)


def _select_blocks(imp, qpos, n_sel):
    tq = imp.shape[0]
    blk = lax.broadcasted_iota(jnp.int32, (tq, N_SCORE), 1)
    cur = qpos >> HEAD_SHIFT
    forced = (blk == 0) | (blk == cur) | (blk == cur - 1)
    valid = (blk * SEL_BLOCK <= qpos) & (blk < n_sel)
    score = jnp.where(forced, jnp.inf, jnp.where(valid, imp, -jnp.inf))
    rank = jnp.zeros((tq, N_SCORE), F32)
    for i in range(min(n_sel, N_SCORE)):
        si = score[:, i:i + 1]
        ahead = (si > score) | ((si == score) & (blk > i))
        rank = rank + jnp.where(ahead, 1.0, 0.0)
    return ((rank < TOP_K) & (blk < n_sel)).astype(BF16)


def _nsa_core(q_blk, qpos_t, ckv, n_half, sel_chunk, n_chunks, kc, n_sel,
              win_rows, win_pos0, kvh, gates):
    tq = q_blk.shape[0]
    r = GQA_GROUP * tq
    q_all = jnp.concatenate([q_blk[:, g * HEAD_DIM:(g + 1) * HEAD_DIM] for g in range(GQA_GROUP)], axis=0)
    qpos = jnp.concatenate([qpos_t] * GQA_GROUP, axis=0)
    zeros64 = jnp.zeros((r, HEAD_DIM), F32)

    q_c = jnp.concatenate([q_all, zeros64], axis=1).astype(BF16)
    ckv16 = ckv.astype(BF16)
    nc = 2 * n_half
    col = lax.broadcasted_iota(jnp.int32, (r, nc), 1)
    cmp_end = (2 * (col & (n_half - 1)) + (col >> int(math.log2(n_half))) + 1) * CMP_BLOCK - 1
    p_c = _softmax_rows(_dot_nt(q_c, ckv16), cmp_end <= qpos)
    o_c = _dot(p_c.astype(BF16), ckv16)[:, HEAD_DIM:2 * HEAD_DIM]
    pair = p_c[:, 0:n_half] + p_c[:, n_half:nc]
    imp = pair[0:tq]
    for g in range(1, GQA_GROUP):
        imp = imp + pair[g * tq:(g + 1) * tq]
    if n_half < N_SCORE:
        imp = jnp.concatenate([imp, jnp.zeros((tq, N_SCORE - n_half), F32)], axis=1)
    sel = _select_blocks(imp, qpos_t, n_sel)

    j_i = lax.broadcasted_iota(jnp.int32, (N_SCORE, kc), 0)
    k_i = lax.broadcasted_iota(jnp.int32, (N_SCORE, kc), 1) >> HEAD_SHIFT
    kcol = lax.broadcasted_iota(jnp.int32, (r, kc), 1)

    def body(c, carry):
        m, l, acc = carry
        rows = sel_chunk(c)
        s = _dot_nt(q_c, rows)
        expand = (j_i == k_i + c * (kc // SEL_BLOCK)).astype(BF16)
        picked = _dot(sel, expand)
        picked = jnp.concatenate([picked] * GQA_GROUP, axis=0)
        mask = (picked > 0.5) & (kcol + c * kc <= qpos)
        s = jnp.where(mask, s, -jnp.inf)
        m_new = jnp.maximum(m, jnp.max(s, axis=-1, keepdims=True))
        m_safe = jnp.where(m_new > -jnp.inf, m_new, 0.0)
        alpha = jnp.exp(m - m_safe)
        p = jnp.exp(s - m_safe)
        l = alpha * l + jnp.sum(p, axis=-1, keepdims=True)
        acc = alpha * acc + _dot(p.astype(BF16), rows)
        return m_new, l, acc

    init = (jnp.full((r, 1), -jnp.inf, F32), jnp.zeros((r, 1), F32), jnp.zeros((r, LANES), F32))
    _, l_s, acc_s = lax.fori_loop(0, n_chunks, body, init)
    o_s = acc_s[:, HEAD_DIM:2 * HEAD_DIM] / jnp.maximum(l_s, 1e-30)

    lane = lax.broadcasted_iota(jnp.int32, (r, ROW_W), 1)
    q_w = jnp.where((lane >> HEAD_SHIFT) == kvh, jnp.concatenate([q_all, q_all, zeros64, zeros64], axis=1), 0.0)
    q_w = q_w.astype(BF16)
    nw = win_rows.shape[0]
    kpos = lax.broadcasted_iota(jnp.int32, (r, nw), 1) + win_pos0
    p_w = _softmax_rows(_dot_nt(q_w, win_rows), (kpos <= qpos) & (kpos >= qpos - WINDOW))
    o_w2 = _dot(p_w.astype(BF16), win_rows)
    o_w = jnp.where(kvh == 0, o_w2[:, 2 * HEAD_DIM:3 * HEAD_DIM], o_w2[:, 3 * HEAD_DIM:4 * HEAD_DIM])

    outs = []
    for g in range(GQA_GROUP):
        rows_g = slice(g * tq, (g + 1) * tq)
        g0 = _sigmoid(gates[:, 3 * g:3 * g + 1])
        g1 = _sigmoid(gates[:, 3 * g + 1:3 * g + 2])
        g2 = _sigmoid(gates[:, 3 * g + 2:3 * g + 3])
        outs.append(g0 * o_c[rows_g] + g1 * o_s[rows_g] + g2 * o_w[rows_g])
    return outs


def _nsa_prompt_kernel(q_ref, kc_ref, ks_ref, win_ref, gate_ref, wbd_ref, pe_ref, cc_ref, cs_ref,
                       o_ref, ckv_ref, *, tq, t, kc):
    kvh = pl.program_id(1)
    i = pl.program_id(2)
    n_half = t // (2 * CMP_BLOCK)

    @pl.when(i == 0)
    def _():
        read = lambda start, size, stride: kc_ref[0, pl.ds(start, size, stride=stride), :]
        ckv_ref[...] = _compress(read, n_half, pe_ref, wbd_ref, cc_ref[...], cs_ref[...])

    t0 = i * tq
    qpos_t = lax.broadcasted_iota(jnp.int32, (tq, 1), 0) + t0
    sel_chunk = lambda c: ks_ref[0, pl.ds(pl.multiple_of(c * kc, kc), kc), :].astype(BF16)
    n_chunks = (t0 + tq + kc - 1) // kc
    span = WINDOW + tq
    start = pl.multiple_of(jnp.maximum(t0 - WINDOW, 0), tq)
    win_rows = win_ref[pl.ds(start, span), :].astype(BF16)
    outs = _nsa_core(q_ref[...], qpos_t, ckv_ref[...], n_half, sel_chunk, n_chunks, kc, t // SEL_BLOCK,
                     win_rows, start, kvh, gate_ref[...])
    for g in range(GQA_GROUP):
        o_ref[:, g * HEAD_DIM:(g + 1) * HEAD_DIM] = outs[g]


def _nsa_prompt(q, kvc, kvs, win, gate, wbd, pe, cc, cs, b, t, tq=128, kc=512):
    n = t // tq
    nc = t // CMP_BLOCK
    kern = functools.partial(_nsa_prompt_kernel, tq=tq, t=t, kc=kc)
    fix2 = lambda bi, h, i: (0, 0)
    return pl.pallas_call(
        kern,
        grid=(b, N_KV_HEADS, n),
        in_specs=[pl.BlockSpec((tq, ROW_W), lambda bi, h, i: (bi * n + i, h)),
                  pl.BlockSpec((1, t, LANES), lambda bi, h, i: (h, bi, 0)),
                  pl.BlockSpec((1, t, LANES), lambda bi, h, i: (h, bi, 0)),
                  pl.BlockSpec((t, ROW_W), lambda bi, h, i: (bi, 0)),
                  pl.BlockSpec((tq, LANES), lambda bi, h, i: (bi * n + i, h)),
                  pl.BlockSpec((CMP_BLOCK, LANES, LANES), lambda bi, h, i: (0, 0, 0)),
                  pl.BlockSpec((CMP_BLOCK, LANES), fix2),
                  pl.BlockSpec((nc, LANES), fix2),
                  pl.BlockSpec((nc, LANES), fix2)],
        out_specs=pl.BlockSpec((tq, ROW_W), lambda bi, h, i: (bi * n + i, h)),
        out_shape=jax.ShapeDtypeStruct((b * t, NSA_W), F32),
        scratch_shapes=[pltpu.VMEM((nc, LANES), F32)],
        compiler_params=_cparams(("parallel", "parallel", "arbitrary")),
        name="nsa_prompt",
    )(q, kvc, kvs, win, gate, wbd, pe, cc, cs)


def _nsa_sample_kernel(pt_ref, pool_ref, q_ref, kvnew_ref, wbuf_ref, wnew_ref, gate_ref,
                       wbd_ref, pe_ref, cc_ref, cs_ref, o_ref, pgc_ref, pgs_ref, wn_ref,
                       *, n_pages, past, tq, n_pad, nw):
    j = pl.program_id(1)
    for kvh in range(N_KV_HEADS):
        page_rows = pl.ds(pl.multiple_of(j * PAGE_SIZE, PAGE_SIZE), PAGE_SIZE)
        pgc_ref[kvh, page_rows, :] = pool_ref[0, 0, kvh, :, 0:LANES]
        pgs_ref[kvh, page_rows, :] = pool_ref[0, 0, kvh, :, LANES:ROW_W]

    @pl.when(j == n_pages - 1)
    def _():
        n_half = past // (2 * CMP_BLOCK)
        wn_ref[0:WINDOW, :] = wbuf_ref[0]
        wn_ref[WINDOW:WINDOW + tq, :] = wnew_ref[0]
        wn_ref[WINDOW + tq:nw, :] = jnp.zeros((nw - WINDOW - tq, ROW_W), F32)
        win_rows = wn_ref[...].astype(BF16)
        qpos_t = lax.broadcasted_iota(jnp.int32, (tq, 1), 0) + past
        for kvh in range(N_KV_HEADS):
            pgs_ref[kvh, past:past + tq, :] = kvnew_ref[kvh, 0]
            pgs_ref[kvh, past + tq:n_pad, :] = jnp.zeros((n_pad - past - tq, LANES), F32)
            read = lambda start, size, stride, kvh=kvh: pgc_ref[kvh, pl.ds(start, size, stride=stride), :]
            ckv = _compress(read, n_half, pe_ref, wbd_ref, cc_ref[...], cs_ref[...])
            sel_chunk = lambda c, kvh=kvh: pgs_ref[kvh].astype(BF16)
            outs = _nsa_core(q_ref[0, :, kvh * ROW_W:(kvh + 1) * ROW_W], qpos_t, ckv, n_half, sel_chunk, 1,
                             n_pad, past // SEL_BLOCK + 1, win_rows, past - WINDOW, kvh,
                             gate_ref[0, :, kvh * LANES:(kvh + 1) * LANES])
            for g in range(GQA_GROUP):
                col = kvh * ROW_W + g * HEAD_DIM
                o_ref[0, :, col:col + HEAD_DIM] = outs[g]


def _nsa_sample(page_table, pool5, layer, q8, kvnew8, wbuf, wnew8, gate8, wbd, pe, cc, cs):
    b, n_pages = page_table.shape
    past = n_pages * PAGE_SIZE
    tq = q8.shape[1]
    n_pad = past + SEL_BLOCK
    nw = WINDOW + LANES
    nc = past // CMP_BLOCK
    kern = functools.partial(_nsa_sample_kernel, n_pages=n_pages, past=past, tq=tq, n_pad=n_pad, nw=nw)
    per_b = lambda bi, j, pt: (bi, 0, 0)
    fix2 = lambda bi, j, pt: (0, 0)
    grid_spec = pltpu.PrefetchScalarGridSpec(
        num_scalar_prefetch=1,
        grid=(b, n_pages),
        in_specs=[pl.BlockSpec((1, 1, N_KV_HEADS, PAGE_SIZE, ROW_W), lambda bi, j, pt: (layer, pt[bi, j], 0, 0, 0)),
                  pl.BlockSpec((1, tq, NSA_W), per_b),
                  pl.BlockSpec((N_KV_HEADS, 1, tq, LANES), lambda bi, j, pt: (0, bi, 0, 0)),
                  pl.BlockSpec((1, WINDOW, ROW_W), per_b),
                  pl.BlockSpec((1, tq, ROW_W), per_b),
                  pl.BlockSpec((1, tq, N_KV_HEADS * LANES), per_b),
                  pl.BlockSpec((CMP_BLOCK, LANES, LANES), lambda bi, j, pt: (0, 0, 0)),
                  pl.BlockSpec((CMP_BLOCK, LANES), fix2),
                  pl.BlockSpec((nc, LANES), fix2),
                  pl.BlockSpec((nc, LANES), fix2)],
        out_specs=pl.BlockSpec((1, tq, NSA_W), per_b),
        scratch_shapes=[pltpu.VMEM((N_KV_HEADS, past, LANES), F32), pltpu.VMEM((N_KV_HEADS, n_pad, LANES), F32),
                        pltpu.VMEM((nw, ROW_W), F32)],
    )
    return pl.pallas_call(
        kern,
        grid_spec=grid_spec,
        out_shape=jax.ShapeDtypeStruct((b, tq, NSA_W), F32),
        compiler_params=_cparams(("parallel", "arbitrary")),
        name="nsa_sample",
    )(page_table, pool5, q8, kvnew8, wbuf, wnew8, gate8, wbd, pe, cc, cs)


def _mlp_kernel(x_ref, og_ref, on_ref, wo_ref, gm_ref, wu_ref, wd_ref, gf_ref, y_ref, x1_ref, h_ref, acc_ref,
                *, final_norm):
    f = pl.program_id(1)

    @pl.when(f == 0)
    def _():
        x1 = (x_ref[...] + _dot(og_ref[...].astype(BF16), wo_ref[0:GDN_W, :])
              + _dot(on_ref[...].astype(BF16), wo_ref[GDN_W:GDN_W + NSA_W, :]))
        x1_ref[...] = x1
        var = jnp.mean(x1 * x1, axis=-1, keepdims=True)
        h_ref[...] = (x1 * lax.rsqrt(var + NORM_EPS) * gm_ref[...]).astype(BF16)
        acc_ref[...] = jnp.zeros(acc_ref.shape, F32)

    up = jnp.maximum(_dot(h_ref[...], wu_ref[...]), 0.0)
    acc_ref[...] += _dot((up * up).astype(BF16), wd_ref[...])

    @pl.when(f == pl.num_programs(1) - 1)
    def _():
        y = x1_ref[...] + acc_ref[...]
        if final_norm:
            var = jnp.mean(y * y, axis=-1, keepdims=True)
            y = y * lax.rsqrt(var + NORM_EPS) * gf_ref[...]
        y_ref[...] = y


def _mlp(x, og, on, wo, gm, wu, wd, gf, final_norm, tm, tf=1024):
    m = x.shape[0]
    kern = functools.partial(_mlp_kernel, final_norm=final_norm)
    row = lambda i, f: (i, 0)
    fix = lambda i, f: (0, 0)
    return pl.pallas_call(
        kern,
        grid=(m // tm, D_FF // tf),
        in_specs=[pl.BlockSpec((tm, D_MODEL), row), pl.BlockSpec((tm, GDN_W), row), pl.BlockSpec((tm, NSA_W), row),
                  pl.BlockSpec((D_MODEL, D_MODEL), fix), pl.BlockSpec((1, D_MODEL), fix),
                  pl.BlockSpec((D_MODEL, tf), lambda i, f: (0, f)), pl.BlockSpec((tf, D_MODEL), lambda i, f: (f, 0)),
                  pl.BlockSpec((1, D_MODEL), fix)],
        out_specs=pl.BlockSpec((tm, D_MODEL), row),
        out_shape=jax.ShapeDtypeStruct((m, D_MODEL), F32),
        scratch_shapes=[pltpu.VMEM((tm, D_MODEL), F32), pltpu.VMEM((tm, D_MODEL), BF16),
                        pltpu.VMEM((tm, D_MODEL), F32)],
        compiler_params=_cparams(("parallel", "arbitrary")),
        name="mlp",
    )(x, og, on, wo, gm, wu, wd, gf)


def _proj_columns():
    src = np.full((N_PROJ,), -1, np.int64)
    o_z, o_a, o_b = 3 * GDN_W, 4 * GDN_W, 4 * GDN_W + N_GDN_HEADS
    o_q = o_b + N_GDN_HEADS
    o_kv = o_q + NSA_W
    o_g = o_kv + 6 * KV_W
    src[C_QKV:C_Z] = np.arange(0, 3 * GDN_W)
    src[C_Z:C_Q] = np.arange(o_z, o_z + GDN_W)
    src[C_Q:C_KV] = np.arange(o_q, o_q + NSA_W)
    for kvh in range(N_KV_HEADS):
        for s in range(4):
            dst = C_KV + kvh * ROW_W + s * HEAD_DIM
            src[dst:dst + HEAD_DIM] = o_kv + s * KV_W + kvh * HEAD_DIM + np.arange(HEAD_DIM)
    src[C_WIN:C_WIN + 2 * KV_W] = o_kv + 4 * KV_W + np.arange(2 * KV_W)
    src[C_AB:C_AB + N_GDN_HEADS] = o_a + np.arange(N_GDN_HEADS)
    src[C_AB + N_GDN_HEADS:C_AB + 2 * N_GDN_HEADS] = o_b + np.arange(N_GDN_HEADS)
    for kvh in range(N_KV_HEADS):
        dst = C_GATE + kvh * LANES
        src[dst:dst + 3 * GQA_GROUP] = o_g + kvh * 3 * GQA_GROUP + np.arange(3 * GQA_GROUP)
    return src


def _rope_tables(pos):
    half = HEAD_DIM // 2
    inv_freq = ROPE_THETA ** (-jnp.arange(half, dtype=F32) / half)
    ang = pos.astype(F32)[:, None] * inv_freq[None, :]
    cos, sin = jnp.cos(ang), jnp.sin(ang)
    c64 = jnp.concatenate([cos, cos], axis=1)
    s64 = jnp.concatenate([-sin, sin], axis=1)
    one, zero = jnp.ones_like(c64), jnp.zeros_like(s64)
    return (jnp.concatenate([c64, c64], axis=1), jnp.concatenate([s64, s64], axis=1),
            jnp.concatenate([c64, one], axis=1), jnp.concatenate([s64, zero], axis=1))


def _cmp_tables(n_blocks):
    n_half = n_blocks // 2
    r = np.arange(n_blocks)
    blk = 2 * (r % n_half) + r // n_half
    end_pos = jnp.asarray((blk + 1) * CMP_BLOCK - 1)
    _, _, cc, cs = _rope_tables(end_pos)
    return cc, cs


def _layer_params(l, norm_mix, w_in, conv_w, a_log, dt_bias, gdn_norm, cmp_pe_k, cmp_w_k, cmp_pe_v, cmp_w_v,
                  w_out, norm_mlp, w_up, w_down):
    src = _proj_columns()
    w_ext = jnp.concatenate([w_in[l], jnp.zeros((D_MODEL, 1), F32)], axis=1)
    w_p = jnp.take(w_ext, jnp.asarray(np.where(src < 0, w_in.shape[2], src)), axis=1).astype(BF16)
    zeros = jnp.zeros((CMP_BLOCK, HEAD_DIM, HEAD_DIM), F32)
    wbd = jnp.concatenate([jnp.concatenate([cmp_w_k[l], zeros], axis=2),
                           jnp.concatenate([zeros, cmp_w_v[l]], axis=2)], axis=1).astype(BF16)
    pad_row = lambda v: jnp.pad(v.astype(F32), (0, LANES - v.shape[0]))[None, :]
    return dict(
        norm_mix=norm_mix[l][None, :], w_p=w_p, conv_w=conv_w[l], alog=pad_row(a_log[l]), dtb=pad_row(dt_bias[l]),
        gdn_norm=gdn_norm[l][None, :], wbd=wbd, pe=jnp.concatenate([cmp_pe_k[l], cmp_pe_v[l]], axis=1),
        w_out=w_out[l].astype(BF16), norm_mlp=norm_mlp[l][None, :], w_up=w_up[l].astype(BF16),
        w_down=w_down[l].astype(BF16))


def _prompt_layer(x, lp, tabs, cmp_tabs, gf, final_norm, b, t):
    qkv_pre, z, q, kvc, kvs, win, ab, gate = _proj(x, lp['norm_mix'], lp['w_p'], tabs, 256)
    w3 = 3 * GDN_W
    cb8 = jnp.zeros((b, SUBLANES, w3), F32)
    qkv3, gb3 = _gdn_prep(qkv_pre.reshape(b, t, w3), cb8, ab.reshape(b, t, LANES), lp['conv_w'], lp['alog'],
                          lp['dtb'], 256, t, 256)
    s0 = jnp.zeros((b, N_GDN_HEADS, HEAD_DIM, HEAD_DIM), F32)
    o_gdn, s_new = _gdn_scan(qkv3, gb3, z.reshape(b, t, GDN_W), s0, lp['gdn_norm'])
    o_nsa = _nsa_prompt(q, kvc, kvs, win, gate, lp['wbd'], lp['pe'], cmp_tabs[0], cmp_tabs[1], b, t)
    y = _mlp(x, o_gdn.reshape(b * t, GDN_W), o_nsa, lp['w_out'], lp['norm_mlp'], lp['w_up'], lp['w_down'], gf,
             final_norm, 512)
    rows = jnp.concatenate([kvc, kvs], axis=2).reshape(N_KV_HEADS, b, t, 4, HEAD_DIM).transpose(1, 0, 2, 3, 4)
    win_new = win.reshape(b, t, 2, N_KV_HEADS, HEAD_DIM)[:, t - min(WINDOW, t):]
    conv_new = qkv_pre.reshape(b, t, w3)[:, t - (CONV_W - 1):]
    return y, rows, win_new, s_new, conv_new


def _pad_rows(a, n):
    return jnp.pad(a, ((0, 0), (0, n - a.shape[1]), (0, 0)))


def _sample_layer(x, lp, tabs, cmp_tabs, gf, final_norm, b, t, layer, pool5, page_table, win_buf, s0, conv_buf):
    m = b * t
    qkv_pre, z, q, kvc, kvs, win, ab, gate = _proj(x, lp['norm_mix'], lp['w_p'], tabs, 256)
    w3 = 3 * GDN_W
    tp = SUBLANES
    cb8 = jnp.pad(conv_buf, ((0, 0), (SUBLANES - (CONV_W - 1), 0), (0, 0)))
    qkv3, gb3 = _gdn_prep(_pad_rows(qkv_pre.reshape(b, t, w3), tp), cb8, _pad_rows(ab.reshape(b, t, LANES), tp),
                          lp['conv_w'], lp['alog'], lp['dtb'], tp, t, GDN_CHUNK)
    o_gdn, s_new = _gdn_scan(qkv3, gb3, _pad_rows(z.reshape(b, t, GDN_W), tp), s0, lp['gdn_norm'])
    o_gdn = o_gdn[:, :t].reshape(m, GDN_W)
    kvnew8 = jnp.pad(kvs.reshape(N_KV_HEADS, b, t, LANES), ((0, 0), (0, 0), (0, tp - t), (0, 0)))
    o_nsa = _nsa_sample(page_table, pool5, layer, _pad_rows(q.reshape(b, t, NSA_W), tp), kvnew8,
                        win_buf.reshape(b, WINDOW, ROW_W), _pad_rows(win.reshape(b, t, ROW_W), tp),
                        _pad_rows(gate.reshape(b, t, N_KV_HEADS * LANES), tp), lp['wbd'], lp['pe'],
                        cmp_tabs[0], cmp_tabs[1])
    o_nsa = o_nsa[:, :t].reshape(m, NSA_W)
    y = _mlp(x, o_gdn, o_nsa, lp['w_out'], lp['norm_mlp'], lp['w_up'], lp['w_down'], gf, final_norm, 512)
    rows = jnp.concatenate([kvc, kvs], axis=2).reshape(N_KV_HEADS, b, t, 4, HEAD_DIM).transpose(1, 0, 2, 3, 4)
    win_new = jnp.concatenate([win_buf[:, t:], win.reshape(b, t, 2, N_KV_HEADS, HEAD_DIM)], axis=1)
    conv_new = jnp.concatenate([conv_buf, qkv_pre.reshape(b, t, w3)], axis=1)[:, t:]
    return y, rows, win_new, s_new, conv_new


def kernel(x_prompt, x_sample, cache_kv, page_table, state_win, state_gdn, state_conv, norm_mix, w_in, conv_w, a_log, dt_bias, gdn_norm, cmp_pe_k, cmp_w_k, cmp_pe_v, cmp_w_v, w_out, norm_mlp, w_up, w_down, norm_final):
    bp, tp_, _ = x_prompt.shape
    bs, ts, _ = x_sample.shape
    depth = cache_kv.shape[0]
    n_pages = page_table.shape[1]
    past = n_pages * PAGE_SIZE
    assert state_win.shape[2] == WINDOW and tp_ % 512 == 0 and ts <= SUBLANES and past % (2 * CMP_BLOCK) == 0

    tabs_p = _rope_tables(jnp.arange(tp_))
    tabs_s = tuple(jnp.tile(tb, (bs, 1)) for tb in _rope_tables(past + jnp.arange(ts)))
    cmp_p = _cmp_tables(tp_ // CMP_BLOCK)
    cmp_s = _cmp_tables((past + ts) // CMP_BLOCK)
    pool5 = cache_kv.reshape(depth, cache_kv.shape[1], N_KV_HEADS, PAGE_SIZE, ROW_W)
    gf = norm_final[None, :]

    xp = x_prompt.reshape(bp * tp_, D_MODEL)
    xs = x_sample.reshape(bs * ts, D_MODEL)
    outs = [[] for _ in range(8)]
    for l in range(depth):
        lp = _layer_params(l, norm_mix, w_in, conv_w, a_log, dt_bias, gdn_norm, cmp_pe_k, cmp_w_k, cmp_pe_v,
                           cmp_w_v, w_out, norm_mlp, w_up, w_down)
        last = l == depth - 1
        xp, r, w, s, c = _prompt_layer(xp, lp, tabs_p, cmp_p, gf, last, bp, tp_)
        for k, v in zip((0, 2, 4, 6), (r, w, s, c)):
            outs[k].append(v)
        xs, r, w, s, c = _sample_layer(xs, lp, tabs_s, cmp_s, gf, last, bs, ts, l, pool5, page_table,
                                       state_win[l], state_gdn[l], state_conv[l])
        for k, v in zip((1, 3, 5, 7), (r, w, s, c)):
            outs[k].append(v)
    return (xp.reshape(bp, tp_, D_MODEL), xs.reshape(bs, ts, D_MODEL)) + tuple(jnp.stack(o) for o in outs)
```

```python
import functools
import math

import numpy as np
import jax
import jax.numpy as jnp
from jax import lax
from jax.experimental import pallas as pl
from jax.experimental.pallas import tpu as pltpu

F32 = jnp.float32
BF16 = jnp.bfloat16
HIGHEST = lax.Precision.HIGHEST

D_MODEL = 1024
HEAD_DIM = 64
N_GDN_HEADS = 8
GDN_W = N_GDN_HEADS * HEAD_DIM
N_NSA_HEADS = 8
NSA_W = N_NSA_HEADS * HEAD_DIM
N_KV_HEADS = 2
GQA_GROUP = N_NSA_HEADS // N_KV_HEADS
KV_W = N_KV_HEADS * HEAD_DIM
CONV_W = 4
GDN_CHUNK = 64
CMP_BLOCK = 32
SEL_BLOCK = 64
TOP_K = 16
WINDOW = 512
D_FF = 4 * D_MODEL
ROPE_THETA = 10000.0
NORM_EPS = 1e-6
PAGE_SIZE = 128
N_SCORE = 64
HEAD_SHIFT = 6
QK_SCALE = HEAD_DIM ** -0.5

LANES = 128
SUBLANES = 8
VMEM_LIMIT = 56 * 1024 * 1024

C_QKV = 0
C_Z = 3 * GDN_W
C_Q = C_Z + GDN_W
C_KV = C_Q + NSA_W
C_WIN = C_KV + N_KV_HEADS * 4 * HEAD_DIM
C_AB = C_WIN + 2 * KV_W
C_GATE = C_AB + LANES
N_PROJ = C_GATE + N_KV_HEADS * LANES
ROW_W = 4 * HEAD_DIM


def _cparams(sem):
    return pltpu.CompilerParams(dimension_semantics=sem, vmem_limit_bytes=VMEM_LIMIT)


def _sigmoid(x):
    return 1.0 / (1.0 + jnp.exp(-x))


def _dot(a, b):
    return jnp.dot(a, b, preferred_element_type=F32)


def _dot_nt(a, b):
    return lax.dot_general(a, b, (((1,), (1,)), ((), ())), preferred_element_type=F32)


def _dot_tn(a, b, precision=None):
    return lax.dot_general(a, b, (((0,), (0,)), ((), ())), preferred_element_type=F32, precision=precision)


def _rope128(v, cos, sin):
    lane = lax.broadcasted_iota(jnp.int32, v.shape, 1)
    first = (lane & (HEAD_DIM - 1)) < (HEAD_DIM // 2)
    swapped = jnp.where(first, pltpu.roll(v, LANES - HEAD_DIM // 2, 1), pltpu.roll(v, HEAD_DIM // 2, 1))
    return v * cos + swapped * sin


def _proj_kernel(x_ref, g_ref, w_ref, cf_ref, sf_ref, ch_ref, sh_ref,
                 qkv_ref, z_ref, q_ref, kc_ref, ks_ref, win_ref, ab_ref, gate_ref):
    x = x_ref[...]
    var = jnp.mean(x * x, axis=-1, keepdims=True)
    h = (x * lax.rsqrt(var + NORM_EPS) * g_ref[...]).astype(BF16)
    p = _dot(h, w_ref[...])
    qkv_ref[...] = p[:, C_QKV:C_Z]
    z_ref[...] = p[:, C_Z:C_Q]
    cf, sf, ch, sh = cf_ref[...], sf_ref[...], ch_ref[...], sh_ref[...]
    for j in range(NSA_W // LANES):
        q_ref[:, j * LANES:(j + 1) * LANES] = _rope128(p[:, C_Q + j * LANES:C_Q + (j + 1) * LANES], cf, sf) * QK_SCALE
    for kvh in range(N_KV_HEADS):
        base = C_KV + kvh * ROW_W
        kc_ref[kvh] = p[:, base:base + LANES]
        ks_ref[kvh] = _rope128(p[:, base + LANES:base + ROW_W], ch, sh)
    win_ref[:, 0:LANES] = _rope128(p[:, C_WIN:C_WIN + LANES], cf, sf)
    win_ref[:, LANES:ROW_W] = p[:, C_WIN + LANES:C_WIN + ROW_W]
    ab_ref[...] = p[:, C_AB:C_AB + LANES]
    gate_ref[...] = p[:, C_GATE:C_GATE + N_KV_HEADS * LANES]


def _proj(x, g, w, tabs, tm):
    m = x.shape[0]
    nt = tabs[0].shape[0] // tm
    row = lambda i: (i, 0)
    tab = lambda i: (i % nt, 0)
    fix = lambda i: (0, 0)
    out_shapes = (
        jax.ShapeDtypeStruct((m, 3 * GDN_W), F32),
        jax.ShapeDtypeStruct((m, GDN_W), F32),
        jax.ShapeDtypeStruct((m, NSA_W), F32),
        jax.ShapeDtypeStruct((N_KV_HEADS, m, LANES), F32),
        jax.ShapeDtypeStruct((N_KV_HEADS, m, LANES), F32),
        jax.ShapeDtypeStruct((m, ROW_W), F32),
        jax.ShapeDtypeStruct((m, LANES), F32),
        jax.ShapeDtypeStruct((m, N_KV_HEADS * LANES), F32),
    )
    return pl.pallas_call(
        _proj_kernel,
        grid=(m // tm,),
        in_specs=[pl.BlockSpec((tm, D_MODEL), row), pl.BlockSpec((1, D_MODEL), fix),
                  pl.BlockSpec((D_MODEL, N_PROJ), fix)] + [pl.BlockSpec((tm, LANES), tab)] * 4,
        out_specs=(pl.BlockSpec((tm, 3 * GDN_W), row), pl.BlockSpec((tm, GDN_W), row),
                   pl.BlockSpec((tm, NSA_W), row), pl.BlockSpec((N_KV_HEADS, tm, LANES), lambda i: (0, i, 0)),
                   pl.BlockSpec((N_KV_HEADS, tm, LANES), lambda i: (0, i, 0)),
                   pl.BlockSpec((tm, ROW_W), row), pl.BlockSpec((tm, LANES), row),
                   pl.BlockSpec((tm, N_KV_HEADS * LANES), row)),
        out_shape=out_shapes,
        compiler_params=_cparams(("parallel",)),
        name="proj",
    )(x, g, w, *tabs)


def _gdn_prep_kernel(x_ref, prev_ref, cb_ref, ab_ref, cw_ref, alog_ref, dtb_ref,
                     qkv_ref, gb_ref, xs_ref, *, tt, t_valid, t_out):
    i = pl.program_id(1)
    xs_ref[SUBLANES:SUBLANES + tt, :] = x_ref[0]

    @pl.when(i == 0)
    def _():
        xs_ref[0:SUBLANES, :] = cb_ref[0]

    @pl.when(i > 0)
    def _():
        xs_ref[0:SUBLANES, :] = prev_ref[0]

    conv = xs_ref[SUBLANES:SUBLANES + tt, :] * cw_ref[CONV_W - 1:CONV_W, :]
    for k in range(1, CONV_W):
        conv = conv + xs_ref[SUBLANES - k:SUBLANES - k + tt, :] * cw_ref[CONV_W - 1 - k:CONV_W - k, :]
    c = conv * _sigmoid(conv)

    rows = lax.broadcasted_iota(jnp.int32, (tt, LANES), 0) + i * tt
    live = rows < t_valid
    r_i = lax.broadcasted_iota(jnp.int32, (LANES, LANES), 0)
    c_i = lax.broadcasted_iota(jnp.int32, (LANES, LANES), 1)
    head_ones = ((r_i >> HEAD_SHIFT) == (c_i >> HEAD_SHIFT)).astype(BF16)

    if t_out != tt:
        qkv_ref[...] = jnp.zeros(qkv_ref.shape, F32)
        gb_ref[...] = jnp.zeros(gb_ref.shape, F32)
    for j in range(3 * GDN_W // LANES):
        blk = c[:, j * LANES:(j + 1) * LANES]
        if j < 2 * GDN_W // LANES:
            sq = blk * blk
            hi = sq.astype(BF16)
            lo = (sq - hi.astype(F32)).astype(BF16)
            ss = _dot(hi, head_ones) + _dot(lo, head_ones)
            blk = blk * lax.rsqrt(ss + NORM_EPS)
            if j < GDN_W // LANES:
                blk = blk * QK_SCALE
        qkv_ref[0, 0:tt, j * LANES:(j + 1) * LANES] = jnp.where(live, blk, 0.0)

    ab = ab_ref[0]
    za = ab + dtb_ref[...]
    softplus = jnp.maximum(za, 0.0) + jnp.log(1.0 + jnp.exp(-jnp.abs(za)))
    gdec = -jnp.exp(alog_ref[...]) * softplus
    lane = lax.broadcasted_iota(jnp.int32, (tt, LANES), 1)
    gb = jnp.where(lane < N_GDN_HEADS, gdec, jnp.where(lane < 2 * N_GDN_HEADS, _sigmoid(ab), 0.0))
    gb_ref[0, 0:tt, :] = jnp.where(live, gb, 0.0)


def _gdn_prep(x3, cb8, ab3, cw, alog_row, dtb_row, tt, t_valid, t_out):
    b, t_in, _ = x3.shape
    n = t_in // tt
    blocks8 = tt // SUBLANES
    w3 = 3 * GDN_W
    kern = functools.partial(_gdn_prep_kernel, tt=tt, t_valid=t_valid, t_out=t_out)
    return pl.pallas_call(
        kern,
        grid=(b, n),
        in_specs=[pl.BlockSpec((1, tt, w3), lambda bi, i: (bi, i, 0)),
                  pl.BlockSpec((1, SUBLANES, w3), lambda bi, i: (bi, jnp.maximum(i * blocks8 - 1, 0), 0)),
                  pl.BlockSpec((1, SUBLANES, w3), lambda bi, i: (bi, 0, 0)),
                  pl.BlockSpec((1, tt, LANES), lambda bi, i: (bi, i, 0)),
                  pl.BlockSpec((CONV_W, w3), lambda bi, i: (0, 0)),
                  pl.BlockSpec((1, LANES), lambda bi, i: (0, 0)),
                  pl.BlockSpec((1, LANES), lambda bi, i: (0, 0))],
        out_specs=(pl.BlockSpec((1, t_out, w3), lambda bi, i: (bi, i, 0)),
                   pl.BlockSpec((1, t_out, LANES), lambda bi, i: (bi, i, 0))),
        out_shape=(jax.ShapeDtypeStruct((b, n * t_out, w3), F32),
                   jax.ShapeDtypeStruct((b, n * t_out, LANES), F32)),
        scratch_shapes=[pltpu.VMEM((tt + SUBLANES, w3), F32)],
        compiler_params=_cparams(("parallel", "arbitrary")),
        name="gdn_prep",
    )(x3, x3, cb8, ab3, cw, alog_row, dtb_row)


def _gdn_scan_kernel(qkv_ref, gb_ref, z_ref, s0_ref, gn_ref, o_ref, s_ref, *, c, g):
    ci = pl.program_id(1)
    gc = g * c
    gs = g * HEAD_DIM
    n_groups = N_GDN_HEADS // g
    c_shift = int(math.log2(c))

    @pl.when(ci == 0)
    def _():
        s_ref[...] = s0_ref[...]

    def iota(shape, dim):
        return lax.broadcasted_iota(jnp.int32, shape, dim)

    gb = gb_ref[0]
    gcum = jnp.dot(jnp.where(iota((c, c), 0) >= iota((c, c), 1), 1.0, 0.0), gb,
                   preferred_element_type=F32, precision=HIGHEST)
    r_i, c_i = iota((gc, gc), 0), iota((gc, gc), 1)
    same = (r_i >> c_shift) == (c_i >> c_shift)
    incl = same & (r_i >= c_i)
    strict = same & (r_i > c_i)
    eye = jnp.where(r_i == c_i, 1.0, 0.0)
    tri = iota((c, gc), 0) <= (iota((c, gc), 1) & (c - 1))
    same_s = (iota((gc, gs), 0) >> c_shift) == (iota((gc, gs), 1) >> HEAD_SHIFT)
    same_s2 = jnp.concatenate([same_s, same_s], axis=0)
    gnorm = gn_ref[...]

    def stack(pieces):
        return jnp.concatenate(pieces, axis=0)

    groups = []
    for gi in range(n_groups):
        heads = range(gi * g, (gi + 1) * g)
        lane = lambda base, h: slice(base + h * HEAD_DIM, base + (h + 1) * HEAD_DIM)
        q_st = stack([qkv_ref[0, :, lane(0, h)] for h in heads])
        k_st = stack([qkv_ref[0, :, lane(GDN_W, h)] for h in heads])
        v_st = stack([qkv_ref[0, :, lane(2 * GDN_W, h)] for h in heads])
        gc_st = stack([gcum[:, h:h + 1] for h in heads])
        bt_st = stack([gb[:, N_GDN_HEADS + h:N_GDN_HEADS + h + 1] for h in heads])
        gl_st = stack([jnp.broadcast_to(gcum[c - 1:c, h:h + 1], (c, 1)) for h in heads])
        gl_s = stack([jnp.broadcast_to(gcum[c - 1:c, h:h + 1], (HEAD_DIM, 1)) for h in heads])
        expand = jnp.where(iota((LANES, gc), 0) == (iota((LANES, gc), 1) >> c_shift) + gi * g, 1.0, 0.0)
        spread = jnp.dot(gb, expand, preferred_element_type=F32, precision=HIGHEST)
        gr_st = jnp.sum(jnp.where(tri, spread, 0.0), axis=0, keepdims=True)
        decay = jnp.exp(jnp.where(incl, gc_st - gr_st, -jnp.inf))
        k16 = k_st.astype(BF16)
        qk_kk = _dot_nt(jnp.concatenate([q_st, k_st], axis=0).astype(BF16), k16)
        a_mat = jnp.where(strict, bt_st * qk_kk[gc:2 * gc] * decay, 0.0)
        groups.append(dict(q=q_st, k=k_st, v=v_st, gc=gc_st, bt=bt_st, gl=gl_st, gl_s=gl_s,
                           qk=qk_kk[0:gc] * decay, t=eye - a_mat, p=a_mat, heads=heads))

    for _ in range(c_shift - 1):
        for gr in groups:
            p16 = gr['p'].astype(BF16)
            gr['p'] = _dot(p16, p16)
        for gr in groups:
            gr['t'] = gr['t'] + _dot(gr['t'].astype(BF16), gr['p'].astype(BF16))

    for gi, gr in enumerate(groups):
        eg = jnp.exp(gr['gc'])
        rhs = jnp.concatenate([gr['k'] * (gr['bt'] * eg), gr['v'] * gr['bt']], axis=1).astype(BF16)
        gr['wu'] = _dot(gr['t'].astype(BF16), rhs)
        gr['qe'] = gr['q'] * eg
    for gi, gr in enumerate(groups):
        s_rows = slice(gi * gs, (gi + 1) * gs)
        s = s_ref[0, s_rows, :]
        wq = jnp.concatenate([gr['wu'][:, 0:HEAD_DIM], gr['qe']], axis=0)
        wq_bd = jnp.where(same_s2, jnp.concatenate([wq] * g, axis=1), 0.0).astype(BF16)
        ws_qs = _dot(wq_bd, s.astype(BF16))
        u = gr['wu'][:, HEAD_DIM:2 * HEAD_DIM] - ws_qs[0:gc]
        u16 = u.astype(BF16)
        o = ws_qs[gc:2 * gc] + _dot(gr['qk'].astype(BF16), u16)
        kd = gr['k'] * jnp.exp(gr['gl'] - gr['gc'])
        kd_bd = jnp.where(same_s, jnp.concatenate([kd] * g, axis=1), 0.0).astype(BF16)
        s_ref[0, s_rows, :] = s * jnp.exp(gr['gl_s']) + _dot_tn(kd_bd, u16)
        on = o * lax.rsqrt(jnp.mean(o * o, axis=-1, keepdims=True) + NORM_EPS) * gnorm
        for j, h in enumerate(gr['heads']):
            sl = slice(h * HEAD_DIM, (h + 1) * HEAD_DIM)
            zh = z_ref[0, :, sl]
            o_ref[0, :, sl] = on[j * c:(j + 1) * c] * (zh * _sigmoid(zh))


def _gdn_scan(qkv3, gb3, z3, s0, gnorm, c, g):
    b, tp, _ = qkv3.shape
    n = tp // c
    kern = functools.partial(_gdn_scan_kernel, c=c, g=g)
    state_rows = N_GDN_HEADS * HEAD_DIM
    state_spec = pl.BlockSpec((1, state_rows, HEAD_DIM), lambda bi, i: (bi, 0, 0))
    o, s_new = pl.pallas_call(
        kern,
        grid=(b, n),
        in_specs=[pl.BlockSpec((1, c, 3 * GDN_W), lambda bi, i: (bi, i, 0)),
                  pl.BlockSpec((1, c, LANES), lambda bi, i: (bi, i, 0)),
                  pl.BlockSpec((1, c, GDN_W), lambda bi, i: (bi, i, 0)),
                  state_spec,
                  pl.BlockSpec((1, HEAD_DIM), lambda bi, i: (0, 0))],
        out_specs=(pl.BlockSpec((1, c, GDN_W), lambda bi, i: (bi, i, 0)), state_spec),
        out_shape=(jax.ShapeDtypeStruct((b, tp, GDN_W), F32),
                   jax.ShapeDtypeStruct((b, state_rows, HEAD_DIM), F32)),
        compiler_params=_cparams(("parallel", "arbitrary")),
        name="gdn_scan",
    )(qkv3, gb3, z3, s0.reshape(b, state_rows, HEAD_DIM), gnorm)
    return o, s_new.reshape(b, N_GDN_HEADS, HEAD_DIM, HEAD_DIM)


def _compress(readers, pe_ref, wbd_ref, cc, cs):
    acc = None
    for c in range(CMP_BLOCK):
        rows = jnp.concatenate([rd(parity * CMP_BLOCK + c) for rd in readers for parity in range(2)], axis=0)
        part = _dot((rows + pe_ref[c:c + 1, :]).astype(BF16), wbd_ref[c])
        acc = part if acc is None else acc + part
    return _rope128(acc, cc, cs)


def _softmax_rows(s, mask):
    s = jnp.where(mask, s, -jnp.inf)
    m = jnp.max(s, axis=-1, keepdims=True)
    m = jnp.where(m > -jnp.inf, m, 0.0)
    e = jnp.exp(s - m)
    return e / jnp.maximum(jnp.sum(e, axis=-1, keepdims=True), 1e-30)


def _select_blocks(imp, qpos, n_sel):
    tq = imp.shape[0]
    blk = lax.broadcasted_iota(jnp.int32, (tq, N_SCORE), 1)
    cur = qpos >> HEAD_SHIFT
    forced = (blk == 0) | (blk == cur) | (blk == cur - 1)
    valid = (blk * SEL_BLOCK <= qpos) & (blk < n_sel)
    score = jnp.where(forced, jnp.inf, jnp.where(valid, imp, -jnp.inf))
    rank = jnp.zeros((tq, N_SCORE), F32)
    for i in range(min(n_sel, N_SCORE)):
        si = score[:, i:i + 1]
        ahead = (si > score) | ((si == score) & (blk > i))
        rank = rank + jnp.where(ahead, 1.0, 0.0)
    return ((rank < TOP_K) & (blk < n_sel)).astype(BF16)


def _nsa_core(q_blk, qpos_t, ckv, n_half, sel_chunk, n_chunks, kc, n_sel,
              win_rows, win_pos0, kvh, gates):
    tq = q_blk.shape[0]
    r = GQA_GROUP * tq
    q_all = jnp.concatenate([q_blk[:, g * HEAD_DIM:(g + 1) * HEAD_DIM] for g in range(GQA_GROUP)], axis=0)
    qpos = jnp.concatenate([qpos_t] * GQA_GROUP, axis=0)
    zeros64 = jnp.zeros((r, HEAD_DIM), F32)

    q_c = jnp.concatenate([q_all, zeros64], axis=1).astype(BF16)
    ckv16 = ckv.astype(BF16)
    nc = 2 * n_half
    col = lax.broadcasted_iota(jnp.int32, (r, nc), 1)
    cmp_end = (2 * (col & (n_half - 1)) + (col >> int(math.log2(n_half))) + 1) * CMP_BLOCK - 1
    p_c = _softmax_rows(_dot_nt(q_c, ckv16), cmp_end <= qpos)
    o_c = _dot(p_c.astype(BF16), ckv16)[:, HEAD_DIM:2 * HEAD_DIM]
    pair = p_c[:, 0:n_half] + p_c[:, n_half:nc]
    imp = pair[0:tq]
    for g in range(1, GQA_GROUP):
        imp = imp + pair[g * tq:(g + 1) * tq]
    if n_half < N_SCORE:
        imp = jnp.concatenate([imp, jnp.zeros((tq, N_SCORE - n_half), F32)], axis=1)
    sel = _select_blocks(imp, qpos_t, n_sel)

    j_i = lax.broadcasted_iota(jnp.int32, (N_SCORE, kc), 0)
    k_i = lax.broadcasted_iota(jnp.int32, (N_SCORE, kc), 1) >> HEAD_SHIFT
    kcol = lax.broadcasted_iota(jnp.int32, (r, kc), 1)

    def body(c, carry):
        m, l, acc = carry
        rows = sel_chunk(c)
        s = _dot_nt(q_c, rows)
        expand = (j_i == k_i + c * (kc // SEL_BLOCK)).astype(BF16)
        picked = _dot(sel, expand)
        picked = jnp.concatenate([picked] * GQA_GROUP, axis=0)
        mask = (picked > 0.5) & (kcol + c * kc <= qpos)
        s = jnp.where(mask, s, -jnp.inf)
        m_new = jnp.maximum(m, jnp.max(s, axis=-1, keepdims=True))
        m_safe = jnp.where(m_new > -jnp.inf, m_new, 0.0)
        alpha = jnp.exp(m - m_safe)
        p = jnp.exp(s - m_safe)
        l = alpha * l + jnp.sum(p, axis=-1, keepdims=True)
        acc = alpha * acc + _dot(p.astype(BF16), rows)
        return m_new, l, acc

    init = (jnp.full((r, 1), -jnp.inf, F32), jnp.zeros((r, 1), F32), jnp.zeros((r, LANES), F32))
    _, l_s, acc_s = lax.fori_loop(0, n_chunks, body, init)
    o_s = acc_s[:, HEAD_DIM:2 * HEAD_DIM] / jnp.maximum(l_s, 1e-30)

    lane = lax.broadcasted_iota(jnp.int32, (r, ROW_W), 1)
    q_w = jnp.where((lane >> HEAD_SHIFT) == kvh, jnp.concatenate([q_all, q_all, zeros64, zeros64], axis=1), 0.0)
    q_w = q_w.astype(BF16)
    nw = win_rows.shape[0]
    kpos = lax.broadcasted_iota(jnp.int32, (r, nw), 1) + win_pos0
    p_w = _softmax_rows(_dot_nt(q_w, win_rows), (kpos <= qpos) & (kpos >= qpos - WINDOW))
    o_w2 = _dot(p_w.astype(BF16), win_rows)
    o_w = jnp.where(kvh == 0, o_w2[:, 2 * HEAD_DIM:3 * HEAD_DIM], o_w2[:, 3 * HEAD_DIM:4 * HEAD_DIM])

    outs = []
    for g in range(GQA_GROUP):
        rows_g = slice(g * tq, (g + 1) * tq)
        g0 = _sigmoid(gates[:, 3 * g:3 * g + 1])
        g1 = _sigmoid(gates[:, 3 * g + 1:3 * g + 2])
        g2 = _sigmoid(gates[:, 3 * g + 2:3 * g + 3])
        outs.append(g0 * o_c[rows_g] + g1 * o_s[rows_g] + g2 * o_w[rows_g])
    return outs


def _nsa_prompt_kernel(q_ref, kc_ref, ks_ref, win_ref, gate_ref, wbd_ref, pe_ref, cc_ref, cs_ref,
                       o_ref, ckv_ref, *, tq, t, kc):
    kvh = pl.program_id(1)
    i = pl.program_id(2)
    n_half = t // (2 * CMP_BLOCK)

    @pl.when(i == 0)
    def _():
        read = lambda start: kc_ref[0, pl.ds(start, n_half, stride=2 * CMP_BLOCK), :]
        ckv_ref[...] = _compress([read], pe_ref, wbd_ref, cc_ref[...], cs_ref[...])

    t0 = i * tq
    qpos_t = lax.broadcasted_iota(jnp.int32, (tq, 1), 0) + t0
    sel_chunk = lambda c: ks_ref[0, pl.ds(pl.multiple_of(c * kc, kc), kc), :].astype(BF16)
    n_chunks = (t0 + tq + kc - 1) // kc
    span = WINDOW + tq
    start = pl.multiple_of(jnp.maximum(t0 - WINDOW, 0), tq)
    win_rows = win_ref[pl.ds(start, span), :].astype(BF16)
    outs = _nsa_core(q_ref[...], qpos_t, ckv_ref[...], n_half, sel_chunk, n_chunks, kc, t // SEL_BLOCK,
                     win_rows, start, kvh, gate_ref[...])
    for g in range(GQA_GROUP):
        o_ref[:, g * HEAD_DIM:(g + 1) * HEAD_DIM] = outs[g]


def _nsa_prompt(q, kvc, kvs, win, gate, wbd, pe, cc, cs, b, t, tq=128, kc=512):
    n = t // tq
    nc = t // CMP_BLOCK
    kern = functools.partial(_nsa_prompt_kernel, tq=tq, t=t, kc=kc)
    fix2 = lambda bi, h, i: (0, 0)
    return pl.pallas_call(
        kern,
        grid=(b, N_KV_HEADS, n),
        in_specs=[pl.BlockSpec((tq, ROW_W), lambda bi, h, i: (bi * n + i, h)),
                  pl.BlockSpec((1, t, LANES), lambda bi, h, i: (h, bi, 0)),
                  pl.BlockSpec((1, t, LANES), lambda bi, h, i: (h, bi, 0)),
                  pl.BlockSpec((t, ROW_W), lambda bi, h, i: (bi, 0)),
                  pl.BlockSpec((tq, LANES), lambda bi, h, i: (bi * n + i, h)),
                  pl.BlockSpec((CMP_BLOCK, LANES, LANES), lambda bi, h, i: (0, 0, 0)),
                  pl.BlockSpec((CMP_BLOCK, LANES), fix2),
                  pl.BlockSpec((nc, LANES), fix2),
                  pl.BlockSpec((nc, LANES), fix2)],
        out_specs=pl.BlockSpec((tq, ROW_W), lambda bi, h, i: (bi * n + i, h)),
        out_shape=jax.ShapeDtypeStruct((b * t, NSA_W), F32),
        scratch_shapes=[pltpu.VMEM((nc, LANES), F32)],
        compiler_params=_cparams(("parallel", "parallel", "arbitrary")),
        name="nsa_prompt",
    )(q, kvc, kvs, win, gate, wbd, pe, cc, cs)


def _nsa_sample_kernel(pt_ref, *refs, n_steps, pages_per_step, past, tq, ts, n_pad, nw):
    page_refs = refs[:pages_per_step]
    (q_ref, kvnew_ref, wbuf_ref, wnew_ref, gate_ref, wbd_ref, pe_ref, cc_ref, cs_ref,
     o_ref, wout_ref, pgc_ref, pgs_ref, wn_ref) = refs[pages_per_step:]
    j = pl.program_id(1)
    for k, page_ref in enumerate(page_refs):
        rows = pl.ds(pl.multiple_of((j * pages_per_step + k) * PAGE_SIZE, PAGE_SIZE), PAGE_SIZE)
        for kvh in range(N_KV_HEADS):
            pgc_ref[kvh, rows, :] = page_ref[0, 0, kvh, 0:LANES, :].T
            pgs_ref[kvh, rows, :] = page_ref[0, 0, kvh, LANES:ROW_W, :].T

    @pl.when(j == n_steps - 1)
    def _():
        n_half = past // (2 * CMP_BLOCK)
        wbuf_t = wbuf_ref[0]
        wnew = wnew_ref[0]
        wn_ref[0:WINDOW, :] = wbuf_t.T
        wn_ref[WINDOW:WINDOW + tq, :] = wnew
        wn_ref[WINDOW + tq:nw, :] = jnp.zeros((nw - WINDOW - tq, ROW_W), F32)
        win_rows = wn_ref[...].astype(BF16)

        shifted = pltpu.roll(wbuf_t, WINDOW - ts, 1)
        new_t = jnp.concatenate([wnew, jnp.zeros((LANES - tq, ROW_W), F32)], axis=0).T
        lane = lax.broadcasted_iota(jnp.int32, (ROW_W, LANES), 1)
        tail = jnp.where(lane >= LANES - ts, pltpu.roll(new_t, LANES - ts, 1), shifted[:, WINDOW - LANES:WINDOW])
        wout_ref[0, :, 0:WINDOW - LANES] = shifted[:, 0:WINDOW - LANES]
        wout_ref[0, :, WINDOW - LANES:WINDOW] = tail

        qpos_t = lax.broadcasted_iota(jnp.int32, (tq, 1), 0) + past
        readers = []
        for kvh in range(N_KV_HEADS):
            pgs_ref[kvh, past:past + tq, :] = kvnew_ref[kvh, 0]
            pgs_ref[kvh, past + tq:n_pad, :] = jnp.zeros((n_pad - past - tq, LANES), F32)
            readers.append(lambda start, kvh=kvh: pgc_ref[kvh, pl.ds(start, n_half, stride=2 * CMP_BLOCK), :])
        ckv2 = _compress(readers, pe_ref, wbd_ref, cc_ref[...], cs_ref[...])
        for kvh in range(N_KV_HEADS):
            ckv = ckv2[kvh * 2 * n_half:(kvh + 1) * 2 * n_half]
            sel_chunk = lambda c, kvh=kvh: pgs_ref[kvh].astype(BF16)
            outs = _nsa_core(q_ref[0, :, kvh * ROW_W:(kvh + 1) * ROW_W], qpos_t, ckv, n_half, sel_chunk, 1,
                             n_pad, past // SEL_BLOCK + 1, win_rows, past - WINDOW, kvh,
                             gate_ref[0, :, kvh * LANES:(kvh + 1) * LANES])
            for g in range(GQA_GROUP):
                col = kvh * ROW_W + g * HEAD_DIM
                o_ref[0, :, col:col + HEAD_DIM] = outs[g]


def _nsa_sample(page_table, pool_t, layer, ts, q8, kvnew8, wbuf_t, wnew8, gate8, wbd, pe, cc, cs, pages_per_step=4):
    b, n_pages = page_table.shape
    past = n_pages * PAGE_SIZE
    tq = q8.shape[1]
    n_pad = past + SEL_BLOCK
    nw = WINDOW + LANES
    nc2 = N_KV_HEADS * (past // CMP_BLOCK)
    n_steps = n_pages // pages_per_step
    kern = functools.partial(_nsa_sample_kernel, n_steps=n_steps, pages_per_step=pages_per_step, past=past,
                             tq=tq, ts=ts, n_pad=n_pad, nw=nw)
    per_b = lambda bi, j, pt: (bi, 0, 0)
    fix2 = lambda bi, j, pt: (0, 0)
    page_spec = lambda k: pl.BlockSpec((1, 1, N_KV_HEADS, ROW_W, PAGE_SIZE),
                                       lambda bi, j, pt: (layer, pt[bi, j * pages_per_step + k], 0, 0, 0))
    grid_spec = pltpu.PrefetchScalarGridSpec(
        num_scalar_prefetch=1,
        grid=(b, n_steps),
        in_specs=[page_spec(k) for k in range(pages_per_step)] + [
                  pl.BlockSpec((1, tq, NSA_W), per_b),
                  pl.BlockSpec((N_KV_HEADS, 1, tq, LANES), lambda bi, j, pt: (0, bi, 0, 0)),
                  pl.BlockSpec((1, ROW_W, WINDOW), per_b),
                  pl.BlockSpec((1, tq, ROW_W), per_b),
                  pl.BlockSpec((1, tq, N_KV_HEADS * LANES), per_b),
                  pl.BlockSpec((CMP_BLOCK, LANES, LANES), lambda bi, j, pt: (0, 0, 0)),
                  pl.BlockSpec((CMP_BLOCK, LANES), fix2),
                  pl.BlockSpec((nc2, LANES), fix2),
                  pl.BlockSpec((nc2, LANES), fix2)],
        out_specs=(pl.BlockSpec((1, tq, NSA_W), per_b), pl.BlockSpec((1, ROW_W, WINDOW), per_b)),
        scratch_shapes=[pltpu.VMEM((N_KV_HEADS, past, LANES), F32), pltpu.VMEM((N_KV_HEADS, n_pad, LANES), F32),
                        pltpu.VMEM((nw, ROW_W), F32)],
    )
    return pl.pallas_call(
        kern,
        grid_spec=grid_spec,
        out_shape=(jax.ShapeDtypeStruct((b, tq, NSA_W), F32), jax.ShapeDtypeStruct((b, ROW_W, WINDOW), F32)),
        compiler_params=_cparams(("parallel", "arbitrary")),
        name="nsa_sample",
    )(page_table, *([pool_t] * pages_per_step), q8, kvnew8, wbuf_t, wnew8, gate8, wbd, pe, cc, cs)


def _mlp_kernel(x_ref, og_ref, on_ref, wo_ref, gm_ref, wu_ref, wd_ref, gf_ref, y_ref, x1_ref, h_ref, acc_ref,
                *, final_norm):
    f = pl.program_id(1)

    @pl.when(f == 0)
    def _():
        x1 = (x_ref[...] + _dot(og_ref[...].astype(BF16), wo_ref[0:GDN_W, :])
              + _dot(on_ref[...].astype(BF16), wo_ref[GDN_W:GDN_W + NSA_W, :]))
        x1_ref[...] = x1
        var = jnp.mean(x1 * x1, axis=-1, keepdims=True)
        h_ref[...] = (x1 * lax.rsqrt(var + NORM_EPS) * gm_ref[...]).astype(BF16)
        acc_ref[...] = jnp.zeros(acc_ref.shape, F32)

    up = jnp.maximum(_dot(h_ref[...], wu_ref[...]), 0.0)
    acc_ref[...] += _dot((up * up).astype(BF16), wd_ref[...])

    @pl.when(f == pl.num_programs(1) - 1)
    def _():
        y = x1_ref[...] + acc_ref[...]
        if final_norm:
            var = jnp.mean(y * y, axis=-1, keepdims=True)
            y = y * lax.rsqrt(var + NORM_EPS) * gf_ref[...]
        y_ref[...] = y


def _mlp(x, og, on, wo, gm, wu, wd, gf, final_norm, tm, tf=1024):
    m = x.shape[0]
    kern = functools.partial(_mlp_kernel, final_norm=final_norm)
    row = lambda i, f: (i, 0)
    fix = lambda i, f: (0, 0)
    return pl.pallas_call(
        kern,
        grid=(m // tm, D_FF // tf),
        in_specs=[pl.BlockSpec((tm, D_MODEL), row), pl.BlockSpec((tm, GDN_W), row), pl.BlockSpec((tm, NSA_W), row),
                  pl.BlockSpec((D_MODEL, D_MODEL), fix), pl.BlockSpec((1, D_MODEL), fix),
                  pl.BlockSpec((D_MODEL, tf), lambda i, f: (0, f)), pl.BlockSpec((tf, D_MODEL), lambda i, f: (f, 0)),
                  pl.BlockSpec((1, D_MODEL), fix)],
        out_specs=pl.BlockSpec((tm, D_MODEL), row),
        out_shape=jax.ShapeDtypeStruct((m, D_MODEL), F32),
        scratch_shapes=[pltpu.VMEM((tm, D_MODEL), F32), pltpu.VMEM((tm, D_MODEL), BF16),
                        pltpu.VMEM((tm, D_MODEL), F32)],
        compiler_params=_cparams(("parallel", "arbitrary")),
        name="mlp",
    )(x, og, on, wo, gm, wu, wd, gf)


def _proj_columns():
    src = np.full((N_PROJ,), -1, np.int64)
    o_z, o_a, o_b = 3 * GDN_W, 4 * GDN_W, 4 * GDN_W + N_GDN_HEADS
    o_q = o_b + N_GDN_HEADS
    o_kv = o_q + NSA_W
    o_g = o_kv + 6 * KV_W
    src[C_QKV:C_Z] = np.arange(0, 3 * GDN_W)
    src[C_Z:C_Q] = np.arange(o_z, o_z + GDN_W)
    src[C_Q:C_KV] = np.arange(o_q, o_q + NSA_W)
    for kvh in range(N_KV_HEADS):
        for s in range(4):
            dst = C_KV + kvh * ROW_W + s * HEAD_DIM
            src[dst:dst + HEAD_DIM] = o_kv + s * KV_W + kvh * HEAD_DIM + np.arange(HEAD_DIM)
    src[C_WIN:C_WIN + 2 * KV_W] = o_kv + 4 * KV_W + np.arange(2 * KV_W)
    src[C_AB:C_AB + N_GDN_HEADS] = o_a + np.arange(N_GDN_HEADS)
    src[C_AB + N_GDN_HEADS:C_AB + 2 * N_GDN_HEADS] = o_b + np.arange(N_GDN_HEADS)
    for kvh in range(N_KV_HEADS):
        dst = C_GATE + kvh * LANES
        src[dst:dst + 3 * GQA_GROUP] = o_g + kvh * 3 * GQA_GROUP + np.arange(3 * GQA_GROUP)
    return src


def _rope_tables(pos):
    half = HEAD_DIM // 2
    inv_freq = ROPE_THETA ** (-jnp.arange(half, dtype=F32) / half)
    ang = pos.astype(F32)[:, None] * inv_freq[None, :]
    cos, sin = jnp.cos(ang), jnp.sin(ang)
    c64 = jnp.concatenate([cos, cos], axis=1)
    s64 = jnp.concatenate([-sin, sin], axis=1)
    one, zero = jnp.ones_like(c64), jnp.zeros_like(s64)
    return (jnp.concatenate([c64, c64], axis=1), jnp.concatenate([s64, s64], axis=1),
            jnp.concatenate([c64, one], axis=1), jnp.concatenate([s64, zero], axis=1))


def _cmp_tables(n_blocks):
    n_half = n_blocks // 2
    r = np.arange(n_blocks)
    blk = 2 * (r % n_half) + r // n_half
    end_pos = jnp.asarray((blk + 1) * CMP_BLOCK - 1)
    _, _, cc, cs = _rope_tables(end_pos)
    return cc, cs


def _layer_params(l, norm_mix, w_in, conv_w, a_log, dt_bias, gdn_norm, cmp_pe_k, cmp_w_k, cmp_pe_v, cmp_w_v,
                  w_out, norm_mlp, w_up, w_down):
    src = _proj_columns()
    w_ext = jnp.concatenate([w_in[l], jnp.zeros((D_MODEL, 1), F32)], axis=1)
    w_p = jnp.take(w_ext, jnp.asarray(np.where(src < 0, w_in.shape[2], src)), axis=1).astype(BF16)
    zeros = jnp.zeros((CMP_BLOCK, HEAD_DIM, HEAD_DIM), F32)
    wbd = jnp.concatenate([jnp.concatenate([cmp_w_k[l], zeros], axis=2),
                           jnp.concatenate([zeros, cmp_w_v[l]], axis=2)], axis=1).astype(BF16)
    pad_row = lambda v: jnp.pad(v.astype(F32), (0, LANES - v.shape[0]))[None, :]
    return dict(
        norm_mix=norm_mix[l][None, :], w_p=w_p, conv_w=conv_w[l], alog=pad_row(a_log[l]), dtb=pad_row(dt_bias[l]),
        gdn_norm=gdn_norm[l][None, :], wbd=wbd, pe=jnp.concatenate([cmp_pe_k[l], cmp_pe_v[l]], axis=1),
        w_out=w_out[l].astype(BF16), norm_mlp=norm_mlp[l][None, :], w_up=w_up[l].astype(BF16),
        w_down=w_down[l].astype(BF16))


def _prompt_layer(x, lp, tabs, cmp_tabs, gf, final_norm, b, t):
    qkv_pre, z, q, kvc, kvs, win, ab, gate = _proj(x, lp['norm_mix'], lp['w_p'], tabs, 256)
    w3 = 3 * GDN_W
    cb8 = jnp.zeros((b, SUBLANES, w3), F32)
    qkv3, gb3 = _gdn_prep(qkv_pre.reshape(b, t, w3), cb8, ab.reshape(b, t, LANES), lp['conv_w'], lp['alog'],
                          lp['dtb'], 256, t, 256)
    s0 = jnp.zeros((b, N_GDN_HEADS, HEAD_DIM, HEAD_DIM), F32)
    o_gdn, s_new = _gdn_scan(qkv3, gb3, z.reshape(b, t, GDN_W), s0, lp['gdn_norm'], GDN_CHUNK, 4)
    o_nsa = _nsa_prompt(q, kvc, kvs, win, gate, lp['wbd'], lp['pe'], cmp_tabs[0], cmp_tabs[1], b, t)
    y = _mlp(x, o_gdn.reshape(b * t, GDN_W), o_nsa, lp['w_out'], lp['norm_mlp'], lp['w_up'], lp['w_down'], gf,
             final_norm, 512)
    rows = jnp.concatenate([kvc, kvs], axis=2).reshape(N_KV_HEADS, b, t, 4, HEAD_DIM).transpose(1, 0, 2, 3, 4)
    win_new = win.reshape(b, t, 2, N_KV_HEADS, HEAD_DIM)[:, t - min(WINDOW, t):]
    conv_new = qkv_pre.reshape(b, t, w3)[:, t - (CONV_W - 1):]
    return y, rows, win_new, s_new, conv_new


def _pad_rows(a, n):
    return jnp.pad(a, ((0, 0), (0, n - a.shape[1]), (0, 0)))


def _sample_layer(x, lp, tabs, cmp_tabs, gf, final_norm, b, t, layer, pool_t, page_table, win_t, s0, conv_buf):
    m = b * t
    qkv_pre, z, q, kvc, kvs, win, ab, gate = _proj(x, lp['norm_mix'], lp['w_p'], tabs, 256)
    w3 = 3 * GDN_W
    tp = SUBLANES
    cb8 = jnp.pad(conv_buf, ((0, 0), (SUBLANES - (CONV_W - 1), 0), (0, 0)))
    qkv3, gb3 = _gdn_prep(_pad_rows(qkv_pre.reshape(b, t, w3), tp), cb8, _pad_rows(ab.reshape(b, t, LANES), tp),
                          lp['conv_w'], lp['alog'], lp['dtb'], tp, t, tp)
    o_gdn, s_new = _gdn_scan(qkv3, gb3, _pad_rows(z.reshape(b, t, GDN_W), tp), s0, lp['gdn_norm'], tp,
                             N_GDN_HEADS)
    o_gdn = o_gdn[:, :t].reshape(m, GDN_W)
    kvnew8 = jnp.pad(kvs.reshape(N_KV_HEADS, b, t, LANES), ((0, 0), (0, 0), (0, tp - t), (0, 0)))
    o_nsa, win_out_t = _nsa_sample(page_table, pool_t, layer, t, _pad_rows(q.reshape(b, t, NSA_W), tp), kvnew8,
                                   win_t[layer], _pad_rows(win.reshape(b, t, ROW_W), tp),
                                   _pad_rows(gate.reshape(b, t, N_KV_HEADS * LANES), tp), lp['wbd'], lp['pe'],
                                   cmp_tabs[0], cmp_tabs[1])
    o_nsa = o_nsa[:, :t].reshape(m, NSA_W)
    y = _mlp(x, o_gdn, o_nsa, lp['w_out'], lp['norm_mlp'], lp['w_up'], lp['w_down'], gf, final_norm, 512)
    rows = jnp.concatenate([kvc, kvs], axis=2).reshape(N_KV_HEADS, b, t, 4, HEAD_DIM).transpose(1, 0, 2, 3, 4)
    win_new = win_out_t.reshape(b, 2, N_KV_HEADS, HEAD_DIM, WINDOW).transpose(0, 4, 1, 2, 3)
    conv_new = jnp.concatenate([conv_buf, qkv_pre.reshape(b, t, w3)], axis=1)[:, t:]
    return y, rows, win_new, s_new, conv_new


def kernel(x_prompt, x_sample, cache_kv, page_table, state_win, state_gdn, state_conv, norm_mix, w_in, conv_w, a_log, dt_bias, gdn_norm, cmp_pe_k, cmp_w_k, cmp_pe_v, cmp_w_v, w_out, norm_mlp, w_up, w_down, norm_final):
    bp, tp_, _ = x_prompt.shape
    bs, ts, _ = x_sample.shape
    depth = cache_kv.shape[0]
    n_pages = page_table.shape[1]
    past = n_pages * PAGE_SIZE
    assert state_win.shape[2] == WINDOW and tp_ % 512 == 0 and ts <= SUBLANES and past % (2 * CMP_BLOCK) == 0

    tabs_p = _rope_tables(jnp.arange(tp_))
    tabs_s = tuple(jnp.tile(tb, (bs, 1)) for tb in _rope_tables(past + jnp.arange(ts)))
    cmp_p = _cmp_tables(tp_ // CMP_BLOCK)
    cmp_s = tuple(jnp.tile(tb, (N_KV_HEADS, 1)) for tb in _cmp_tables((past + ts) // CMP_BLOCK))
    pool_t = cache_kv.transpose(0, 1, 2, 4, 5, 3).reshape(depth, cache_kv.shape[1], N_KV_HEADS, ROW_W, PAGE_SIZE)
    win_t = state_win.transpose(0, 1, 3, 4, 5, 2).reshape(depth, bs, ROW_W, WINDOW)
    gf = norm_final[None, :]

    xp = x_prompt.reshape(bp * tp_, D_MODEL)
    xs = x_sample.reshape(bs * ts, D_MODEL)
    outs = [[] for _ in range(8)]
    for l in range(depth):
        lp = _layer_params(l, norm_mix, w_in, conv_w, a_log, dt_bias, gdn_norm, cmp_pe_k, cmp_w_k, cmp_pe_v,
                           cmp_w_v, w_out, norm_mlp, w_up, w_down)
        last = l == depth - 1
        xp, r, w, s, c = _prompt_layer(xp, lp, tabs_p, cmp_p, gf, last, bp, tp_)
        for k, v in zip((0, 2, 4, 6), (r, w, s, c)):
            outs[k].append(v)
        xs, r, w, s, c = _sample_layer(xs, lp, tabs_s, cmp_s, gf, last, bs, ts, l, pool_t, page_table,
                                       win_t, state_gdn[l], state_conv[l])
        for k, v in zip((1, 3, 5, 7), (r, w, s, c)):
            outs[k].append(v)
    return (xp.reshape(bp, tp_, D_MODEL), xs.reshape(bs, ts, D_MODEL)) + tuple(jnp.stack(o) for o in outs)
```

```python
import functools
import math

import numpy as np
import jax
import jax.numpy as jnp
from jax import lax
from jax.experimental import pallas as pl
from jax.experimental.pallas import tpu as pltpu

F32 = jnp.float32
BF16 = jnp.bfloat16
HIGHEST = lax.Precision.HIGHEST

D_MODEL = 1024
HEAD_DIM = 64
N_GDN_HEADS = 8
GDN_W = N_GDN_HEADS * HEAD_DIM
N_NSA_HEADS = 8
NSA_W = N_NSA_HEADS * HEAD_DIM
N_KV_HEADS = 2
GQA_GROUP = N_NSA_HEADS // N_KV_HEADS
KV_W = N_KV_HEADS * HEAD_DIM
CONV_W = 4
GDN_CHUNK = 64
CMP_BLOCK = 32
SEL_BLOCK = 64
TOP_K = 16
WINDOW = 512
D_FF = 4 * D_MODEL
ROPE_THETA = 10000.0
NORM_EPS = 1e-6
PAGE_SIZE = 128
N_SCORE = 64
HEAD_SHIFT = 6
INV_BASE_SHIFT = 3
QK_SCALE = HEAD_DIM ** -0.5

LANES = 128
SUBLANES = 8
VMEM_LIMIT = 56 * 1024 * 1024

C_QKV = 0
C_Z = 3 * GDN_W
C_Q = C_Z + GDN_W
C_KV = C_Q + NSA_W
C_WIN = C_KV + N_KV_HEADS * 4 * HEAD_DIM
C_AB = C_WIN + 2 * KV_W
C_GATE = C_AB + LANES
N_PROJ = C_GATE + N_KV_HEADS * LANES
ROW_W = 4 * HEAD_DIM


def _cparams(sem):
    return pltpu.CompilerParams(dimension_semantics=sem, vmem_limit_bytes=VMEM_LIMIT)


def _sigmoid(x):
    return 1.0 / (1.0 + jnp.exp(-x))


def _dot(a, b):
    return jnp.dot(a, b, preferred_element_type=F32)


def _dot_nt(a, b):
    return lax.dot_general(a, b, (((1,), (1,)), ((), ())), preferred_element_type=F32)


def _dot_tn(a, b, precision=None):
    return lax.dot_general(a, b, (((0,), (0,)), ((), ())), preferred_element_type=F32, precision=precision)


def _rope128(v, cos, sin):
    lane = lax.broadcasted_iota(jnp.int32, v.shape, 1)
    first = (lane & (HEAD_DIM - 1)) < (HEAD_DIM // 2)
    swapped = jnp.where(first, pltpu.roll(v, LANES - HEAD_DIM // 2, 1), pltpu.roll(v, HEAD_DIM // 2, 1))
    return v * cos + swapped * sin


def _proj_kernel(x_ref, g_ref, w_ref, cf_ref, sf_ref, ch_ref, sh_ref,
                 qkv_ref, z_ref, q_ref, kc_ref, ks_ref, win_ref, ab_ref, gate_ref):
    x = x_ref[...]
    var = jnp.mean(x * x, axis=-1, keepdims=True)
    h = (x * lax.rsqrt(var + NORM_EPS) * g_ref[...]).astype(BF16)
    p = _dot(h, w_ref[...])
    qkv_ref[...] = p[:, C_QKV:C_Z]
    z_ref[...] = p[:, C_Z:C_Q]
    cf, sf, ch, sh = cf_ref[...], sf_ref[...], ch_ref[...], sh_ref[...]
    for j in range(NSA_W // LANES):
        q_ref[:, j * LANES:(j + 1) * LANES] = _rope128(p[:, C_Q + j * LANES:C_Q + (j + 1) * LANES], cf, sf) * QK_SCALE
    for kvh in range(N_KV_HEADS):
        base = C_KV + kvh * ROW_W
        kc_ref[kvh] = p[:, base:base + LANES]
        ks_ref[kvh] = _rope128(p[:, base + LANES:base + ROW_W], ch, sh)
    win_ref[:, 0:LANES] = _rope128(p[:, C_WIN:C_WIN + LANES], cf, sf)
    win_ref[:, LANES:ROW_W] = p[:, C_WIN + LANES:C_WIN + ROW_W]
    ab_ref[...] = p[:, C_AB:C_AB + LANES]
    gate_ref[...] = p[:, C_GATE:C_GATE + N_KV_HEADS * LANES]


def _proj(x, g, w, tabs, tm):
    m = x.shape[0]
    nt = tabs[0].shape[0] // tm
    row = lambda i: (i, 0)
    tab = lambda i: (i % nt, 0)
    fix = lambda i: (0, 0)
    out_shapes = (
        jax.ShapeDtypeStruct((m, 3 * GDN_W), F32),
        jax.ShapeDtypeStruct((m, GDN_W), F32),
        jax.ShapeDtypeStruct((m, NSA_W), F32),
        jax.ShapeDtypeStruct((N_KV_HEADS, m, LANES), F32),
        jax.ShapeDtypeStruct((N_KV_HEADS, m, LANES), F32),
        jax.ShapeDtypeStruct((m, ROW_W), F32),
        jax.ShapeDtypeStruct((m, LANES), F32),
        jax.ShapeDtypeStruct((m, N_KV_HEADS * LANES), F32),
    )
    return pl.pallas_call(
        _proj_kernel,
        grid=(m // tm,),
        in_specs=[pl.BlockSpec((tm, D_MODEL), row), pl.BlockSpec((1, D_MODEL), fix),
                  pl.BlockSpec((D_MODEL, N_PROJ), fix)] + [pl.BlockSpec((tm, LANES), tab)] * 4,
        out_specs=(pl.BlockSpec((tm, 3 * GDN_W), row), pl.BlockSpec((tm, GDN_W), row),
                   pl.BlockSpec((tm, NSA_W), row), pl.BlockSpec((N_KV_HEADS, tm, LANES), lambda i: (0, i, 0)),
                   pl.BlockSpec((N_KV_HEADS, tm, LANES), lambda i: (0, i, 0)),
                   pl.BlockSpec((tm, ROW_W), row), pl.BlockSpec((tm, LANES), row),
                   pl.BlockSpec((tm, N_KV_HEADS * LANES), row)),
        out_shape=out_shapes,
        compiler_params=_cparams(("parallel",)),
        name="proj",
    )(x, g, w, *tabs)


def _gdn_prep_kernel(x_ref, prev_ref, cb_ref, ab_ref, cw_ref, alog_ref, dtb_ref,
                     qkv_ref, gb_ref, xs_ref, *, tt, t_valid, t_out):
    i = pl.program_id(1)
    xs_ref[SUBLANES:SUBLANES + tt, :] = x_ref[0]

    @pl.when(i == 0)
    def _():
        xs_ref[0:SUBLANES, :] = cb_ref[0]

    @pl.when(i > 0)
    def _():
        xs_ref[0:SUBLANES, :] = prev_ref[0]

    conv = xs_ref[SUBLANES:SUBLANES + tt, :] * cw_ref[CONV_W - 1:CONV_W, :]
    for k in range(1, CONV_W):
        conv = conv + xs_ref[SUBLANES - k:SUBLANES - k + tt, :] * cw_ref[CONV_W - 1 - k:CONV_W - k, :]
    c = conv * _sigmoid(conv)

    rows = lax.broadcasted_iota(jnp.int32, (tt, LANES), 0) + i * tt
    live = rows < t_valid
    r_i = lax.broadcasted_iota(jnp.int32, (LANES, LANES), 0)
    c_i = lax.broadcasted_iota(jnp.int32, (LANES, LANES), 1)
    head_ones = ((r_i >> HEAD_SHIFT) == (c_i >> HEAD_SHIFT)).astype(BF16)

    if t_out != tt:
        qkv_ref[...] = jnp.zeros(qkv_ref.shape, F32)
        gb_ref[...] = jnp.zeros(gb_ref.shape, F32)
    for j in range(3 * GDN_W // LANES):
        blk = c[:, j * LANES:(j + 1) * LANES]
        if j < 2 * GDN_W // LANES:
            sq = blk * blk
            hi = sq.astype(BF16)
            lo = (sq - hi.astype(F32)).astype(BF16)
            ss = _dot(hi, head_ones) + _dot(lo, head_ones)
            blk = blk * lax.rsqrt(ss + NORM_EPS)
            if j < GDN_W // LANES:
                blk = blk * QK_SCALE
        qkv_ref[0, 0:tt, j * LANES:(j + 1) * LANES] = jnp.where(live, blk, 0.0)

    ab = ab_ref[0]
    za = ab + dtb_ref[...]
    softplus = jnp.maximum(za, 0.0) + jnp.log(1.0 + jnp.exp(-jnp.abs(za)))
    gdec = -jnp.exp(alog_ref[...]) * softplus
    lane = lax.broadcasted_iota(jnp.int32, (tt, LANES), 1)
    gb = jnp.where(lane < N_GDN_HEADS, gdec, jnp.where(lane < 2 * N_GDN_HEADS, _sigmoid(ab), 0.0))
    gb_ref[0, 0:tt, :] = jnp.where(live, gb, 0.0)


def _gdn_prep(x3, cb8, ab3, cw, alog_row, dtb_row, tt, t_valid, t_out):
    b, t_in, _ = x3.shape
    n = t_in // tt
    blocks8 = tt // SUBLANES
    w3 = 3 * GDN_W
    kern = functools.partial(_gdn_prep_kernel, tt=tt, t_valid=t_valid, t_out=t_out)
    return pl.pallas_call(
        kern,
        grid=(b, n),
        in_specs=[pl.BlockSpec((1, tt, w3), lambda bi, i: (bi, i, 0)),
                  pl.BlockSpec((1, SUBLANES, w3), lambda bi, i: (bi, jnp.maximum(i * blocks8 - 1, 0), 0)),
                  pl.BlockSpec((1, SUBLANES, w3), lambda bi, i: (bi, 0, 0)),
                  pl.BlockSpec((1, tt, LANES), lambda bi, i: (bi, i, 0)),
                  pl.BlockSpec((CONV_W, w3), lambda bi, i: (0, 0)),
                  pl.BlockSpec((1, LANES), lambda bi, i: (0, 0)),
                  pl.BlockSpec((1, LANES), lambda bi, i: (0, 0))],
        out_specs=(pl.BlockSpec((1, t_out, w3), lambda bi, i: (bi, i, 0)),
                   pl.BlockSpec((1, t_out, LANES), lambda bi, i: (bi, i, 0))),
        out_shape=(jax.ShapeDtypeStruct((b, n * t_out, w3), F32),
                   jax.ShapeDtypeStruct((b, n * t_out, LANES), F32)),
        scratch_shapes=[pltpu.VMEM((tt + SUBLANES, w3), F32)],
        compiler_params=_cparams(("parallel", "arbitrary")),
        name="gdn_prep",
    )(x3, x3, cb8, ab3, cw, alog_row, dtb_row)


def _gdn_scan_kernel(qkv_ref, gb_ref, z_ref, s0_ref, gn_ref, o_ref, s_ref, *, c, g):
    ci = pl.program_id(1)
    gc = g * c
    gs = g * HEAD_DIM
    n_groups = N_GDN_HEADS // g
    c_shift = int(math.log2(c))

    @pl.when(ci == 0)
    def _():
        s_ref[...] = s0_ref[...]

    def iota(shape, dim):
        return lax.broadcasted_iota(jnp.int32, shape, dim)

    gb = gb_ref[0]
    gcum = jnp.dot(jnp.where(iota((c, c), 0) >= iota((c, c), 1), 1.0, 0.0), gb,
                   preferred_element_type=F32, precision=HIGHEST)
    r_i, c_i = iota((gc, gc), 0), iota((gc, gc), 1)
    same = (r_i >> c_shift) == (c_i >> c_shift)
    incl = same & (r_i >= c_i)
    strict = same & (r_i > c_i)
    eye = jnp.where(r_i == c_i, 1.0, 0.0)
    base_shift = min(INV_BASE_SHIFT, c_shift)
    base_blk = (r_i >> base_shift) == (c_i >> base_shift)
    tri = iota((c, gc), 0) <= (iota((c, gc), 1) & (c - 1))
    same_s = (iota((gc, gs), 0) >> c_shift) == (iota((gc, gs), 1) >> HEAD_SHIFT)
    same_s2 = jnp.concatenate([same_s, same_s], axis=0)
    gnorm = gn_ref[...]

    def stack(pieces):
        return jnp.concatenate(pieces, axis=0)

    groups = []
    for gi in range(n_groups):
        heads = range(gi * g, (gi + 1) * g)
        lane = lambda base, h: slice(base + h * HEAD_DIM, base + (h + 1) * HEAD_DIM)
        q_st = stack([qkv_ref[0, :, lane(0, h)] for h in heads])
        k_st = stack([qkv_ref[0, :, lane(GDN_W, h)] for h in heads])
        v_st = stack([qkv_ref[0, :, lane(2 * GDN_W, h)] for h in heads])
        gc_st = stack([gcum[:, h:h + 1] for h in heads])
        bt_st = stack([gb[:, N_GDN_HEADS + h:N_GDN_HEADS + h + 1] for h in heads])
        gl_st = stack([jnp.broadcast_to(gcum[c - 1:c, h:h + 1], (c, 1)) for h in heads])
        gl_s = stack([jnp.broadcast_to(gcum[c - 1:c, h:h + 1], (HEAD_DIM, 1)) for h in heads])
        expand = jnp.where(iota((LANES, gc), 0) == (iota((LANES, gc), 1) >> c_shift) + gi * g, 1.0, 0.0)
        spread = jnp.dot(gb, expand, preferred_element_type=F32, precision=HIGHEST)
        gr_st = jnp.sum(jnp.where(tri, spread, 0.0), axis=0, keepdims=True)
        decay = jnp.exp(jnp.where(incl, gc_st - gr_st, -jnp.inf))
        k16 = k_st.astype(BF16)
        qk_kk = _dot_nt(jnp.concatenate([q_st, k_st], axis=0).astype(BF16), k16)
        a_mat = jnp.where(strict, bt_st * qk_kk[gc:2 * gc] * decay, 0.0)
        a_base = jnp.where(base_blk, a_mat, 0.0)
        groups.append(dict(q=q_st, k=k_st, v=v_st, gc=gc_st, bt=bt_st, gl=gl_st, gl_s=gl_s,
                           qk=qk_kk[0:gc] * decay, a=a_mat, t=eye - a_base, p=a_base, heads=heads))

    for _ in range(base_shift - 1):
        for gr in groups:
            p16 = gr['p'].astype(BF16)
            gr['p'] = _dot(p16, p16)
        for gr in groups:
            gr['t'] = gr['t'] + _dot(gr['t'].astype(BF16), gr['p'].astype(BF16))
    for lvl in range(base_shift, c_shift):
        off = ((r_i >> (lvl + 1)) == (c_i >> (lvl + 1))) & ((r_i >> lvl) != (c_i >> lvl))
        for gr in groups:
            t16 = gr['t'].astype(BF16)
            gr['at'] = _dot(jnp.where(off, gr['a'], 0.0).astype(BF16), t16)
        for gr in groups:
            gr['t'] = gr['t'] - _dot(gr['t'].astype(BF16), gr['at'].astype(BF16))

    for gi, gr in enumerate(groups):
        eg = jnp.exp(gr['gc'])
        rhs = jnp.concatenate([gr['k'] * (gr['bt'] * eg), gr['v'] * gr['bt']], axis=1).astype(BF16)
        gr['wu'] = _dot(gr['t'].astype(BF16), rhs)
        gr['qe'] = gr['q'] * eg
    for gi, gr in enumerate(groups):
        s_rows = slice(gi * gs, (gi + 1) * gs)
        s = s_ref[0, s_rows, :]
        wq = jnp.concatenate([gr['wu'][:, 0:HEAD_DIM], gr['qe']], axis=0)
        wq_bd = jnp.where(same_s2, jnp.concatenate([wq] * g, axis=1), 0.0).astype(BF16)
        ws_qs = _dot(wq_bd, s.astype(BF16))
        u = gr['wu'][:, HEAD_DIM:2 * HEAD_DIM] - ws_qs[0:gc]
        u16 = u.astype(BF16)
        o = ws_qs[gc:2 * gc] + _dot(gr['qk'].astype(BF16), u16)
        kd = gr['k'] * jnp.exp(gr['gl'] - gr['gc'])
        kd_bd = jnp.where(same_s, jnp.concatenate([kd] * g, axis=1), 0.0).astype(BF16)
        s_ref[0, s_rows, :] = s * jnp.exp(gr['gl_s']) + _dot_tn(kd_bd, u16)
        on = o * lax.rsqrt(jnp.mean(o * o, axis=-1, keepdims=True) + NORM_EPS) * gnorm
        for j, h in enumerate(gr['heads']):
            sl = slice(h * HEAD_DIM, (h + 1) * HEAD_DIM)
            zh = z_ref[0, :, sl]
            o_ref[0, :, sl] = on[j * c:(j + 1) * c] * (zh * _sigmoid(zh))


def _gdn_scan(qkv3, gb3, z3, s0, gnorm, c, g):
    b, tp, _ = qkv3.shape
    n = tp // c
    kern = functools.partial(_gdn_scan_kernel, c=c, g=g)
    state_rows = N_GDN_HEADS * HEAD_DIM
    state_spec = pl.BlockSpec((1, state_rows, HEAD_DIM), lambda bi, i: (bi, 0, 0))
    o, s_new = pl.pallas_call(
        kern,
        grid=(b, n),
        in_specs=[pl.BlockSpec((1, c, 3 * GDN_W), lambda bi, i: (bi, i, 0)),
                  pl.BlockSpec((1, c, LANES), lambda bi, i: (bi, i, 0)),
                  pl.BlockSpec((1, c, GDN_W), lambda bi, i: (bi, i, 0)),
                  state_spec,
                  pl.BlockSpec((1, HEAD_DIM), lambda bi, i: (0, 0))],
        out_specs=(pl.BlockSpec((1, c, GDN_W), lambda bi, i: (bi, i, 0)), state_spec),
        out_shape=(jax.ShapeDtypeStruct((b, tp, GDN_W), F32),
                   jax.ShapeDtypeStruct((b, state_rows, HEAD_DIM), F32)),
        compiler_params=_cparams(("parallel", "arbitrary")),
        name="gdn_scan",
    )(qkv3, gb3, z3, s0.reshape(b, state_rows, HEAD_DIM), gnorm)
    return o, s_new.reshape(b, N_GDN_HEADS, HEAD_DIM, HEAD_DIM)


def _compress(readers, pe_ref, wbd_ref, cc, cs):
    acc = None
    for c in range(CMP_BLOCK):
        rows = jnp.concatenate([rd(parity * CMP_BLOCK + c) for rd in readers for parity in range(2)], axis=0)
        part = _dot((rows + pe_ref[c:c + 1, :]).astype(BF16), wbd_ref[c])
        acc = part if acc is None else acc + part
    return _rope128(acc, cc, cs)


def _softmax_rows(s, mask):
    s = jnp.where(mask, s, -jnp.inf)
    m = jnp.max(s, axis=-1, keepdims=True)
    m = jnp.where(m > -jnp.inf, m, 0.0)
    e = jnp.exp(s - m)
    return e / jnp.maximum(jnp.sum(e, axis=-1, keepdims=True), 1e-30)


def _select_blocks(imp, qpos, n_sel, blk_axis):
    blk = lax.broadcasted_iota(jnp.int32, imp.shape, blk_axis)
    cur = qpos >> HEAD_SHIFT
    forced = (blk == 0) | (blk == cur) | (blk == cur - 1)
    valid = (blk * SEL_BLOCK <= qpos) & (blk < n_sel)
    score = jnp.where(forced, jnp.inf, jnp.where(valid, imp, -jnp.inf))
    rank = jnp.zeros(imp.shape, F32)
    for i in range(min(n_sel, N_SCORE)):
        si = score[i:i + 1, :] if blk_axis == 0 else score[:, i:i + 1]
        ahead = (si > score) | ((si == score) & (blk > i))
        rank = rank + jnp.where(ahead, 1.0, 0.0)
    return ((rank < TOP_K) & (blk < n_sel)).astype(BF16)


def _cmp_branch_rows(q_c, ckv16, qpos, qpos_t, n_half, n_sel, tq):
    r = q_c.shape[0]
    nc = 2 * n_half
    col = lax.broadcasted_iota(jnp.int32, (r, nc), 1)
    cmp_end = (2 * (col & (n_half - 1)) + (col >> int(math.log2(n_half))) + 1) * CMP_BLOCK - 1
    p_c = _softmax_rows(_dot_nt(q_c, ckv16), cmp_end <= qpos)
    o_c = _dot(p_c.astype(BF16), ckv16)[:, HEAD_DIM:2 * HEAD_DIM]
    pair = p_c[:, 0:n_half] + p_c[:, n_half:nc]
    imp = pair[0:tq]
    for g in range(1, GQA_GROUP):
        imp = imp + pair[g * tq:(g + 1) * tq]
    if n_half < N_SCORE:
        imp = jnp.concatenate([imp, jnp.zeros((tq, N_SCORE - n_half), F32)], axis=1)
    sel = _select_blocks(imp, qpos_t, n_sel, 1)
    sel_r = jnp.concatenate([sel] * GQA_GROUP, axis=0)
    return o_c, lambda expand: _dot(sel_r, expand)


def _cmp_branch_lanes(q_c, ckv16, t0, n_half, n_sel, tq):
    r = q_c.shape[0]
    nc = 2 * n_half
    qpos_l = (lax.broadcasted_iota(jnp.int32, (1, r), 1) & (tq - 1)) + t0
    row = lax.broadcasted_iota(jnp.int32, (nc, r), 0)
    cmp_end = (2 * (row & (n_half - 1)) + (row >> int(math.log2(n_half))) + 1) * CMP_BLOCK - 1
    s = jnp.where(cmp_end <= qpos_l, _dot_nt(ckv16, q_c), -jnp.inf)
    m = jnp.max(s, axis=0, keepdims=True)
    m = jnp.where(m > -jnp.inf, m, 0.0)
    e = jnp.exp(s - m)
    p_c = e / jnp.maximum(jnp.sum(e, axis=0, keepdims=True), 1e-30)
    o_c = _dot_tn(p_c.astype(BF16), ckv16)[:, HEAD_DIM:2 * HEAD_DIM]
    pair = p_c[0:n_half] + p_c[n_half:nc]
    imp = pair[:, 0:tq]
    for g in range(1, GQA_GROUP):
        imp = imp + pair[:, g * tq:(g + 1) * tq]
    if n_half < N_SCORE:
        imp = jnp.concatenate([imp, jnp.zeros((N_SCORE - n_half, tq), F32)], axis=0)
    sel = _select_blocks(imp, qpos_l[:, 0:tq], n_sel, 0)
    sel_r = jnp.concatenate([sel] * GQA_GROUP, axis=1)
    return o_c, lambda expand: _dot_tn(sel_r, expand)


def _nsa_core(q_blk, t0, ckv, n_half, sel_chunk, n_full, kc, n_sel, win_rows, win_pos0, kvh, gates):
    tq = q_blk.shape[0]
    r = GQA_GROUP * tq
    q_all = jnp.concatenate([q_blk[:, g * HEAD_DIM:(g + 1) * HEAD_DIM] for g in range(GQA_GROUP)], axis=0)
    qpos_t = lax.broadcasted_iota(jnp.int32, (tq, 1), 0) + t0
    qpos = jnp.concatenate([qpos_t] * GQA_GROUP, axis=0)
    zeros64 = jnp.zeros((r, HEAD_DIM), F32)
    q_c = jnp.concatenate([q_all, zeros64], axis=1).astype(BF16)
    ckv16 = ckv.astype(BF16)
    if tq % LANES == 0:
        o_c, picked_fn = _cmp_branch_lanes(q_c, ckv16, t0, n_half, n_sel, tq)
    else:
        o_c, picked_fn = _cmp_branch_rows(q_c, ckv16, qpos, qpos_t, n_half, n_sel, tq)

    j_i = lax.broadcasted_iota(jnp.int32, (N_SCORE, kc), 0)
    k_i = lax.broadcasted_iota(jnp.int32, (N_SCORE, kc), 1) >> HEAD_SHIFT
    kcol = lax.broadcasted_iota(jnp.int32, (r, kc), 1)

    def body(c, carry, causal):
        m, l, acc = carry
        rows = sel_chunk(c)
        s = _dot_nt(q_c, rows)
        expand = jnp.where(j_i == k_i + c * (kc // SEL_BLOCK), 1.0, 0.0).astype(BF16)
        mask = picked_fn(expand) > 0.5
        if causal:
            mask = mask & (kcol + c * kc <= qpos)
        s = jnp.where(mask, s, -jnp.inf)
        m_new = jnp.maximum(m, jnp.max(s, axis=-1, keepdims=True))
        m_safe = jnp.where(m_new > -jnp.inf, m_new, 0.0)
        alpha = jnp.exp(m - m_safe)
        p = jnp.exp(s - m_safe)
        l = alpha * l + jnp.sum(p, axis=-1, keepdims=True)
        acc = alpha * acc + _dot(p.astype(BF16), rows)
        return m_new, l, acc

    carry = (jnp.full((r, 1), -jnp.inf, F32), jnp.zeros((r, 1), F32), jnp.zeros((r, LANES), F32))
    if not isinstance(n_full, int) or n_full > 0:
        carry = lax.fori_loop(0, n_full, lambda c, cr: body(c, cr, False), carry)
    _, l_s, acc_s = body(n_full, carry, True)
    o_s = acc_s[:, HEAD_DIM:2 * HEAD_DIM] / jnp.maximum(l_s, 1e-30)

    lane = lax.broadcasted_iota(jnp.int32, (r, ROW_W), 1)
    q_w = jnp.where((lane >> HEAD_SHIFT) == kvh, jnp.concatenate([q_all, q_all, zeros64, zeros64], axis=1), 0.0)
    q_w = q_w.astype(BF16)
    nw = win_rows.shape[0]
    kpos = lax.broadcasted_iota(jnp.int32, (r, nw), 1) + win_pos0
    p_w = _softmax_rows(_dot_nt(q_w, win_rows), (kpos <= qpos) & (kpos >= qpos - WINDOW))
    o_w2 = _dot(p_w.astype(BF16), win_rows)
    o_w = jnp.where(kvh == 0, o_w2[:, 2 * HEAD_DIM:3 * HEAD_DIM], o_w2[:, 3 * HEAD_DIM:4 * HEAD_DIM])

    outs = []
    for g in range(GQA_GROUP):
        rows_g = slice(g * tq, (g + 1) * tq)
        g0 = _sigmoid(gates[:, 3 * g:3 * g + 1])
        g1 = _sigmoid(gates[:, 3 * g + 1:3 * g + 2])
        g2 = _sigmoid(gates[:, 3 * g + 2:3 * g + 3])
        outs.append(g0 * o_c[rows_g] + g1 * o_s[rows_g] + g2 * o_w[rows_g])
    return outs


def _nsa_prompt_kernel(q_ref, kc_ref, ks_ref, win_ref, gate_ref, wbd_ref, pe_ref, cc_ref, cs_ref,
                       o_ref, ckv_ref, *, tq, t, kc):
    kvh = pl.program_id(1)
    i = pl.program_id(2)
    n_half = t // (2 * CMP_BLOCK)

    @pl.when(i == 0)
    def _():
        read = lambda start: kc_ref[0, pl.ds(start, n_half, stride=2 * CMP_BLOCK), :]
        ckv_ref[...] = _compress([read], pe_ref, wbd_ref, cc_ref[...], cs_ref[...])

    t0 = i * tq
    sel_chunk = lambda c: ks_ref[0, pl.ds(pl.multiple_of(c * kc, kc), kc), :].astype(BF16)
    span = WINDOW + tq
    start = pl.multiple_of(jnp.maximum(t0 - WINDOW, 0), tq)
    win_rows = win_ref[pl.ds(start, span), :].astype(BF16)
    outs = _nsa_core(q_ref[...], t0, ckv_ref[...], n_half, sel_chunk, t0 // kc, kc, t // SEL_BLOCK,
                     win_rows, start, kvh, gate_ref[...])
    for g in range(GQA_GROUP):
        o_ref[:, g * HEAD_DIM:(g + 1) * HEAD_DIM] = outs[g]


def _nsa_prompt(q, kvc, kvs, win, gate, wbd, pe, cc, cs, b, t, tq=128, kc=512):
    n = t // tq
    nc = t // CMP_BLOCK
    kern = functools.partial(_nsa_prompt_kernel, tq=tq, t=t, kc=kc)
    fix2 = lambda bi, h, i: (0, 0)
    return pl.pallas_call(
        kern,
        grid=(b, N_KV_HEADS, n),
        in_specs=[pl.BlockSpec((tq, ROW_W), lambda bi, h, i: (bi * n + i, h)),
                  pl.BlockSpec((1, t, LANES), lambda bi, h, i: (h, bi, 0)),
                  pl.BlockSpec((1, t, LANES), lambda bi, h, i: (h, bi, 0)),
                  pl.BlockSpec((t, ROW_W), lambda bi, h, i: (bi, 0)),
                  pl.BlockSpec((tq, LANES), lambda bi, h, i: (bi * n + i, h)),
                  pl.BlockSpec((CMP_BLOCK, LANES, LANES), lambda bi, h, i: (0, 0, 0)),
                  pl.BlockSpec((CMP_BLOCK, LANES), fix2),
                  pl.BlockSpec((nc, LANES), fix2),
                  pl.BlockSpec((nc, LANES), fix2)],
        out_specs=pl.BlockSpec((tq, ROW_W), lambda bi, h, i: (bi * n + i, h)),
        out_shape=jax.ShapeDtypeStruct((b * t, NSA_W), F32),
        scratch_shapes=[pltpu.VMEM((nc, LANES), F32)],
        compiler_params=_cparams(("parallel", "parallel", "arbitrary")),
        name="nsa_prompt",
    )(q, kvc, kvs, win, gate, wbd, pe, cc, cs)


def _nsa_sample_kernel(pt_ref, *refs, n_steps, pages_per_step, past, tq, ts, n_pad, nw):
    page_refs = refs[:pages_per_step]
    (q_ref, kvnew_ref, wbuf_ref, wnew_ref, gate_ref, wbd_ref, pe_ref, cc_ref, cs_ref,
     o_ref, wout_ref, pgc_ref, pgs_ref, wn_ref) = refs[pages_per_step:]
    j = pl.program_id(1)
    for k, page_ref in enumerate(page_refs):
        rows = pl.ds(pl.multiple_of((j * pages_per_step + k) * PAGE_SIZE, PAGE_SIZE), PAGE_SIZE)
        for kvh in range(N_KV_HEADS):
            pgc_ref[kvh, rows, :] = page_ref[0, 0, kvh, 0:LANES, :].T
            pgs_ref[kvh, rows, :] = page_ref[0, 0, kvh, LANES:ROW_W, :].T

    @pl.when(j == n_steps - 1)
    def _():
        n_half = past // (2 * CMP_BLOCK)
        wbuf_t = wbuf_ref[0, 0]
        wnew = wnew_ref[0]
        wn_ref[0:WINDOW, :] = wbuf_t.T
        wn_ref[WINDOW:WINDOW + tq, :] = wnew
        wn_ref[WINDOW + tq:nw, :] = jnp.zeros((nw - WINDOW - tq, ROW_W), F32)
        win_rows = wn_ref[...].astype(BF16)

        shifted = pltpu.roll(wbuf_t, WINDOW - ts, 1)
        new_t = jnp.concatenate([wnew, jnp.zeros((LANES - tq, ROW_W), F32)], axis=0).T
        lane = lax.broadcasted_iota(jnp.int32, (ROW_W, LANES), 1)
        tail = jnp.where(lane >= LANES - ts, pltpu.roll(new_t, LANES - ts, 1), shifted[:, WINDOW - LANES:WINDOW])
        wout_ref[0, :, 0:WINDOW - LANES] = shifted[:, 0:WINDOW - LANES]
        wout_ref[0, :, WINDOW - LANES:WINDOW] = tail

        readers = []
        for kvh in range(N_KV_HEADS):
            pgs_ref[kvh, past:past + tq, :] = kvnew_ref[kvh, 0]
            pgs_ref[kvh, past + tq:n_pad, :] = jnp.zeros((n_pad - past - tq, LANES), F32)
            readers.append(lambda start, kvh=kvh: pgc_ref[kvh, pl.ds(start, n_half, stride=2 * CMP_BLOCK), :])
        ckv2 = _compress(readers, pe_ref, wbd_ref, cc_ref[...], cs_ref[...])
        for kvh in range(N_KV_HEADS):
            ckv = ckv2[kvh * 2 * n_half:(kvh + 1) * 2 * n_half]
            sel_chunk = lambda c, kvh=kvh: pgs_ref[kvh].astype(BF16)
            outs = _nsa_core(q_ref[0, :, kvh * ROW_W:(kvh + 1) * ROW_W], past, ckv, n_half, sel_chunk, 0,
                             n_pad, past // SEL_BLOCK + 1, win_rows, past - WINDOW, kvh,
                             gate_ref[0, :, kvh * LANES:(kvh + 1) * LANES])
            for g in range(GQA_GROUP):
                col = kvh * ROW_W + g * HEAD_DIM
                o_ref[0, :, col:col + HEAD_DIM] = outs[g]


def _nsa_sample(page_table, pool_t, layer, ts, q8, kvnew8, win_t, wnew8, gate8, wbd, pe, cc, cs, pages_per_step=4):
    b, n_pages = page_table.shape
    past = n_pages * PAGE_SIZE
    tq = q8.shape[1]
    n_pad = past + SEL_BLOCK
    nw = WINDOW + LANES
    nc2 = N_KV_HEADS * (past // CMP_BLOCK)
    n_steps = n_pages // pages_per_step
    kern = functools.partial(_nsa_sample_kernel, n_steps=n_steps, pages_per_step=pages_per_step, past=past,
                             tq=tq, ts=ts, n_pad=n_pad, nw=nw)
    per_b = lambda bi, j, pt: (bi, 0, 0)
    fix2 = lambda bi, j, pt: (0, 0)
    page_spec = lambda k: pl.BlockSpec((1, 1, N_KV_HEADS, ROW_W, PAGE_SIZE),
                                       lambda bi, j, pt: (layer, pt[bi, j * pages_per_step + k], 0, 0, 0))
    grid_spec = pltpu.PrefetchScalarGridSpec(
        num_scalar_prefetch=1,
        grid=(b, n_steps),
        in_specs=[page_spec(k) for k in range(pages_per_step)] + [
                  pl.BlockSpec((1, tq, NSA_W), per_b),
                  pl.BlockSpec((N_KV_HEADS, 1, tq, LANES), lambda bi, j, pt: (0, bi, 0, 0)),
                  pl.BlockSpec((1, 1, ROW_W, WINDOW), lambda bi, j, pt: (layer, bi, 0, 0)),
                  pl.BlockSpec((1, tq, ROW_W), per_b),
                  pl.BlockSpec((1, tq, N_KV_HEADS * LANES), per_b),
                  pl.BlockSpec((CMP_BLOCK, LANES, LANES), lambda bi, j, pt: (0, 0, 0)),
                  pl.BlockSpec((CMP_BLOCK, LANES), fix2),
                  pl.BlockSpec((nc2, LANES), fix2),
                  pl.BlockSpec((nc2, LANES), fix2)],
        out_specs=(pl.BlockSpec((1, tq, NSA_W), per_b), pl.BlockSpec((1, ROW_W, WINDOW), per_b)),
        scratch_shapes=[pltpu.VMEM((N_KV_HEADS, past, LANES), F32), pltpu.VMEM((N_KV_HEADS, n_pad, LANES), F32),
                        pltpu.VMEM((nw, ROW_W), F32)],
    )
    return pl.pallas_call(
        kern,
        grid_spec=grid_spec,
        out_shape=(jax.ShapeDtypeStruct((b, tq, NSA_W), F32), jax.ShapeDtypeStruct((b, ROW_W, WINDOW), F32)),
        compiler_params=_cparams(("parallel", "arbitrary")),
        name="nsa_sample",
    )(page_table, *([pool_t] * pages_per_step), q8, kvnew8, win_t, wnew8, gate8, wbd, pe, cc, cs)


def _mlp_kernel(x_ref, og_ref, on_ref, wo_ref, gm_ref, wu_ref, wd_ref, gf_ref, y_ref, x1_ref, h_ref, acc_ref,
                *, final_norm):
    f = pl.program_id(1)

    @pl.when(f == 0)
    def _():
        x1 = (x_ref[...] + _dot(og_ref[...].astype(BF16), wo_ref[0:GDN_W, :])
              + _dot(on_ref[...].astype(BF16), wo_ref[GDN_W:GDN_W + NSA_W, :]))
        x1_ref[...] = x1
        var = jnp.mean(x1 * x1, axis=-1, keepdims=True)
        h_ref[...] = (x1 * lax.rsqrt(var + NORM_EPS) * gm_ref[...]).astype(BF16)
        acc_ref[...] = jnp.zeros(acc_ref.shape, F32)

    up = jnp.maximum(_dot(h_ref[...], wu_ref[...]), 0.0)
    acc_ref[...] += _dot((up * up).astype(BF16), wd_ref[...])

    @pl.when(f == pl.num_programs(1) - 1)
    def _():
        y = x1_ref[...] + acc_ref[...]
        if final_norm:
            var = jnp.mean(y * y, axis=-1, keepdims=True)
            y = y * lax.rsqrt(var + NORM_EPS) * gf_ref[...]
        y_ref[...] = y


def _mlp(x, og, on, wo, gm, wu, wd, gf, final_norm, tm, tf=1024):
    m = x.shape[0]
    kern = functools.partial(_mlp_kernel, final_norm=final_norm)
    row = lambda i, f: (i, 0)
    fix = lambda i, f: (0, 0)
    return pl.pallas_call(
        kern,
        grid=(m // tm, D_FF // tf),
        in_specs=[pl.BlockSpec((tm, D_MODEL), row), pl.BlockSpec((tm, GDN_W), row), pl.BlockSpec((tm, NSA_W), row),
                  pl.BlockSpec((D_MODEL, D_MODEL), fix), pl.BlockSpec((1, D_MODEL), fix),
                  pl.BlockSpec((D_MODEL, tf), lambda i, f: (0, f)), pl.BlockSpec((tf, D_MODEL), lambda i, f: (f, 0)),
                  pl.BlockSpec((1, D_MODEL), fix)],
        out_specs=pl.BlockSpec((tm, D_MODEL), row),
        out_shape=jax.ShapeDtypeStruct((m, D_MODEL), F32),
        scratch_shapes=[pltpu.VMEM((tm, D_MODEL), F32), pltpu.VMEM((tm, D_MODEL), BF16),
                        pltpu.VMEM((tm, D_MODEL), F32)],
        compiler_params=_cparams(("parallel", "arbitrary")),
        name="mlp",
    )(x, og, on, wo, gm, wu, wd, gf)


def _proj_columns():
    src = np.full((N_PROJ,), -1, np.int64)
    o_z, o_a, o_b = 3 * GDN_W, 4 * GDN_W, 4 * GDN_W + N_GDN_HEADS
    o_q = o_b + N_GDN_HEADS
    o_kv = o_q + NSA_W
    o_g = o_kv + 6 * KV_W
    src[C_QKV:C_Z] = np.arange(0, 3 * GDN_W)
    src[C_Z:C_Q] = np.arange(o_z, o_z + GDN_W)
    src[C_Q:C_KV] = np.arange(o_q, o_q + NSA_W)
    for kvh in range(N_KV_HEADS):
        for s in range(4):
            dst = C_KV + kvh * ROW_W + s * HEAD_DIM
            src[dst:dst + HEAD_DIM] = o_kv + s * KV_W + kvh * HEAD_DIM + np.arange(HEAD_DIM)
    src[C_WIN:C_WIN + 2 * KV_W] = o_kv + 4 * KV_W + np.arange(2 * KV_W)
    src[C_AB:C_AB + N_GDN_HEADS] = o_a + np.arange(N_GDN_HEADS)
    src[C_AB + N_GDN_HEADS:C_AB + 2 * N_GDN_HEADS] = o_b + np.arange(N_GDN_HEADS)
    for kvh in range(N_KV_HEADS):
        dst = C_GATE + kvh * LANES
        src[dst:dst + 3 * GQA_GROUP] = o_g + kvh * 3 * GQA_GROUP + np.arange(3 * GQA_GROUP)
    return src


def _rope_tables(pos):
    half = HEAD_DIM // 2
    inv_freq = ROPE_THETA ** (-jnp.arange(half, dtype=F32) / half)
    ang = pos.astype(F32)[:, None] * inv_freq[None, :]
    cos, sin = jnp.cos(ang), jnp.sin(ang)
    c64 = jnp.concatenate([cos, cos], axis=1)
    s64 = jnp.concatenate([-sin, sin], axis=1)
    one, zero = jnp.ones_like(c64), jnp.zeros_like(s64)
    return (jnp.concatenate([c64, c64], axis=1), jnp.concatenate([s64, s64], axis=1),
            jnp.concatenate([c64, one], axis=1), jnp.concatenate([s64, zero], axis=1))


def _cmp_tables(n_blocks):
    n_half = n_blocks // 2
    r = np.arange(n_blocks)
    blk = 2 * (r % n_half) + r // n_half
    end_pos = jnp.asarray((blk + 1) * CMP_BLOCK - 1)
    _, _, cc, cs = _rope_tables(end_pos)
    return cc, cs


def _layer_params(l, norm_mix, w_in, conv_w, a_log, dt_bias, gdn_norm, cmp_pe_k, cmp_w_k, cmp_pe_v, cmp_w_v,
                  w_out, norm_mlp, w_up, w_down):
    src = _proj_columns()
    w_ext = jnp.concatenate([w_in[l], jnp.zeros((D_MODEL, 1), F32)], axis=1)
    w_p = jnp.take(w_ext, jnp.asarray(np.where(src < 0, w_in.shape[2], src)), axis=1).astype(BF16)
    zeros = jnp.zeros((CMP_BLOCK, HEAD_DIM, HEAD_DIM), F32)
    wbd = jnp.concatenate([jnp.concatenate([cmp_w_k[l], zeros], axis=2),
                           jnp.concatenate([zeros, cmp_w_v[l]], axis=2)], axis=1).astype(BF16)
    pad_row = lambda v: jnp.pad(v.astype(F32), (0, LANES - v.shape[0]))[None, :]
    return dict(
        norm_mix=norm_mix[l][None, :], w_p=w_p, conv_w=conv_w[l], alog=pad_row(a_log[l]), dtb=pad_row(dt_bias[l]),
        gdn_norm=gdn_norm[l][None, :], wbd=wbd, pe=jnp.concatenate([cmp_pe_k[l], cmp_pe_v[l]], axis=1),
        w_out=w_out[l].astype(BF16), norm_mlp=norm_mlp[l][None, :], w_up=w_up[l].astype(BF16),
        w_down=w_down[l].astype(BF16))


def _prompt_layer(x, lp, tabs, cmp_tabs, gf, final_norm, b, t):
    qkv_pre, z, q, kvc, kvs, win, ab, gate = _proj(x, lp['norm_mix'], lp['w_p'], tabs, 256)
    w3 = 3 * GDN_W
    cb8 = jnp.zeros((b, SUBLANES, w3), F32)
    qkv3, gb3 = _gdn_prep(qkv_pre.reshape(b, t, w3), cb8, ab.reshape(b, t, LANES), lp['conv_w'], lp['alog'],
                          lp['dtb'], 256, t, 256)
    s0 = jnp.zeros((b, N_GDN_HEADS, HEAD_DIM, HEAD_DIM), F32)
    o_gdn, s_new = _gdn_scan(qkv3, gb3, z.reshape(b, t, GDN_W), s0, lp['gdn_norm'], GDN_CHUNK, 4)
    o_nsa = _nsa_prompt(q, kvc, kvs, win, gate, lp['wbd'], lp['pe'], cmp_tabs[0], cmp_tabs[1], b, t)
    y = _mlp(x, o_gdn.reshape(b * t, GDN_W), o_nsa, lp['w_out'], lp['norm_mlp'], lp['w_up'], lp['w_down'], gf,
             final_norm, 512)
    rows = jnp.concatenate([kvc, kvs], axis=2).reshape(N_KV_HEADS, b, t, 4, HEAD_DIM).transpose(1, 0, 2, 3, 4)
    win_new = win.reshape(b, t, 2, N_KV_HEADS, HEAD_DIM)[:, t - min(WINDOW, t):]
    conv_new = qkv_pre.reshape(b, t, w3)[:, t - (CONV_W - 1):]
    return y, rows, win_new, s_new, conv_new


def _pad_rows(a, n):
    return jnp.pad(a, ((0, 0), (0, n - a.shape[1]), (0, 0)))


def _sample_layer(x, lp, tabs, cmp_tabs, gf, final_norm, b, t, layer, pool_t, page_table, win_t, s0, conv_buf):
    m = b * t
    qkv_pre, z, q, kvc, kvs, win, ab, gate = _proj(x, lp['norm_mix'], lp['w_p'], tabs, 256)
    w3 = 3 * GDN_W
    tp = SUBLANES
    cb8 = jnp.pad(conv_buf, ((0, 0), (SUBLANES - (CONV_W - 1), 0), (0, 0)))
    qkv3, gb3 = _gdn_prep(_pad_rows(qkv_pre.reshape(b, t, w3), tp), cb8, _pad_rows(ab.reshape(b, t, LANES), tp),
                          lp['conv_w'], lp['alog'], lp['dtb'], tp, t, tp)
    o_gdn, s_new = _gdn_scan(qkv3, gb3, _pad_rows(z.reshape(b, t, GDN_W), tp), s0, lp['gdn_norm'], tp,
                             N_GDN_HEADS)
    o_gdn = o_gdn[:, :t].reshape(m, GDN_W)
    kvnew8 = jnp.pad(kvs.reshape(N_KV_HEADS, b, t, LANES), ((0, 0), (0, 0), (0, tp - t), (0, 0)))
    o_nsa, win_out_t = _nsa_sample(page_table, pool_t, layer, t, _pad_rows(q.reshape(b, t, NSA_W), tp), kvnew8,
                                   win_t, _pad_rows(win.reshape(b, t, ROW_W), tp),
                                   _pad_rows(gate.reshape(b, t, N_KV_HEADS * LANES), tp), lp['wbd'], lp['pe'],
                                   cmp_tabs[0], cmp_tabs[1])
    o_nsa = o_nsa[:, :t].reshape(m, NSA_W)
    y = _mlp(x, o_gdn, o_nsa, lp['w_out'], lp['norm_mlp'], lp['w_up'], lp['w_down'], gf, final_norm, 512)
    rows = jnp.concatenate([kvc, kvs], axis=2).reshape(N_KV_HEADS, b, t, 4, HEAD_DIM).transpose(1, 0, 2, 3, 4)
    win_new = win_out_t.reshape(b, 2, N_KV_HEADS, HEAD_DIM, WINDOW).transpose(0, 4, 1, 2, 3)
    conv_new = jnp.concatenate([conv_buf, qkv_pre.reshape(b, t, w3)], axis=1)[:, t:]
    return y, rows, win_new, s_new, conv_new


def kernel(x_prompt, x_sample, cache_kv, page_table, state_win, state_gdn, state_conv, norm_mix, w_in, conv_w, a_log, dt_bias, gdn_norm, cmp_pe_k, cmp_w_k, cmp_pe_v, cmp_w_v, w_out, norm_mlp, w_up, w_down, norm_final):
    bp, tp_, _ = x_prompt.shape
    bs, ts, _ = x_sample.shape
    depth = cache_kv.shape[0]
    n_pages = page_table.shape[1]
    past = n_pages * PAGE_SIZE
    assert state_win.shape[2] == WINDOW and tp_ % 512 == 0 and ts <= SUBLANES and past % (2 * CMP_BLOCK) == 0

    tabs_p = _rope_tables(jnp.arange(tp_))
    tabs_s = tuple(jnp.tile(tb, (bs, 1)) for tb in _rope_tables(past + jnp.arange(ts)))
    cmp_p = _cmp_tables(tp_ // CMP_BLOCK)
    cmp_s = tuple(jnp.tile(tb, (N_KV_HEADS, 1)) for tb in _cmp_tables((past + ts) // CMP_BLOCK))
    pool_t = cache_kv.transpose(0, 1, 2, 4, 5, 3).reshape(depth, cache_kv.shape[1], N_KV_HEADS, ROW_W, PAGE_SIZE)
    win_t = state_win.transpose(0, 1, 3, 4, 5, 2).reshape(depth, bs, ROW_W, WINDOW)
    gf = norm_final[None, :]

    xp = x_prompt.reshape(bp * tp_, D_MODEL)
    xs = x_sample.reshape(bs * ts, D_MODEL)
    outs = [[] for _ in range(8)]
    for l in range(depth):
        lp = _layer_params(l, norm_mix, w_in, conv_w, a_log, dt_bias, gdn_norm, cmp_pe_k, cmp_w_k, cmp_pe_v,
                           cmp_w_v, w_out, norm_mlp, w_up, w_down)
        last = l == depth - 1
        xp, r, w, s, c = _prompt_layer(xp, lp, tabs_p, cmp_p, gf, last, bp, tp_)
        for k, v in zip((0, 2, 4, 6), (r, w, s, c)):
            outs[k].append(v)
        xs, r, w, s, c = _sample_layer(xs, lp, tabs_s, cmp_s, gf, last, bs, ts, l, pool_t, page_table,
                                       win_t, state_gdn[l], state_conv[l])
        for k, v in zip((1, 3, 5, 7), (r, w, s, c)):
            outs[k].append(v)
    return (xp.reshape(bp, tp_, D_MODEL), xs.reshape(bs, ts, D_MODEL)) + tuple(jnp.stack(o) for o in outs)
```

```python
import functools
import math

import numpy as np
import jax
import jax.numpy as jnp
from jax import lax
from jax.experimental import pallas as pl
from jax.experimental.pallas import tpu as pltpu

F32 = jnp.float32
BF16 = jnp.bfloat16
HIGHEST = lax.Precision.HIGHEST

D_MODEL = 1024
HEAD_DIM = 64
N_GDN_HEADS = 8
GDN_W = N_GDN_HEADS * HEAD_DIM
N_NSA_HEADS = 8
NSA_W = N_NSA_HEADS * HEAD_DIM
N_KV_HEADS = 2
GQA_GROUP = N_NSA_HEADS // N_KV_HEADS
KV_W = N_KV_HEADS * HEAD_DIM
CONV_W = 4
GDN_CHUNK = 64
CMP_BLOCK = 32
SEL_BLOCK = 64
TOP_K = 16
WINDOW = 512
D_FF = 4 * D_MODEL
ROPE_THETA = 10000.0
NORM_EPS = 1e-6
PAGE_SIZE = 128
N_SCORE = 64
HEAD_SHIFT = 6
INV_BASE_SHIFT = 3
QK_SCALE = HEAD_DIM ** -0.5

LANES = 128
SUBLANES = 8
VMEM_LIMIT = 56 * 1024 * 1024

C_QKV = 0
C_Z = 3 * GDN_W
C_Q = C_Z + GDN_W
C_KV = C_Q + NSA_W
C_WIN = C_KV + N_KV_HEADS * 4 * HEAD_DIM
C_AB = C_WIN + 2 * KV_W
C_GATE = C_AB + LANES
N_PROJ = C_GATE + N_KV_HEADS * LANES
ROW_W = 4 * HEAD_DIM
PROJ_ROWS = 512


def _cparams(sem):
    return pltpu.CompilerParams(dimension_semantics=sem, vmem_limit_bytes=VMEM_LIMIT)


def _sigmoid(x):
    return 1.0 / (1.0 + jnp.exp(-x))


def _dot(a, b):
    return jnp.dot(a, b, preferred_element_type=F32)


def _dot_nt(a, b):
    return lax.dot_general(a, b, (((1,), (1,)), ((), ())), preferred_element_type=F32)


def _dot_tn(a, b, precision=None):
    return lax.dot_general(a, b, (((0,), (0,)), ((), ())), preferred_element_type=F32, precision=precision)


def _rope128(v, cos, sin):
    lane = lax.broadcasted_iota(jnp.int32, v.shape, 1)
    first = (lane & (HEAD_DIM - 1)) < (HEAD_DIM // 2)
    swapped = jnp.where(first, pltpu.roll(v, LANES - HEAD_DIM // 2, 1), pltpu.roll(v, HEAD_DIM // 2, 1))
    return v * cos + swapped * sin


def _proj_kernel(x_ref, g_ref, w_ref, cf_ref, sf_ref, ch_ref, sh_ref,
                 qkv_ref, z_ref, q_ref, kc_ref, ks_ref, win_ref, ab_ref, gate_ref, kst_ref, wint_ref):
    x = x_ref[...]
    var = jnp.mean(x * x, axis=-1, keepdims=True)
    h = (x * lax.rsqrt(var + NORM_EPS) * g_ref[...]).astype(BF16)
    p = _dot(h, w_ref[...])
    qkv_ref[...] = p[:, C_QKV:C_Z]
    z_ref[...] = p[:, C_Z:C_Q]
    cf, sf, ch, sh = cf_ref[...], sf_ref[...], ch_ref[...], sh_ref[...]
    for j in range(NSA_W // LANES):
        q_ref[:, j * LANES:(j + 1) * LANES] = _rope128(p[:, C_Q + j * LANES:C_Q + (j + 1) * LANES], cf, sf) * QK_SCALE
    for kvh in range(N_KV_HEADS):
        base = C_KV + kvh * ROW_W
        kc_ref[kvh] = p[:, base:base + LANES]
        ks = _rope128(p[:, base + LANES:base + ROW_W], ch, sh)
        ks_ref[kvh] = ks
        kst_ref[kvh, 0] = ks.T
    wk = _rope128(p[:, C_WIN:C_WIN + LANES], cf, sf)
    wv = p[:, C_WIN + LANES:C_WIN + ROW_W]
    win_ref[:, 0:LANES] = wk
    win_ref[:, LANES:ROW_W] = wv
    for j in range(x.shape[0] // LANES):
        rows = slice(j * LANES, (j + 1) * LANES)
        wint_ref[j, 0:LANES, :] = wk[rows].T
        wint_ref[j, LANES:ROW_W, :] = wv[rows].T
    ab_ref[...] = p[:, C_AB:C_AB + LANES]
    gate_ref[...] = p[:, C_GATE:C_GATE + N_KV_HEADS * LANES]


def _proj(x, g, w, tabs, tm):
    m = x.shape[0]
    nt = tabs[0].shape[0] // tm
    row = lambda i: (i, 0)
    tab = lambda i: (i % nt, 0)
    fix = lambda i: (0, 0)
    out_shapes = (
        jax.ShapeDtypeStruct((m, 3 * GDN_W), F32),
        jax.ShapeDtypeStruct((m, GDN_W), F32),
        jax.ShapeDtypeStruct((m, NSA_W), F32),
        jax.ShapeDtypeStruct((N_KV_HEADS, m, LANES), F32),
        jax.ShapeDtypeStruct((N_KV_HEADS, m, LANES), F32),
        jax.ShapeDtypeStruct((m, ROW_W), F32),
        jax.ShapeDtypeStruct((m, LANES), F32),
        jax.ShapeDtypeStruct((m, N_KV_HEADS * LANES), F32),
        jax.ShapeDtypeStruct((N_KV_HEADS, m // tm, LANES, tm), F32),
        jax.ShapeDtypeStruct((m // LANES, ROW_W, LANES), F32),
    )
    return pl.pallas_call(
        _proj_kernel,
        grid=(m // tm,),
        in_specs=[pl.BlockSpec((tm, D_MODEL), row), pl.BlockSpec((1, D_MODEL), fix),
                  pl.BlockSpec((D_MODEL, N_PROJ), fix)] + [pl.BlockSpec((tm, LANES), tab)] * 4,
        out_specs=(pl.BlockSpec((tm, 3 * GDN_W), row), pl.BlockSpec((tm, GDN_W), row),
                   pl.BlockSpec((tm, NSA_W), row), pl.BlockSpec((N_KV_HEADS, tm, LANES), lambda i: (0, i, 0)),
                   pl.BlockSpec((N_KV_HEADS, tm, LANES), lambda i: (0, i, 0)),
                   pl.BlockSpec((tm, ROW_W), row), pl.BlockSpec((tm, LANES), row),
                   pl.BlockSpec((tm, N_KV_HEADS * LANES), row),
                   pl.BlockSpec((N_KV_HEADS, 1, LANES, tm), lambda i: (0, i, 0, 0)),
                   pl.BlockSpec((tm // LANES, ROW_W, LANES), lambda i: (i, 0, 0))),
        out_shape=out_shapes,
        compiler_params=_cparams(("parallel",)),
        name="proj",
    )(x, g, w, *tabs)


def _gdn_prep_kernel(x_ref, prev_ref, cb_ref, ab_ref, cw_ref, alog_ref, dtb_ref,
                     qkv_ref, gb_ref, xs_ref, *, tt, t_valid, t_out):
    i = pl.program_id(1)
    xs_ref[SUBLANES:SUBLANES + tt, :] = x_ref[0]

    @pl.when(i == 0)
    def _():
        xs_ref[0:SUBLANES, :] = cb_ref[0]

    @pl.when(i > 0)
    def _():
        xs_ref[0:SUBLANES, :] = prev_ref[0]

    conv = xs_ref[SUBLANES:SUBLANES + tt, :] * cw_ref[CONV_W - 1:CONV_W, :]
    for k in range(1, CONV_W):
        conv = conv + xs_ref[SUBLANES - k:SUBLANES - k + tt, :] * cw_ref[CONV_W - 1 - k:CONV_W - k, :]
    c = conv * _sigmoid(conv)

    rows = lax.broadcasted_iota(jnp.int32, (tt, LANES), 0) + i * tt
    live = rows < t_valid
    r_i = lax.broadcasted_iota(jnp.int32, (LANES, LANES), 0)
    c_i = lax.broadcasted_iota(jnp.int32, (LANES, LANES), 1)
    head_ones = ((r_i >> HEAD_SHIFT) == (c_i >> HEAD_SHIFT)).astype(BF16)

    if t_out != tt:
        qkv_ref[...] = jnp.zeros(qkv_ref.shape, F32)
        gb_ref[...] = jnp.zeros(gb_ref.shape, F32)
    for j in range(3 * GDN_W // LANES):
        blk = c[:, j * LANES:(j + 1) * LANES]
        if j < 2 * GDN_W // LANES:
            sq = blk * blk
            hi = sq.astype(BF16)
            lo = (sq - hi.astype(F32)).astype(BF16)
            ss = _dot(hi, head_ones) + _dot(lo, head_ones)
            blk = blk * lax.rsqrt(ss + NORM_EPS)
            if j < GDN_W // LANES:
                blk = blk * QK_SCALE
        qkv_ref[0, 0:tt, j * LANES:(j + 1) * LANES] = jnp.where(live, blk, 0.0)

    ab = ab_ref[0]
    za = ab + dtb_ref[...]
    softplus = jnp.maximum(za, 0.0) + jnp.log(1.0 + jnp.exp(-jnp.abs(za)))
    gdec = -jnp.exp(alog_ref[...]) * softplus
    lane = lax.broadcasted_iota(jnp.int32, (tt, LANES), 1)
    gb = jnp.where(lane < N_GDN_HEADS, gdec, jnp.where(lane < 2 * N_GDN_HEADS, _sigmoid(ab), 0.0))
    gb_ref[0, 0:tt, :] = jnp.where(live, gb, 0.0)


def _gdn_prep(x3, cb8, ab3, cw, alog_row, dtb_row, tt, t_valid, t_out):
    b, t_in, _ = x3.shape
    n = t_in // tt
    blocks8 = tt // SUBLANES
    w3 = 3 * GDN_W
    kern = functools.partial(_gdn_prep_kernel, tt=tt, t_valid=t_valid, t_out=t_out)
    return pl.pallas_call(
        kern,
        grid=(b, n),
        in_specs=[pl.BlockSpec((1, tt, w3), lambda bi, i: (bi, i, 0)),
                  pl.BlockSpec((1, SUBLANES, w3), lambda bi, i: (bi, jnp.maximum(i * blocks8 - 1, 0), 0)),
                  pl.BlockSpec((1, SUBLANES, w3), lambda bi, i: (bi, 0, 0)),
                  pl.BlockSpec((1, tt, LANES), lambda bi, i: (bi, i, 0)),
                  pl.BlockSpec((CONV_W, w3), lambda bi, i: (0, 0)),
                  pl.BlockSpec((1, LANES), lambda bi, i: (0, 0)),
                  pl.BlockSpec((1, LANES), lambda bi, i: (0, 0))],
        out_specs=(pl.BlockSpec((1, t_out, w3), lambda bi, i: (bi, i, 0)),
                   pl.BlockSpec((1, t_out, LANES), lambda bi, i: (bi, i, 0))),
        out_shape=(jax.ShapeDtypeStruct((b, n * t_out, w3), F32),
                   jax.ShapeDtypeStruct((b, n * t_out, LANES), F32)),
        scratch_shapes=[pltpu.VMEM((tt + SUBLANES, w3), F32)],
        compiler_params=_cparams(("parallel", "arbitrary")),
        name="gdn_prep",
    )(x3, x3, cb8, ab3, cw, alog_row, dtb_row)


def _gdn_scan_kernel(qkv_ref, gb_ref, z_ref, s0_ref, gn_ref, o_ref, s_ref, *, c, g):
    ci = pl.program_id(1)
    gc = g * c
    gs = g * HEAD_DIM
    n_groups = N_GDN_HEADS // g
    c_shift = int(math.log2(c))

    @pl.when(ci == 0)
    def _():
        s_ref[...] = s0_ref[...]

    def iota(shape, dim):
        return lax.broadcasted_iota(jnp.int32, shape, dim)

    gb = gb_ref[0]
    gcum = jnp.dot(jnp.where(iota((c, c), 0) >= iota((c, c), 1), 1.0, 0.0), gb,
                   preferred_element_type=F32, precision=HIGHEST)
    r_i, c_i = iota((gc, gc), 0), iota((gc, gc), 1)
    same = (r_i >> c_shift) == (c_i >> c_shift)
    incl = same & (r_i >= c_i)
    strict = same & (r_i > c_i)
    eye = jnp.where(r_i == c_i, 1.0, 0.0)
    base_shift = min(INV_BASE_SHIFT, c_shift)
    base_blk = (r_i >> base_shift) == (c_i >> base_shift)
    tri = iota((c, gc), 0) <= (iota((c, gc), 1) & (c - 1))
    same_s = (iota((gc, gs), 0) >> c_shift) == (iota((gc, gs), 1) >> HEAD_SHIFT)
    same_s2 = jnp.concatenate([same_s, same_s], axis=0)
    gnorm = gn_ref[...]

    def stack(pieces):
        return jnp.concatenate(pieces, axis=0)

    groups = []
    for gi in range(n_groups):
        heads = range(gi * g, (gi + 1) * g)
        lane = lambda base, h: slice(base + h * HEAD_DIM, base + (h + 1) * HEAD_DIM)
        q_st = stack([qkv_ref[0, :, lane(0, h)] for h in heads])
        k_st = stack([qkv_ref[0, :, lane(GDN_W, h)] for h in heads])
        v_st = stack([qkv_ref[0, :, lane(2 * GDN_W, h)] for h in heads])
        gc_st = stack([gcum[:, h:h + 1] for h in heads])
        bt_st = stack([gb[:, N_GDN_HEADS + h:N_GDN_HEADS + h + 1] for h in heads])
        gl_st = stack([jnp.broadcast_to(gcum[c - 1:c, h:h + 1], (c, 1)) for h in heads])
        gl_s = stack([jnp.broadcast_to(gcum[c - 1:c, h:h + 1], (HEAD_DIM, 1)) for h in heads])
        expand = jnp.where(iota((LANES, gc), 0) == (iota((LANES, gc), 1) >> c_shift) + gi * g, 1.0, 0.0)
        spread = jnp.dot(gb, expand, preferred_element_type=F32, precision=HIGHEST)
        gr_st = jnp.sum(jnp.where(tri, spread, 0.0), axis=0, keepdims=True)
        decay = jnp.exp(jnp.where(incl, gc_st - gr_st, -jnp.inf))
        k16 = k_st.astype(BF16)
        qk_kk = _dot_nt(jnp.concatenate([q_st, k_st], axis=0).astype(BF16), k16)
        a_mat = jnp.where(strict, bt_st * qk_kk[gc:2 * gc] * decay, 0.0)
        a_base = jnp.where(base_blk, a_mat, 0.0)
        groups.append(dict(q=q_st, k=k_st, v=v_st, gc=gc_st, bt=bt_st, gl=gl_st, gl_s=gl_s,
                           qk=qk_kk[0:gc] * decay, a=a_mat, t=eye - a_base, p=a_base, heads=heads))

    for _ in range(base_shift - 1):
        for gr in groups:
            p16 = gr['p'].astype(BF16)
            gr['p'] = _dot(p16, p16)
        for gr in groups:
            gr['t'] = gr['t'] + _dot(gr['t'].astype(BF16), gr['p'].astype(BF16))
    for lvl in range(base_shift, c_shift):
        off = ((r_i >> (lvl + 1)) == (c_i >> (lvl + 1))) & ((r_i >> lvl) != (c_i >> lvl))
        for gr in groups:
            t16 = gr['t'].astype(BF16)
            gr['at'] = _dot(jnp.where(off, gr['a'], 0.0).astype(BF16), t16)
        for gr in groups:
            gr['t'] = gr['t'] - _dot(gr['t'].astype(BF16), gr['at'].astype(BF16))

    for gi, gr in enumerate(groups):
        eg = jnp.exp(gr['gc'])
        rhs = jnp.concatenate([gr['k'] * (gr['bt'] * eg), gr['v'] * gr['bt']], axis=1).astype(BF16)
        gr['wu'] = _dot(gr['t'].astype(BF16), rhs)
        gr['qe'] = gr['q'] * eg
    for gi, gr in enumerate(groups):
        s_rows = slice(gi * gs, (gi + 1) * gs)
        s = s_ref[0, s_rows, :]
        wq = jnp.concatenate([gr['wu'][:, 0:HEAD_DIM], gr['qe']], axis=0)
        wq_bd = jnp.where(same_s2, jnp.concatenate([wq] * g, axis=1), 0.0).astype(BF16)
        ws_qs = _dot(wq_bd, s.astype(BF16))
        u = gr['wu'][:, HEAD_DIM:2 * HEAD_DIM] - ws_qs[0:gc]
        u16 = u.astype(BF16)
        o = ws_qs[gc:2 * gc] + _dot(gr['qk'].astype(BF16), u16)
        kd = gr['k'] * jnp.exp(gr['gl'] - gr['gc'])
        kd_bd = jnp.where(same_s, jnp.concatenate([kd] * g, axis=1), 0.0).astype(BF16)
        s_ref[0, s_rows, :] = s * jnp.exp(gr['gl_s']) + _dot_tn(kd_bd, u16)
        on = o * lax.rsqrt(jnp.mean(o * o, axis=-1, keepdims=True) + NORM_EPS) * gnorm
        for j, h in enumerate(gr['heads']):
            sl = slice(h * HEAD_DIM, (h + 1) * HEAD_DIM)
            zh = z_ref[0, :, sl]
            o_ref[0, :, sl] = on[j * c:(j + 1) * c] * (zh * _sigmoid(zh))


def _gdn_scan(qkv3, gb3, z3, s0, gnorm, c, g):
    b, tp, _ = qkv3.shape
    n = tp // c
    kern = functools.partial(_gdn_scan_kernel, c=c, g=g)
    state_rows = N_GDN_HEADS * HEAD_DIM
    state_spec = pl.BlockSpec((1, state_rows, HEAD_DIM), lambda bi, i: (bi, 0, 0))
    o, s_new = pl.pallas_call(
        kern,
        grid=(b, n),
        in_specs=[pl.BlockSpec((1, c, 3 * GDN_W), lambda bi, i: (bi, i, 0)),
                  pl.BlockSpec((1, c, LANES), lambda bi, i: (bi, i, 0)),
                  pl.BlockSpec((1, c, GDN_W), lambda bi, i: (bi, i, 0)),
                  state_spec,
                  pl.BlockSpec((1, HEAD_DIM), lambda bi, i: (0, 0))],
        out_specs=(pl.BlockSpec((1, c, GDN_W), lambda bi, i: (bi, i, 0)), state_spec),
        out_shape=(jax.ShapeDtypeStruct((b, tp, GDN_W), F32),
                   jax.ShapeDtypeStruct((b, state_rows, HEAD_DIM), F32)),
        compiler_params=_cparams(("parallel", "arbitrary")),
        name="gdn_scan",
    )(qkv3, gb3, z3, s0.reshape(b, state_rows, HEAD_DIM), gnorm)
    return o, s_new.reshape(b, N_GDN_HEADS, HEAD_DIM, HEAD_DIM)


def _compress(readers, pe_ref, wbd_ref, cc, cs):
    acc = None
    for c in range(CMP_BLOCK):
        rows = jnp.concatenate([rd(parity, c) for rd in readers for parity in range(2)], axis=0)
        part = _dot((rows + pe_ref[c:c + 1, :]).astype(BF16), wbd_ref[c])
        acc = part if acc is None else acc + part
    return _rope128(acc, cc, cs)


def _softmax_rows(s, mask):
    s = jnp.where(mask, s, -jnp.inf)
    m = jnp.max(s, axis=-1, keepdims=True)
    m = jnp.where(m > -jnp.inf, m, 0.0)
    e = jnp.exp(s - m)
    return e / jnp.maximum(jnp.sum(e, axis=-1, keepdims=True), 1e-30)


def _select_blocks(imp, qpos, n_sel, blk_axis):
    blk = lax.broadcasted_iota(jnp.int32, imp.shape, blk_axis)
    cur = qpos >> HEAD_SHIFT
    forced = (blk == 0) | (blk == cur) | (blk == cur - 1)
    valid = (blk * SEL_BLOCK <= qpos) & (blk < n_sel)
    score = jnp.where(forced, jnp.inf, jnp.where(valid, imp, -jnp.inf))
    rank = jnp.zeros(imp.shape, F32)
    for i in range(min(n_sel, N_SCORE)):
        si = score[i:i + 1, :] if blk_axis == 0 else score[:, i:i + 1]
        ahead = (si > score) | ((si == score) & (blk > i))
        rank = rank + jnp.where(ahead, 1.0, 0.0)
    return ((rank < TOP_K) & (blk < n_sel)).astype(BF16)


def _cmp_branch_rows(q_c, ckv16, qpos, qpos_t, n_half, n_sel, tq):
    r = q_c.shape[0]
    nc = 2 * n_half
    col = lax.broadcasted_iota(jnp.int32, (r, nc), 1)
    cmp_end = (2 * (col & (n_half - 1)) + (col >> int(math.log2(n_half))) + 1) * CMP_BLOCK - 1
    p_c = _softmax_rows(_dot_nt(q_c, ckv16), cmp_end <= qpos)
    o_c = _dot(p_c.astype(BF16), ckv16)[:, HEAD_DIM:2 * HEAD_DIM]
    pair = p_c[:, 0:n_half] + p_c[:, n_half:nc]
    imp = pair[0:tq]
    for g in range(1, GQA_GROUP):
        imp = imp + pair[g * tq:(g + 1) * tq]
    if n_half < N_SCORE:
        imp = jnp.concatenate([imp, jnp.zeros((tq, N_SCORE - n_half), F32)], axis=1)
    sel = _select_blocks(imp, qpos_t, n_sel, 1)
    sel_r = jnp.concatenate([sel] * GQA_GROUP, axis=0)
    return o_c, lambda expand: _dot(sel_r, expand)


def _softmax_cols(s, mask):
    s = jnp.where(mask, s, -jnp.inf)
    m = jnp.max(s, axis=0, keepdims=True)
    m = jnp.where(m > -jnp.inf, m, 0.0)
    e = jnp.exp(s - m)
    return e / jnp.maximum(jnp.sum(e, axis=0, keepdims=True), 1e-30)


def _nsa_core_lanes(q_blk, t0, ckv, n_half, n_sel, ks_rows, ks_t, n_full, kc, win_rows, win_t, win_pos0, kvh,
                    gates, sel_ref):
    tq = q_blk.shape[0]
    r = GQA_GROUP * tq
    nc = 2 * n_half
    q_all = jnp.concatenate([q_blk[:, g * HEAD_DIM:(g + 1) * HEAD_DIM] for g in range(GQA_GROUP)], axis=0)
    zeros64 = jnp.zeros((r, HEAD_DIM), F32)
    q_c = jnp.concatenate([q_all, zeros64], axis=1)
    q_c16 = q_c.astype(BF16)
    q_ct = q_c.T.astype(BF16)
    ckv16 = ckv.astype(BF16)
    qpos_l = (lax.broadcasted_iota(jnp.int32, (1, r), 1) & (tq - 1)) + t0

    row = lax.broadcasted_iota(jnp.int32, (nc, r), 0)
    cmp_end = (2 * (row & (n_half - 1)) + (row >> int(math.log2(n_half))) + 1) * CMP_BLOCK - 1
    p_c = _softmax_cols(_dot_nt(ckv16, q_c16), cmp_end <= qpos_l)
    o_ct = _dot_tn(ckv16, p_c.astype(BF16))
    pair = p_c[0:n_half] + p_c[n_half:nc]
    imp = pair[:, 0:tq]
    for g in range(1, GQA_GROUP):
        imp = imp + pair[:, g * tq:(g + 1) * tq]
    if n_half < N_SCORE:
        imp = jnp.concatenate([imp, jnp.zeros((N_SCORE - n_half, tq), F32)], axis=0)
    sel = _select_blocks(imp, qpos_l[:, 0:tq], n_sel, 0).astype(F32)
    sel_ref[...] = jnp.concatenate([sel] * GQA_GROUP, axis=1)

    blocks_per_chunk = kc // SEL_BLOCK
    krow = lax.broadcasted_iota(jnp.int32, (SEL_BLOCK, r), 0)

    def body(c, carry, causal):
        m, l, acc = carry
        s = _dot(ks_rows(c), q_ct)
        sel_c = sel_ref[pl.ds(pl.multiple_of(c * blocks_per_chunk, blocks_per_chunk), blocks_per_chunk), :]
        parts = []
        for j in range(blocks_per_chunk):
            mask = jnp.broadcast_to(sel_c[j:j + 1, :], (SEL_BLOCK, r)) > 0.5
            if causal:
                mask = mask & (krow + (c * kc + j * SEL_BLOCK) <= qpos_l)
            parts.append(jnp.where(mask, s[j * SEL_BLOCK:(j + 1) * SEL_BLOCK], -jnp.inf))
        s = jnp.concatenate(parts, axis=0)
        m_new = jnp.maximum(m, jnp.max(s, axis=0, keepdims=True))
        m_safe = jnp.where(m_new > -jnp.inf, m_new, 0.0)
        alpha = jnp.exp(m - m_safe)
        p = jnp.exp(s - m_safe)
        l = alpha * l + jnp.sum(p, axis=0, keepdims=True)
        acc = alpha * acc + _dot(ks_t(c), p.astype(BF16))
        return m_new, l, acc

    carry = (jnp.full((1, r), -jnp.inf, F32), jnp.zeros((1, r), F32), jnp.zeros((LANES, r), F32))
    carry = lax.fori_loop(0, n_full, lambda c, cr: body(c, cr, False), carry)
    _, l_s, acc_s = body(n_full, carry, True)
    o_st = acc_s / jnp.maximum(l_s, 1e-30)

    lane = lax.broadcasted_iota(jnp.int32, (r, ROW_W), 1)
    q_w = jnp.where((lane >> HEAD_SHIFT) == kvh, jnp.concatenate([q_all, q_all, zeros64, zeros64], axis=1), 0.0)
    nw = win_rows.shape[0]
    kpos = lax.broadcasted_iota(jnp.int32, (nw, r), 0) + win_pos0
    p_w = _softmax_cols(_dot(win_rows, q_w.T.astype(BF16)), (kpos <= qpos_l) & (kpos >= qpos_l - WINDOW))
    o_wt2 = _dot(win_t, p_w.astype(BF16))
    o_wt = jnp.where(kvh == 0, o_wt2[2 * HEAD_DIM:3 * HEAD_DIM], o_wt2[3 * HEAD_DIM:4 * HEAD_DIM])

    gates_t = gates.T
    outs = []
    for g in range(GQA_GROUP):
        cols = slice(g * tq, (g + 1) * tq)
        g0 = _sigmoid(gates_t[3 * g:3 * g + 1, :])
        g1 = _sigmoid(gates_t[3 * g + 1:3 * g + 2, :])
        g2 = _sigmoid(gates_t[3 * g + 2:3 * g + 3, :])
        outs.append(g0 * o_ct[HEAD_DIM:2 * HEAD_DIM, cols] + g1 * o_st[HEAD_DIM:2 * HEAD_DIM, cols]
                    + g2 * o_wt[:, cols])
    return jnp.concatenate(outs, axis=0).T


def _nsa_core_rows(q_blk, t0, ckv, n_half, sel_chunk, n_full, kc, n_sel, win_rows, win_pos0, kvh, gates):
    tq = q_blk.shape[0]
    r = GQA_GROUP * tq
    q_all = jnp.concatenate([q_blk[:, g * HEAD_DIM:(g + 1) * HEAD_DIM] for g in range(GQA_GROUP)], axis=0)
    qpos_t = lax.broadcasted_iota(jnp.int32, (tq, 1), 0) + t0
    qpos = jnp.concatenate([qpos_t] * GQA_GROUP, axis=0)
    zeros64 = jnp.zeros((r, HEAD_DIM), F32)
    q_c = jnp.concatenate([q_all, zeros64], axis=1).astype(BF16)
    ckv16 = ckv.astype(BF16)
    o_c, picked_fn = _cmp_branch_rows(q_c, ckv16, qpos, qpos_t, n_half, n_sel, tq)

    j_i = lax.broadcasted_iota(jnp.int32, (N_SCORE, kc), 0)
    k_i = lax.broadcasted_iota(jnp.int32, (N_SCORE, kc), 1) >> HEAD_SHIFT
    kcol = lax.broadcasted_iota(jnp.int32, (r, kc), 1)

    def body(c, carry, causal):
        m, l, acc = carry
        rows = sel_chunk(c)
        s = _dot_nt(q_c, rows)
        expand = jnp.where(j_i == k_i + c * (kc // SEL_BLOCK), 1.0, 0.0).astype(BF16)
        mask = picked_fn(expand) > 0.5
        if causal:
            mask = mask & (kcol + c * kc <= qpos)
        s = jnp.where(mask, s, -jnp.inf)
        m_new = jnp.maximum(m, jnp.max(s, axis=-1, keepdims=True))
        m_safe = jnp.where(m_new > -jnp.inf, m_new, 0.0)
        alpha = jnp.exp(m - m_safe)
        p = jnp.exp(s - m_safe)
        l = alpha * l + jnp.sum(p, axis=-1, keepdims=True)
        acc = alpha * acc + _dot(p.astype(BF16), rows)
        return m_new, l, acc

    carry = (jnp.full((r, 1), -jnp.inf, F32), jnp.zeros((r, 1), F32), jnp.zeros((r, LANES), F32))
    if not isinstance(n_full, int) or n_full > 0:
        carry = lax.fori_loop(0, n_full, lambda c, cr: body(c, cr, False), carry)
    _, l_s, acc_s = body(n_full, carry, True)
    o_s = acc_s[:, HEAD_DIM:2 * HEAD_DIM] / jnp.maximum(l_s, 1e-30)

    lane = lax.broadcasted_iota(jnp.int32, (r, ROW_W), 1)
    q_w = jnp.where((lane >> HEAD_SHIFT) == kvh, jnp.concatenate([q_all, q_all, zeros64, zeros64], axis=1), 0.0)
    q_w = q_w.astype(BF16)
    nw = win_rows.shape[0]
    kpos = lax.broadcasted_iota(jnp.int32, (r, nw), 1) + win_pos0
    p_w = _softmax_rows(_dot_nt(q_w, win_rows), (kpos <= qpos) & (kpos >= qpos - WINDOW))
    o_w2 = _dot(p_w.astype(BF16), win_rows)
    o_w = jnp.where(kvh == 0, o_w2[:, 2 * HEAD_DIM:3 * HEAD_DIM], o_w2[:, 3 * HEAD_DIM:4 * HEAD_DIM])

    outs = []
    for g in range(GQA_GROUP):
        rows_g = slice(g * tq, (g + 1) * tq)
        g0 = _sigmoid(gates[:, 3 * g:3 * g + 1])
        g1 = _sigmoid(gates[:, 3 * g + 1:3 * g + 2])
        g2 = _sigmoid(gates[:, 3 * g + 2:3 * g + 3])
        outs.append(g0 * o_c[rows_g] + g1 * o_s[rows_g] + g2 * o_w[rows_g])
    return outs


def _nsa_prompt_kernel(q_ref, kc_ref, ks_ref, kst_ref, win_ref, wint_ref, gate_ref, wbd_ref, pe_ref, cc_ref, cs_ref,
                       o_ref, ckv_ref, sel_ref, *, tq, t, kc):
    kvh = pl.program_id(1)
    i = pl.program_id(2)
    n_half = t // (2 * CMP_BLOCK)

    @pl.when(i == 0)
    def _():
        read = lambda parity, c: kc_ref[0, pl.ds(parity * CMP_BLOCK + c, n_half, stride=2 * CMP_BLOCK), :]
        ckv_ref[...] = _compress([read], pe_ref, wbd_ref, cc_ref[...], cs_ref[...])

    t0 = i * tq
    ks_rows = lambda c: ks_ref[0, pl.ds(pl.multiple_of(c * kc, kc), kc), :].astype(BF16)
    ks_t = lambda c: kst_ref[0, c].astype(BF16)
    span = WINDOW + tq
    start = pl.multiple_of(jnp.maximum(t0 - WINDOW, 0), tq)
    win_rows = win_ref[pl.ds(start, span), :].astype(BF16)
    blk0 = start // LANES
    win_t = jnp.concatenate([wint_ref[blk0 + j] for j in range(span // LANES)], axis=1).astype(BF16)
    o_ref[...] = _nsa_core_lanes(q_ref[...], t0, ckv_ref[...], n_half, t // SEL_BLOCK, ks_rows, ks_t, t0 // kc, kc,
                                 win_rows, win_t, start, kvh, gate_ref[...], sel_ref)


def _nsa_prompt(q, kvc, kvs, kst, win, wint, gate, wbd, pe, cc, cs, b, t, tq=128):
    n = t // tq
    nc = t // CMP_BLOCK
    kc = kst.shape[3]
    assert tq % LANES == 0 and kc % tq == 0 and t % kc == 0
    kern = functools.partial(_nsa_prompt_kernel, tq=tq, t=t, kc=kc)
    fix2 = lambda bi, h, i: (0, 0)
    return pl.pallas_call(
        kern,
        grid=(b, N_KV_HEADS, n),
        in_specs=[pl.BlockSpec((tq, ROW_W), lambda bi, h, i: (bi * n + i, h)),
                  pl.BlockSpec((1, t, LANES), lambda bi, h, i: (h, bi, 0)),
                  pl.BlockSpec((1, t, LANES), lambda bi, h, i: (h, bi, 0)),
                  pl.BlockSpec((1, t // kc, LANES, kc), lambda bi, h, i: (h, bi, 0, 0)),
                  pl.BlockSpec((t, ROW_W), lambda bi, h, i: (bi, 0)),
                  pl.BlockSpec((t // LANES, ROW_W, LANES), lambda bi, h, i: (bi, 0, 0)),
                  pl.BlockSpec((tq, LANES), lambda bi, h, i: (bi * n + i, h)),
                  pl.BlockSpec((CMP_BLOCK, LANES, LANES), lambda bi, h, i: (0, 0, 0)),
                  pl.BlockSpec((CMP_BLOCK, LANES), fix2),
                  pl.BlockSpec((nc, LANES), fix2),
                  pl.BlockSpec((nc, LANES), fix2)],
        out_specs=pl.BlockSpec((tq, ROW_W), lambda bi, h, i: (bi * n + i, h)),
        out_shape=jax.ShapeDtypeStruct((b * t, NSA_W), F32),
        scratch_shapes=[pltpu.VMEM((nc, LANES), F32), pltpu.VMEM((N_SCORE, GQA_GROUP * tq), F32)],
        compiler_params=_cparams(("parallel", "parallel", "arbitrary")),
        name="nsa_prompt",
    )(q, kvc, kvs, kst, win, wint, gate, wbd, pe, cc, cs)


def _nsa_sample_kernel(pt_ref, *refs, n_steps, pages_per_step, past, tq, ts, n_pad, nw):
    page_refs = refs[:pages_per_step]
    (perm_ref, q_ref, kvnew_ref, wbuf_ref, wnew_ref, gate_ref, wbd_ref, pe_ref, cc_ref, cs_ref,
     o_ref, wout_ref, pge_ref, pgo_ref, pgs_ref, wn_ref) = refs[pages_per_step:]
    j = pl.program_id(1)
    half = perm_ref.shape[0] // 2
    per_parity = half // CMP_BLOCK
    slot = pl.ds(pl.multiple_of(j * per_parity, per_parity), per_parity)
    for kvh in range(N_KV_HEADS):
        for k, page_ref in enumerate(page_refs):
            rows = pl.ds(pl.multiple_of((j * pages_per_step + k) * PAGE_SIZE, PAGE_SIZE), PAGE_SIZE)
            pgs_ref[kvh, rows, :] = page_ref[0, 0, kvh, LANES:ROW_W, :].T
        cmp_t = jnp.concatenate([page_ref[0, 0, kvh, 0:LANES, :] for page_ref in page_refs], axis=1)
        grouped = _dot_nt(perm_ref[...], cmp_t.astype(BF16))
        for c in range(CMP_BLOCK):
            pge_ref[kvh, c, slot, :] = grouped[c * per_parity:(c + 1) * per_parity]
            pgo_ref[kvh, c, slot, :] = grouped[half + c * per_parity:half + (c + 1) * per_parity]

    @pl.when(j == n_steps - 1)
    def _():
        n_half = past // (2 * CMP_BLOCK)
        wbuf_t = wbuf_ref[0, 0]
        wnew = wnew_ref[0]
        wn_ref[0:WINDOW, :] = wbuf_t.T
        wn_ref[WINDOW:WINDOW + tq, :] = wnew
        wn_ref[WINDOW + tq:nw, :] = jnp.zeros((nw - WINDOW - tq, ROW_W), F32)
        win_rows = wn_ref[...].astype(BF16)

        shifted = pltpu.roll(wbuf_t, WINDOW - ts, 1)
        new_t = jnp.concatenate([wnew, jnp.zeros((LANES - tq, ROW_W), F32)], axis=0).T
        lane = lax.broadcasted_iota(jnp.int32, (ROW_W, LANES), 1)
        tail = jnp.where(lane >= LANES - ts, pltpu.roll(new_t, LANES - ts, 1), shifted[:, WINDOW - LANES:WINDOW])
        wout_ref[0, :, 0:WINDOW - LANES] = shifted[:, 0:WINDOW - LANES]
        wout_ref[0, :, WINDOW - LANES:WINDOW] = tail

        readers = []
        for kvh in range(N_KV_HEADS):
            pgs_ref[kvh, past:past + tq, :] = kvnew_ref[kvh, 0]
            pgs_ref[kvh, past + tq:n_pad, :] = jnp.zeros((n_pad - past - tq, LANES), F32)
            readers.append(lambda parity, c, kvh=kvh: (pgo_ref if parity else pge_ref)[kvh, c])
        ckv2 = _compress(readers, pe_ref, wbd_ref, cc_ref[...], cs_ref[...])
        for kvh in range(N_KV_HEADS):
            ckv = ckv2[kvh * 2 * n_half:(kvh + 1) * 2 * n_half]
            sel_chunk = lambda c, kvh=kvh: pgs_ref[kvh].astype(BF16)
            outs = _nsa_core_rows(q_ref[0, :, kvh * ROW_W:(kvh + 1) * ROW_W], past, ckv, n_half, sel_chunk, 0,
                             n_pad, past // SEL_BLOCK + 1, win_rows, past - WINDOW, kvh,
                             gate_ref[0, :, kvh * LANES:(kvh + 1) * LANES])
            for g in range(GQA_GROUP):
                col = kvh * ROW_W + g * HEAD_DIM
                o_ref[0, :, col:col + HEAD_DIM] = outs[g]


def _page_group_permutation(pages_per_step):
    n = pages_per_step * PAGE_SIZE
    half = n // 2
    per_parity = half // CMP_BLOCK
    r = np.arange(n)
    parity, c, i = r // half, (r % half) // per_parity, r % per_parity
    perm = np.zeros((n, n), np.float32)
    perm[r, (2 * i + parity) * CMP_BLOCK + c] = 1.0
    return jnp.asarray(perm, dtype=BF16)


def _nsa_sample(page_table, pool_t, layer, ts, q8, kvnew8, win_t, wnew8, gate8, wbd, pe, cc, cs, pages_per_step=4):
    b, n_pages = page_table.shape
    past = n_pages * PAGE_SIZE
    tq = q8.shape[1]
    n_pad = past + SEL_BLOCK
    nw = WINDOW + LANES
    nc2 = N_KV_HEADS * (past // CMP_BLOCK)
    n_steps = n_pages // pages_per_step
    n_half = past // (2 * CMP_BLOCK)
    perm = _page_group_permutation(pages_per_step)
    assert (pages_per_step * PAGE_SIZE // (2 * CMP_BLOCK)) % SUBLANES == 0
    kern = functools.partial(_nsa_sample_kernel, n_steps=n_steps, pages_per_step=pages_per_step, past=past,
                             tq=tq, ts=ts, n_pad=n_pad, nw=nw)
    per_b = lambda bi, j, pt: (bi, 0, 0)
    fix2 = lambda bi, j, pt: (0, 0)
    page_spec = lambda k: pl.BlockSpec((1, 1, N_KV_HEADS, ROW_W, PAGE_SIZE),
                                       lambda bi, j, pt: (layer, pt[bi, j * pages_per_step + k], 0, 0, 0))
    grid_spec = pltpu.PrefetchScalarGridSpec(
        num_scalar_prefetch=1,
        grid=(b, n_steps),
        in_specs=[page_spec(k) for k in range(pages_per_step)] + [
                  pl.BlockSpec(perm.shape, fix2),
                  pl.BlockSpec((1, tq, NSA_W), per_b),
                  pl.BlockSpec((N_KV_HEADS, 1, tq, LANES), lambda bi, j, pt: (0, bi, 0, 0)),
                  pl.BlockSpec((1, 1, ROW_W, WINDOW), lambda bi, j, pt: (layer, bi, 0, 0)),
                  pl.BlockSpec((1, tq, ROW_W), per_b),
                  pl.BlockSpec((1, tq, N_KV_HEADS * LANES), per_b),
                  pl.BlockSpec((CMP_BLOCK, LANES, LANES), lambda bi, j, pt: (0, 0, 0)),
                  pl.BlockSpec((CMP_BLOCK, LANES), fix2),
                  pl.BlockSpec((nc2, LANES), fix2),
                  pl.BlockSpec((nc2, LANES), fix2)],
        out_specs=(pl.BlockSpec((1, tq, NSA_W), per_b), pl.BlockSpec((1, ROW_W, WINDOW), per_b)),
        scratch_shapes=[pltpu.VMEM((N_KV_HEADS, CMP_BLOCK, n_half, LANES), F32),
                        pltpu.VMEM((N_KV_HEADS, CMP_BLOCK, n_half, LANES), F32),
                        pltpu.VMEM((N_KV_HEADS, n_pad, LANES), F32), pltpu.VMEM((nw, ROW_W), F32)],
    )
    return pl.pallas_call(
        kern,
        grid_spec=grid_spec,
        out_shape=(jax.ShapeDtypeStruct((b, tq, NSA_W), F32), jax.ShapeDtypeStruct((b, ROW_W, WINDOW), F32)),
        compiler_params=_cparams(("parallel", "arbitrary")),
        name="nsa_sample",
    )(page_table, *([pool_t] * pages_per_step), perm, q8, kvnew8, win_t, wnew8, gate8, wbd, pe, cc, cs)


def _mlp_kernel(x_ref, og_ref, on_ref, wo_ref, gm_ref, wu_ref, wd_ref, gf_ref, y_ref, x1_ref, h_ref, acc_ref,
                *, final_norm):
    f = pl.program_id(1)

    @pl.when(f == 0)
    def _():
        x1 = (x_ref[...] + _dot(og_ref[...].astype(BF16), wo_ref[0:GDN_W, :])
              + _dot(on_ref[...].astype(BF16), wo_ref[GDN_W:GDN_W + NSA_W, :]))
        x1_ref[...] = x1
        var = jnp.mean(x1 * x1, axis=-1, keepdims=True)
        h_ref[...] = (x1 * lax.rsqrt(var + NORM_EPS) * gm_ref[...]).astype(BF16)
        acc_ref[...] = jnp.zeros(acc_ref.shape, F32)

    up = jnp.maximum(_dot(h_ref[...], wu_ref[...]), 0.0)
    acc_ref[...] += _dot((up * up).astype(BF16), wd_ref[...])

    @pl.when(f == pl.num_programs(1) - 1)
    def _():
        y = x1_ref[...] + acc_ref[...]
        if final_norm:
            var = jnp.mean(y * y, axis=-1, keepdims=True)
            y = y * lax.rsqrt(var + NORM_EPS) * gf_ref[...]
        y_ref[...] = y


def _mlp(x, og, on, wo, gm, wu, wd, gf, final_norm, tm, tf=1024):
    m = x.shape[0]
    kern = functools.partial(_mlp_kernel, final_norm=final_norm)
    row = lambda i, f: (i, 0)
    fix = lambda i, f: (0, 0)
    return pl.pallas_call(
        kern,
        grid=(m // tm, D_FF // tf),
        in_specs=[pl.BlockSpec((tm, D_MODEL), row), pl.BlockSpec((tm, GDN_W), row), pl.BlockSpec((tm, NSA_W), row),
                  pl.BlockSpec((D_MODEL, D_MODEL), fix), pl.BlockSpec((1, D_MODEL), fix),
                  pl.BlockSpec((D_MODEL, tf), lambda i, f: (0, f)), pl.BlockSpec((tf, D_MODEL), lambda i, f: (f, 0)),
                  pl.BlockSpec((1, D_MODEL), fix)],
        out_specs=pl.BlockSpec((tm, D_MODEL), row),
        out_shape=jax.ShapeDtypeStruct((m, D_MODEL), F32),
        scratch_shapes=[pltpu.VMEM((tm, D_MODEL), F32), pltpu.VMEM((tm, D_MODEL), BF16),
                        pltpu.VMEM((tm, D_MODEL), F32)],
        compiler_params=_cparams(("parallel", "arbitrary")),
        name="mlp",
    )(x, og, on, wo, gm, wu, wd, gf)


def _proj_columns():
    src = np.full((N_PROJ,), -1, np.int64)
    o_z, o_a, o_b = 3 * GDN_W, 4 * GDN_W, 4 * GDN_W + N_GDN_HEADS
    o_q = o_b + N_GDN_HEADS
    o_kv = o_q + NSA_W
    o_g = o_kv + 6 * KV_W
    src[C_QKV:C_Z] = np.arange(0, 3 * GDN_W)
    src[C_Z:C_Q] = np.arange(o_z, o_z + GDN_W)
    src[C_Q:C_KV] = np.arange(o_q, o_q + NSA_W)
    for kvh in range(N_KV_HEADS):
        for s in range(4):
            dst = C_KV + kvh * ROW_W + s * HEAD_DIM
            src[dst:dst + HEAD_DIM] = o_kv + s * KV_W + kvh * HEAD_DIM + np.arange(HEAD_DIM)
    src[C_WIN:C_WIN + 2 * KV_W] = o_kv + 4 * KV_W + np.arange(2 * KV_W)
    src[C_AB:C_AB + N_GDN_HEADS] = o_a + np.arange(N_GDN_HEADS)
    src[C_AB + N_GDN_HEADS:C_AB + 2 * N_GDN_HEADS] = o_b + np.arange(N_GDN_HEADS)
    for kvh in range(N_KV_HEADS):
        dst = C_GATE + kvh * LANES
        src[dst:dst + 3 * GQA_GROUP] = o_g + kvh * 3 * GQA_GROUP + np.arange(3 * GQA_GROUP)
    return src


def _rope_tables(pos):
    half = HEAD_DIM // 2
    inv_freq = ROPE_THETA ** (-jnp.arange(half, dtype=F32) / half)
    ang = pos.astype(F32)[:, None] * inv_freq[None, :]
    cos, sin = jnp.cos(ang), jnp.sin(ang)
    c64 = jnp.concatenate([cos, cos], axis=1)
    s64 = jnp.concatenate([-sin, sin], axis=1)
    one, zero = jnp.ones_like(c64), jnp.zeros_like(s64)
    return (jnp.concatenate([c64, c64], axis=1), jnp.concatenate([s64, s64], axis=1),
            jnp.concatenate([c64, one], axis=1), jnp.concatenate([s64, zero], axis=1))


def _cmp_tables(n_blocks):
    n_half = n_blocks // 2
    r = np.arange(n_blocks)
    blk = 2 * (r % n_half) + r // n_half
    end_pos = jnp.asarray((blk + 1) * CMP_BLOCK - 1)
    _, _, cc, cs = _rope_tables(end_pos)
    return cc, cs


def _layer_params(l, norm_mix, w_in, conv_w, a_log, dt_bias, gdn_norm, cmp_pe_k, cmp_w_k, cmp_pe_v, cmp_w_v,
                  w_out, norm_mlp, w_up, w_down):
    src = _proj_columns()
    w_ext = jnp.concatenate([w_in[l], jnp.zeros((D_MODEL, 1), F32)], axis=1)
    w_p = jnp.take(w_ext, jnp.asarray(np.where(src < 0, w_in.shape[2], src)), axis=1).astype(BF16)
    zeros = jnp.zeros((CMP_BLOCK, HEAD_DIM, HEAD_DIM), F32)
    wbd = jnp.concatenate([jnp.concatenate([cmp_w_k[l], zeros], axis=2),
                           jnp.concatenate([zeros, cmp_w_v[l]], axis=2)], axis=1).astype(BF16)
    pad_row = lambda v: jnp.pad(v.astype(F32), (0, LANES - v.shape[0]))[None, :]
    return dict(
        norm_mix=norm_mix[l][None, :], w_p=w_p, conv_w=conv_w[l], alog=pad_row(a_log[l]), dtb=pad_row(dt_bias[l]),
        gdn_norm=gdn_norm[l][None, :], wbd=wbd, pe=jnp.concatenate([cmp_pe_k[l], cmp_pe_v[l]], axis=1),
        w_out=w_out[l].astype(BF16), norm_mlp=norm_mlp[l][None, :], w_up=w_up[l].astype(BF16),
        w_down=w_down[l].astype(BF16))


def _prompt_layer(x, lp, tabs, cmp_tabs, gf, final_norm, b, t):
    qkv_pre, z, q, kvc, kvs, win, ab, gate, kst, wint = _proj(x, lp['norm_mix'], lp['w_p'], tabs, PROJ_ROWS)
    w3 = 3 * GDN_W
    cb8 = jnp.zeros((b, SUBLANES, w3), F32)
    qkv3, gb3 = _gdn_prep(qkv_pre.reshape(b, t, w3), cb8, ab.reshape(b, t, LANES), lp['conv_w'], lp['alog'],
                          lp['dtb'], 256, t, 256)
    s0 = jnp.zeros((b, N_GDN_HEADS, HEAD_DIM, HEAD_DIM), F32)
    o_gdn, s_new = _gdn_scan(qkv3, gb3, z.reshape(b, t, GDN_W), s0, lp['gdn_norm'], GDN_CHUNK, 4)
    o_nsa = _nsa_prompt(q, kvc, kvs, kst, win, wint, gate, lp['wbd'], lp['pe'], cmp_tabs[0], cmp_tabs[1], b, t)
    y = _mlp(x, o_gdn.reshape(b * t, GDN_W), o_nsa, lp['w_out'], lp['norm_mlp'], lp['w_up'], lp['w_down'], gf,
             final_norm, 512)
    rows = jnp.concatenate([kvc, kvs], axis=2).reshape(N_KV_HEADS, b, t, 4, HEAD_DIM).transpose(1, 0, 2, 3, 4)
    win_new = win.reshape(b, t, 2, N_KV_HEADS, HEAD_DIM)[:, t - min(WINDOW, t):]
    conv_new = qkv_pre.reshape(b, t, w3)[:, t - (CONV_W - 1):]
    return y, rows, win_new, s_new, conv_new


def _pad_rows(a, n):
    return jnp.pad(a, ((0, 0), (0, n - a.shape[1]), (0, 0)))


def _sample_layer(x, lp, tabs, cmp_tabs, gf, final_norm, b, t, layer, pool_t, page_table, win_t, s0, conv_buf):
    m = b * t
    qkv_pre, z, q, kvc, kvs, win, ab, gate, _, _ = _proj(x, lp['norm_mix'], lp['w_p'], tabs, PROJ_ROWS)
    w3 = 3 * GDN_W
    tp = SUBLANES
    cb8 = jnp.pad(conv_buf, ((0, 0), (SUBLANES - (CONV_W - 1), 0), (0, 0)))
    qkv3, gb3 = _gdn_prep(_pad_rows(qkv_pre.reshape(b, t, w3), tp), cb8, _pad_rows(ab.reshape(b, t, LANES), tp),
                          lp['conv_w'], lp['alog'], lp['dtb'], tp, t, tp)
    o_gdn, s_new = _gdn_scan(qkv3, gb3, _pad_rows(z.reshape(b, t, GDN_W), tp), s0, lp['gdn_norm'], tp,
                             N_GDN_HEADS)
    o_gdn = o_gdn[:, :t].reshape(m, GDN_W)
    kvnew8 = jnp.pad(kvs.reshape(N_KV_HEADS, b, t, LANES), ((0, 0), (0, 0), (0, tp - t), (0, 0)))
    o_nsa, win_out_t = _nsa_sample(page_table, pool_t, layer, t, _pad_rows(q.reshape(b, t, NSA_W), tp), kvnew8,
                                   win_t, _pad_rows(win.reshape(b, t, ROW_W), tp),
                                   _pad_rows(gate.reshape(b, t, N_KV_HEADS * LANES), tp), lp['wbd'], lp['pe'],
                                   cmp_tabs[0], cmp_tabs[1])
    o_nsa = o_nsa[:, :t].reshape(m, NSA_W)
    y = _mlp(x, o_gdn, o_nsa, lp['w_out'], lp['norm_mlp'], lp['w_up'], lp['w_down'], gf, final_norm, 512)
    rows = jnp.concatenate([kvc, kvs], axis=2).reshape(N_KV_HEADS, b, t, 4, HEAD_DIM).transpose(1, 0, 2, 3, 4)
    win_new = win_out_t.reshape(b, 2, N_KV_HEADS, HEAD_DIM, WINDOW).transpose(0, 4, 1, 2, 3)
    conv_new = jnp.concatenate([conv_buf, qkv_pre.reshape(b, t, w3)], axis=1)[:, t:]
    return y, rows, win_new, s_new, conv_new


def kernel(x_prompt, x_sample, cache_kv, page_table, state_win, state_gdn, state_conv, norm_mix, w_in, conv_w, a_log, dt_bias, gdn_norm, cmp_pe_k, cmp_w_k, cmp_pe_v, cmp_w_v, w_out, norm_mlp, w_up, w_down, norm_final):
    bp, tp_, _ = x_prompt.shape
    bs, ts, _ = x_sample.shape
    depth = cache_kv.shape[0]
    n_pages = page_table.shape[1]
    past = n_pages * PAGE_SIZE
    assert state_win.shape[2] == WINDOW and tp_ % 512 == 0 and ts <= SUBLANES and past % (2 * CMP_BLOCK) == 0

    tabs_p = _rope_tables(jnp.arange(tp_))
    tabs_s = tuple(jnp.tile(tb, (bs, 1)) for tb in _rope_tables(past + jnp.arange(ts)))
    cmp_p = _cmp_tables(tp_ // CMP_BLOCK)
    cmp_s = tuple(jnp.tile(tb, (N_KV_HEADS, 1)) for tb in _cmp_tables((past + ts) // CMP_BLOCK))
    pool_t = cache_kv.transpose(0, 1, 2, 4, 5, 3).reshape(depth, cache_kv.shape[1], N_KV_HEADS, ROW_W, PAGE_SIZE)
    win_t = state_win.transpose(0, 1, 3, 4, 5, 2).reshape(depth, bs, ROW_W, WINDOW)
    gf = norm_final[None, :]

    xp = x_prompt.reshape(bp * tp_, D_MODEL)
    xs = x_sample.reshape(bs * ts, D_MODEL)
    outs = [[] for _ in range(8)]
    for l in range(depth):
        lp = _layer_params(l, norm_mix, w_in, conv_w, a_log, dt_bias, gdn_norm, cmp_pe_k, cmp_w_k, cmp_pe_v,
                           cmp_w_v, w_out, norm_mlp, w_up, w_down)
        last = l == depth - 1
        xp, r, w, s, c = _prompt_layer(xp, lp, tabs_p, cmp_p, gf, last, bp, tp_)
        for k, v in zip((0, 2, 4, 6), (r, w, s, c)):
            outs[k].append(v)
        xs, r, w, s, c = _sample_layer(xs, lp, tabs_s, cmp_s, gf, last, bs, ts, l, pool_t, page_table,
                                       win_t, state_gdn[l], state_conv[l])
        for k, v in zip((1, 3, 5, 7), (r, w, s, c)):
            outs[k].append(v)
    return (xp.reshape(bp, tp_, D_MODEL), xs.reshape(bs, ts, D_MODEL)) + tuple(jnp.stack(o) for o in outs)
```

```python
import functools
import math

import numpy as np
import jax
import jax.numpy as jnp
from jax import lax
from jax.experimental import pallas as pl
from jax.experimental.pallas import tpu as pltpu

F32 = jnp.float32
BF16 = jnp.bfloat16
HIGHEST = lax.Precision.HIGHEST

D_MODEL = 1024
HEAD_DIM = 64
N_GDN_HEADS = 8
GDN_W = N_GDN_HEADS * HEAD_DIM
N_NSA_HEADS = 8
NSA_W = N_NSA_HEADS * HEAD_DIM
N_KV_HEADS = 2
GQA_GROUP = N_NSA_HEADS // N_KV_HEADS
KV_W = N_KV_HEADS * HEAD_DIM
CONV_W = 4
GDN_CHUNK = 64
CMP_BLOCK = 32
SEL_BLOCK = 64
TOP_K = 16
WINDOW = 512
D_FF = 4 * D_MODEL
ROPE_THETA = 10000.0
NORM_EPS = 1e-6
PAGE_SIZE = 128
N_SCORE = 64
HEAD_SHIFT = 6
INV_BASE_SHIFT = 3
QK_SCALE = HEAD_DIM ** -0.5

LANES = 128
SUBLANES = 8
VMEM_LIMIT = 56 * 1024 * 1024

C_QKV = 0
C_Z = 3 * GDN_W
C_Q = C_Z + GDN_W
C_KV = C_Q + NSA_W
C_WIN = C_KV + N_KV_HEADS * 4 * HEAD_DIM
C_AB = C_WIN + 2 * KV_W
C_GATE = C_AB + LANES
N_PROJ = C_GATE + N_KV_HEADS * LANES
ROW_W = 4 * HEAD_DIM
PROJ_ROWS = 512
SAMPLE_SEQS_PER_STEP = 8


def _cparams(sem):
    return pltpu.CompilerParams(dimension_semantics=sem, vmem_limit_bytes=VMEM_LIMIT)


def _sigmoid(x):
    return 1.0 / (1.0 + jnp.exp(-x))


def _dot(a, b):
    return jnp.dot(a, b, preferred_element_type=F32)


def _dot_nt(a, b):
    return lax.dot_general(a, b, (((1,), (1,)), ((), ())), preferred_element_type=F32)


def _dot_tn(a, b, precision=None):
    return lax.dot_general(a, b, (((0,), (0,)), ((), ())), preferred_element_type=F32, precision=precision)


def _rope128(v, cos, sin):
    lane = lax.broadcasted_iota(jnp.int32, v.shape, 1)
    first = (lane & (HEAD_DIM - 1)) < (HEAD_DIM // 2)
    swapped = jnp.where(first, pltpu.roll(v, LANES - HEAD_DIM // 2, 1), pltpu.roll(v, HEAD_DIM // 2, 1))
    return v * cos + swapped * sin


def _proj_kernel(x_ref, g_ref, w_ref, cf_ref, sf_ref, ch_ref, sh_ref,
                 qkv_ref, z_ref, q_ref, kc_ref, ks_ref, win_ref, ab_ref, gate_ref, kst_ref, wint_ref):
    x = x_ref[...]
    var = jnp.mean(x * x, axis=-1, keepdims=True)
    h = (x * lax.rsqrt(var + NORM_EPS) * g_ref[...]).astype(BF16)
    p = _dot(h, w_ref[...])
    qkv_ref[...] = p[:, C_QKV:C_Z]
    z_ref[...] = p[:, C_Z:C_Q]
    cf, sf, ch, sh = cf_ref[...], sf_ref[...], ch_ref[...], sh_ref[...]
    for j in range(NSA_W // LANES):
        q_ref[:, j * LANES:(j + 1) * LANES] = _rope128(p[:, C_Q + j * LANES:C_Q + (j + 1) * LANES], cf, sf) * QK_SCALE
    for kvh in range(N_KV_HEADS):
        base = C_KV + kvh * ROW_W
        kc_ref[kvh] = p[:, base:base + LANES]
        ks = _rope128(p[:, base + LANES:base + ROW_W], ch, sh)
        ks_ref[kvh] = ks
        kst_ref[kvh, 0] = ks.T
    wk = _rope128(p[:, C_WIN:C_WIN + LANES], cf, sf)
    wv = p[:, C_WIN + LANES:C_WIN + ROW_W]
    win_ref[:, 0:LANES] = wk
    win_ref[:, LANES:ROW_W] = wv
    for j in range(x.shape[0] // LANES):
        rows = slice(j * LANES, (j + 1) * LANES)
        wint_ref[j, 0:LANES, :] = wk[rows].T
        wint_ref[j, LANES:ROW_W, :] = wv[rows].T
    ab_ref[...] = p[:, C_AB:C_AB + LANES]
    gate_ref[...] = p[:, C_GATE:C_GATE + N_KV_HEADS * LANES]


def _proj(x, g, w, tabs, tm):
    m = x.shape[0]
    nt = tabs[0].shape[0] // tm
    row = lambda i: (i, 0)
    tab = lambda i: (i % nt, 0)
    fix = lambda i: (0, 0)
    out_shapes = (
        jax.ShapeDtypeStruct((m, 3 * GDN_W), F32),
        jax.ShapeDtypeStruct((m, GDN_W), F32),
        jax.ShapeDtypeStruct((m, NSA_W), F32),
        jax.ShapeDtypeStruct((N_KV_HEADS, m, LANES), F32),
        jax.ShapeDtypeStruct((N_KV_HEADS, m, LANES), F32),
        jax.ShapeDtypeStruct((m, ROW_W), F32),
        jax.ShapeDtypeStruct((m, LANES), F32),
        jax.ShapeDtypeStruct((m, N_KV_HEADS * LANES), F32),
        jax.ShapeDtypeStruct((N_KV_HEADS, m // tm, LANES, tm), F32),
        jax.ShapeDtypeStruct((m // LANES, ROW_W, LANES), F32),
    )
    return pl.pallas_call(
        _proj_kernel,
        grid=(m // tm,),
        in_specs=[pl.BlockSpec((tm, D_MODEL), row), pl.BlockSpec((1, D_MODEL), fix),
                  pl.BlockSpec((D_MODEL, N_PROJ), fix)] + [pl.BlockSpec((tm, LANES), tab)] * 4,
        out_specs=(pl.BlockSpec((tm, 3 * GDN_W), row), pl.BlockSpec((tm, GDN_W), row),
                   pl.BlockSpec((tm, NSA_W), row), pl.BlockSpec((N_KV_HEADS, tm, LANES), lambda i: (0, i, 0)),
                   pl.BlockSpec((N_KV_HEADS, tm, LANES), lambda i: (0, i, 0)),
                   pl.BlockSpec((tm, ROW_W), row), pl.BlockSpec((tm, LANES), row),
                   pl.BlockSpec((tm, N_KV_HEADS * LANES), row),
                   pl.BlockSpec((N_KV_HEADS, 1, LANES, tm), lambda i: (0, i, 0, 0)),
                   pl.BlockSpec((tm // LANES, ROW_W, LANES), lambda i: (i, 0, 0))),
        out_shape=out_shapes,
        compiler_params=_cparams(("parallel",)),
        name="proj",
    )(x, g, w, *tabs)


def _gdn_prep_kernel(x_ref, prev_ref, cb_ref, ab_ref, cw_ref, alog_ref, dtb_ref,
                     qkv_ref, gb_ref, xs_ref, *, tt, t_valid, t_out):
    for s in range(x_ref.shape[0]):
        _gdn_prep_one(x_ref.at[s], prev_ref.at[s], cb_ref.at[s], ab_ref.at[s], cw_ref, alog_ref, dtb_ref,
                      qkv_ref.at[s], gb_ref.at[s], xs_ref, tt=tt, t_valid=t_valid, t_out=t_out)


def _gdn_prep_one(x_ref, prev_ref, cb_ref, ab_ref, cw_ref, alog_ref, dtb_ref, qkv_ref, gb_ref, xs_ref,
                  *, tt, t_valid, t_out):
    i = pl.program_id(1)
    xs_ref[SUBLANES:SUBLANES + tt, :] = x_ref[...]

    @pl.when(i == 0)
    def _():
        xs_ref[0:SUBLANES, :] = cb_ref[...]

    @pl.when(i > 0)
    def _():
        xs_ref[0:SUBLANES, :] = prev_ref[...]

    conv = xs_ref[SUBLANES:SUBLANES + tt, :] * cw_ref[CONV_W - 1:CONV_W, :]
    for k in range(1, CONV_W):
        conv = conv + xs_ref[SUBLANES - k:SUBLANES - k + tt, :] * cw_ref[CONV_W - 1 - k:CONV_W - k, :]
    c = conv * _sigmoid(conv)

    rows = lax.broadcasted_iota(jnp.int32, (tt, LANES), 0) + i * tt
    live = rows < t_valid
    r_i = lax.broadcasted_iota(jnp.int32, (LANES, LANES), 0)
    c_i = lax.broadcasted_iota(jnp.int32, (LANES, LANES), 1)
    head_ones = ((r_i >> HEAD_SHIFT) == (c_i >> HEAD_SHIFT)).astype(BF16)

    if t_out != tt:
        qkv_ref[...] = jnp.zeros(qkv_ref.shape, F32)
        gb_ref[...] = jnp.zeros(gb_ref.shape, F32)
    for j in range(3 * GDN_W // LANES):
        blk = c[:, j * LANES:(j + 1) * LANES]
        if j < 2 * GDN_W // LANES:
            sq = blk * blk
            hi = sq.astype(BF16)
            lo = (sq - hi.astype(F32)).astype(BF16)
            ss = _dot(hi, head_ones) + _dot(lo, head_ones)
            blk = blk * lax.rsqrt(ss + NORM_EPS)
            if j < GDN_W // LANES:
                blk = blk * QK_SCALE
        qkv_ref[0:tt, j * LANES:(j + 1) * LANES] = jnp.where(live, blk, 0.0)

    ab = ab_ref[...]
    za = ab + dtb_ref[...]
    softplus = jnp.maximum(za, 0.0) + jnp.log(1.0 + jnp.exp(-jnp.abs(za)))
    gdec = -jnp.exp(alog_ref[...]) * softplus
    lane = lax.broadcasted_iota(jnp.int32, (tt, LANES), 1)
    gb = jnp.where(lane < N_GDN_HEADS, gdec, jnp.where(lane < 2 * N_GDN_HEADS, _sigmoid(ab), 0.0))
    gb_ref[0:tt, :] = jnp.where(live, gb, 0.0)


def _gdn_prep(x3, cb8, ab3, cw, alog_row, dtb_row, tt, t_valid, t_out, bb=1):
    b, t_in, _ = x3.shape
    n = t_in // tt
    blocks8 = tt // SUBLANES
    w3 = 3 * GDN_W
    kern = functools.partial(_gdn_prep_kernel, tt=tt, t_valid=t_valid, t_out=t_out)
    return pl.pallas_call(
        kern,
        grid=(b // bb, n),
        in_specs=[pl.BlockSpec((bb, tt, w3), lambda bi, i: (bi, i, 0)),
                  pl.BlockSpec((bb, SUBLANES, w3), lambda bi, i: (bi, jnp.maximum(i * blocks8 - 1, 0), 0)),
                  pl.BlockSpec((bb, SUBLANES, w3), lambda bi, i: (bi, 0, 0)),
                  pl.BlockSpec((bb, tt, LANES), lambda bi, i: (bi, i, 0)),
                  pl.BlockSpec((CONV_W, w3), lambda bi, i: (0, 0)),
                  pl.BlockSpec((1, LANES), lambda bi, i: (0, 0)),
                  pl.BlockSpec((1, LANES), lambda bi, i: (0, 0))],
        out_specs=(pl.BlockSpec((bb, t_out, w3), lambda bi, i: (bi, i, 0)),
                   pl.BlockSpec((bb, t_out, LANES), lambda bi, i: (bi, i, 0))),
        out_shape=(jax.ShapeDtypeStruct((b, n * t_out, w3), F32),
                   jax.ShapeDtypeStruct((b, n * t_out, LANES), F32)),
        scratch_shapes=[pltpu.VMEM((tt + SUBLANES, w3), F32)],
        compiler_params=_cparams(("parallel", "arbitrary")),
        name="gdn_prep",
    )(x3, x3, cb8, ab3, cw, alog_row, dtb_row)


def _gdn_scan_kernel(qkv_ref, gb_ref, z_ref, s0_ref, gn_ref, o_ref, s_ref, *, c, g):
    ci = pl.program_id(1)
    gc = g * c
    gs = g * HEAD_DIM
    n_groups = N_GDN_HEADS // g
    c_shift = int(math.log2(c))

    @pl.when(ci == 0)
    def _():
        s_ref[...] = s0_ref[...]

    def iota(shape, dim):
        return lax.broadcasted_iota(jnp.int32, shape, dim)

    r_i, c_i = iota((gc, gc), 0), iota((gc, gc), 1)
    same = (r_i >> c_shift) == (c_i >> c_shift)
    incl = same & (r_i >= c_i)
    strict = same & (r_i > c_i)
    eye = jnp.where(r_i == c_i, 1.0, 0.0)
    base_shift = min(INV_BASE_SHIFT, c_shift)
    base_blk = (r_i >> base_shift) == (c_i >> base_shift)
    tri = iota((c, gc), 0) <= (iota((c, gc), 1) & (c - 1))
    same_s = (iota((gc, gs), 0) >> c_shift) == (iota((gc, gs), 1) >> HEAD_SHIFT)
    same_s2 = jnp.concatenate([same_s, same_s], axis=0)
    gnorm = gn_ref[...]

    def stack(pieces):
        return jnp.concatenate(pieces, axis=0)

    groups = []
    for sq, gi in [(sq, gi) for sq in range(qkv_ref.shape[0]) for gi in range(n_groups)]:
        gb = gb_ref[sq]
        gcum = jnp.dot(jnp.where(iota((c, c), 0) >= iota((c, c), 1), 1.0, 0.0), gb,
                       preferred_element_type=F32, precision=HIGHEST)
        heads = range(gi * g, (gi + 1) * g)
        lane = lambda base, h: slice(base + h * HEAD_DIM, base + (h + 1) * HEAD_DIM)
        q_st = stack([qkv_ref[sq, :, lane(0, h)] for h in heads])
        k_st = stack([qkv_ref[sq, :, lane(GDN_W, h)] for h in heads])
        v_st = stack([qkv_ref[sq, :, lane(2 * GDN_W, h)] for h in heads])
        gc_st = stack([gcum[:, h:h + 1] for h in heads])
        bt_st = stack([gb[:, N_GDN_HEADS + h:N_GDN_HEADS + h + 1] for h in heads])
        gl_st = stack([jnp.broadcast_to(gcum[c - 1:c, h:h + 1], (c, 1)) for h in heads])
        gl_s = stack([jnp.broadcast_to(gcum[c - 1:c, h:h + 1], (HEAD_DIM, 1)) for h in heads])
        expand = jnp.where(iota((LANES, gc), 0) == (iota((LANES, gc), 1) >> c_shift) + gi * g, 1.0, 0.0)
        spread = jnp.dot(gb, expand, preferred_element_type=F32, precision=HIGHEST)
        gr_st = jnp.sum(jnp.where(tri, spread, 0.0), axis=0, keepdims=True)
        decay = jnp.exp(jnp.where(incl, gc_st - gr_st, -jnp.inf))
        k16 = k_st.astype(BF16)
        qk_kk = _dot_nt(jnp.concatenate([q_st, k_st], axis=0).astype(BF16), k16)
        a_mat = jnp.where(strict, bt_st * qk_kk[gc:2 * gc] * decay, 0.0)
        a_base = jnp.where(base_blk, a_mat, 0.0)
        groups.append(dict(q=q_st, k=k_st, v=v_st, gc=gc_st, bt=bt_st, gl=gl_st, gl_s=gl_s,
                           qk=qk_kk[0:gc] * decay, a=a_mat, t=eye - a_base, p=a_base, heads=heads, sq=sq, gi=gi))

    for _ in range(base_shift - 1):
        for gr in groups:
            p16 = gr['p'].astype(BF16)
            gr['p'] = _dot(p16, p16)
        for gr in groups:
            gr['t'] = gr['t'] + _dot(gr['t'].astype(BF16), gr['p'].astype(BF16))
    for lvl in range(base_shift, c_shift):
        off = ((r_i >> (lvl + 1)) == (c_i >> (lvl + 1))) & ((r_i >> lvl) != (c_i >> lvl))
        for gr in groups:
            t16 = gr['t'].astype(BF16)
            gr['at'] = _dot(jnp.where(off, gr['a'], 0.0).astype(BF16), t16)
        for gr in groups:
            gr['t'] = gr['t'] - _dot(gr['t'].astype(BF16), gr['at'].astype(BF16))

    for gr in groups:
        eg = jnp.exp(gr['gc'])
        rhs = jnp.concatenate([gr['k'] * (gr['bt'] * eg), gr['v'] * gr['bt']], axis=1).astype(BF16)
        gr['wu'] = _dot(gr['t'].astype(BF16), rhs)
        gr['qe'] = gr['q'] * eg
    for gr in groups:
        sq = gr['sq']
        s_rows = slice(gr['gi'] * gs, (gr['gi'] + 1) * gs)
        s = s_ref[sq, s_rows, :]
        wq = jnp.concatenate([gr['wu'][:, 0:HEAD_DIM], gr['qe']], axis=0)
        wq_bd = jnp.where(same_s2, jnp.concatenate([wq] * g, axis=1), 0.0).astype(BF16)
        ws_qs = _dot(wq_bd, s.astype(BF16))
        u = gr['wu'][:, HEAD_DIM:2 * HEAD_DIM] - ws_qs[0:gc]
        u16 = u.astype(BF16)
        o = ws_qs[gc:2 * gc] + _dot(gr['qk'].astype(BF16), u16)
        kd = gr['k'] * jnp.exp(gr['gl'] - gr['gc'])
        kd_bd = jnp.where(same_s, jnp.concatenate([kd] * g, axis=1), 0.0).astype(BF16)
        s_ref[sq, s_rows, :] = s * jnp.exp(gr['gl_s']) + _dot_tn(kd_bd, u16)
        on = o * lax.rsqrt(jnp.mean(o * o, axis=-1, keepdims=True) + NORM_EPS) * gnorm
        for j, h in enumerate(gr['heads']):
            sl = slice(h * HEAD_DIM, (h + 1) * HEAD_DIM)
            zh = z_ref[sq, :, sl]
            o_ref[sq, :, sl] = on[j * c:(j + 1) * c] * (zh * _sigmoid(zh))


def _gdn_scan(qkv3, gb3, z3, s0, gnorm, c, g, bb=1):
    b, tp, _ = qkv3.shape
    n = tp // c
    kern = functools.partial(_gdn_scan_kernel, c=c, g=g)
    state_rows = N_GDN_HEADS * HEAD_DIM
    state_spec = pl.BlockSpec((bb, state_rows, HEAD_DIM), lambda bi, i: (bi, 0, 0))
    o, s_new = pl.pallas_call(
        kern,
        grid=(b // bb, n),
        in_specs=[pl.BlockSpec((bb, c, 3 * GDN_W), lambda bi, i: (bi, i, 0)),
                  pl.BlockSpec((bb, c, LANES), lambda bi, i: (bi, i, 0)),
                  pl.BlockSpec((bb, c, GDN_W), lambda bi, i: (bi, i, 0)),
                  state_spec,
                  pl.BlockSpec((1, HEAD_DIM), lambda bi, i: (0, 0))],
        out_specs=(pl.BlockSpec((bb, c, GDN_W), lambda bi, i: (bi, i, 0)), state_spec),
        out_shape=(jax.ShapeDtypeStruct((b, tp, GDN_W), F32),
                   jax.ShapeDtypeStruct((b, state_rows, HEAD_DIM), F32)),
        compiler_params=_cparams(("parallel", "arbitrary")),
        name="gdn_scan",
    )(qkv3, gb3, z3, s0.reshape(b, state_rows, HEAD_DIM), gnorm)
    return o, s_new.reshape(b, N_GDN_HEADS, HEAD_DIM, HEAD_DIM)


def _compress(readers, pe_ref, wbd_ref, cc, cs):
    acc = None
    for c in range(CMP_BLOCK):
        rows = jnp.concatenate([rd(parity, c) for rd in readers for parity in range(2)], axis=0)
        part = _dot((rows + pe_ref[c:c + 1, :]).astype(BF16), wbd_ref[c])
        acc = part if acc is None else acc + part
    return _rope128(acc, cc, cs)


def _softmax_rows(s, mask):
    s = jnp.where(mask, s, -jnp.inf)
    m = jnp.max(s, axis=-1, keepdims=True)
    m = jnp.where(m > -jnp.inf, m, 0.0)
    e = jnp.exp(s - m)
    return e / jnp.maximum(jnp.sum(e, axis=-1, keepdims=True), 1e-30)


def _select_blocks(imp, qpos, n_sel, blk_axis):
    blk = lax.broadcasted_iota(jnp.int32, imp.shape, blk_axis)
    cur = qpos >> HEAD_SHIFT
    forced = (blk == 0) | (blk == cur) | (blk == cur - 1)
    valid = (blk * SEL_BLOCK <= qpos) & (blk < n_sel)
    score = jnp.where(forced, jnp.inf, jnp.where(valid, imp, -jnp.inf))
    rank = jnp.zeros(imp.shape, F32)
    for i in range(min(n_sel, N_SCORE)):
        si = score[i:i + 1, :] if blk_axis == 0 else score[:, i:i + 1]
        ahead = (si > score) | ((si == score) & (blk > i))
        rank = rank + jnp.where(ahead, 1.0, 0.0)
    return ((rank < TOP_K) & (blk < n_sel)).astype(BF16)


def _cmp_branch_rows(q_c, ckv16, qpos, qpos_t, n_half, n_sel, tq):
    r = q_c.shape[0]
    nc = 2 * n_half
    col = lax.broadcasted_iota(jnp.int32, (r, nc), 1)
    cmp_end = (2 * (col & (n_half - 1)) + (col >> int(math.log2(n_half))) + 1) * CMP_BLOCK - 1
    p_c = _softmax_rows(_dot_nt(q_c, ckv16), cmp_end <= qpos)
    o_c = _dot(p_c.astype(BF16), ckv16)[:, HEAD_DIM:2 * HEAD_DIM]
    pair = p_c[:, 0:n_half] + p_c[:, n_half:nc]
    imp = pair[0:tq]
    for g in range(1, GQA_GROUP):
        imp = imp + pair[g * tq:(g + 1) * tq]
    if n_half < N_SCORE:
        imp = jnp.concatenate([imp, jnp.zeros((tq, N_SCORE - n_half), F32)], axis=1)
    sel = _select_blocks(imp, qpos_t, n_sel, 1)
    sel_r = jnp.concatenate([sel] * GQA_GROUP, axis=0)
    return o_c, lambda expand: _dot(sel_r, expand)


def _softmax_cols(s, mask):
    s = jnp.where(mask, s, -jnp.inf)
    m = jnp.max(s, axis=0, keepdims=True)
    m = jnp.where(m > -jnp.inf, m, 0.0)
    e = jnp.exp(s - m)
    return e / jnp.maximum(jnp.sum(e, axis=0, keepdims=True), 1e-30)


def _nsa_core_lanes(q_blk, t0, ckv, n_half, n_sel, ks_rows, ks_t, n_full, kc, win_rows, win_t, win_pos0, kvh,
                    gates, sel_ref):
    tq = q_blk.shape[0]
    r = GQA_GROUP * tq
    nc = 2 * n_half
    q_all = jnp.concatenate([q_blk[:, g * HEAD_DIM:(g + 1) * HEAD_DIM] for g in range(GQA_GROUP)], axis=0)
    zeros64 = jnp.zeros((r, HEAD_DIM), F32)
    q_c = jnp.concatenate([q_all, zeros64], axis=1)
    q_c16 = q_c.astype(BF16)
    q_ct = q_c.T.astype(BF16)
    ckv16 = ckv.astype(BF16)
    qpos_l = (lax.broadcasted_iota(jnp.int32, (1, r), 1) & (tq - 1)) + t0

    row = lax.broadcasted_iota(jnp.int32, (nc, r), 0)
    cmp_end = (2 * (row & (n_half - 1)) + (row >> int(math.log2(n_half))) + 1) * CMP_BLOCK - 1
    p_c = _softmax_cols(_dot_nt(ckv16, q_c16), cmp_end <= qpos_l)
    o_ct = _dot_tn(ckv16, p_c.astype(BF16))
    pair = p_c[0:n_half] + p_c[n_half:nc]
    imp = pair[:, 0:tq]
    for g in range(1, GQA_GROUP):
        imp = imp + pair[:, g * tq:(g + 1) * tq]
    if n_half < N_SCORE:
        imp = jnp.concatenate([imp, jnp.zeros((N_SCORE - n_half, tq), F32)], axis=0)
    sel = _select_blocks(imp, qpos_l[:, 0:tq], n_sel, 0).astype(F32)
    sel_ref[...] = jnp.concatenate([sel] * GQA_GROUP, axis=1)

    blocks_per_chunk = kc // SEL_BLOCK
    krow = lax.broadcasted_iota(jnp.int32, (SEL_BLOCK, r), 0)

    def body(c, carry, causal):
        m, l, acc = carry
        s = _dot(ks_rows(c), q_ct)
        sel_c = sel_ref[pl.ds(pl.multiple_of(c * blocks_per_chunk, blocks_per_chunk), blocks_per_chunk), :]
        parts = []
        for j in range(blocks_per_chunk):
            mask = jnp.broadcast_to(sel_c[j:j + 1, :], (SEL_BLOCK, r)) > 0.5
            if causal:
                mask = mask & (krow + (c * kc + j * SEL_BLOCK) <= qpos_l)
            parts.append(jnp.where(mask, s[j * SEL_BLOCK:(j + 1) * SEL_BLOCK], -jnp.inf))
        s = jnp.concatenate(parts, axis=0)
        m_new = jnp.maximum(m, jnp.max(s, axis=0, keepdims=True))
        m_safe = jnp.where(m_new > -jnp.inf, m_new, 0.0)
        alpha = jnp.exp(m - m_safe)
        p = jnp.exp(s - m_safe)
        l = alpha * l + jnp.sum(p, axis=0, keepdims=True)
        acc = alpha * acc + _dot(ks_t(c), p.astype(BF16))
        return m_new, l, acc

    carry = (jnp.full((1, r), -jnp.inf, F32), jnp.zeros((1, r), F32), jnp.zeros((LANES, r), F32))
    carry = lax.fori_loop(0, n_full, lambda c, cr: body(c, cr, False), carry)
    _, l_s, acc_s = body(n_full, carry, True)
    o_st = acc_s / jnp.maximum(l_s, 1e-30)

    lane = lax.broadcasted_iota(jnp.int32, (r, ROW_W), 1)
    q_w = jnp.where((lane >> HEAD_SHIFT) == kvh, jnp.concatenate([q_all, q_all, zeros64, zeros64], axis=1), 0.0)
    nw = win_rows.shape[0]
    kpos = lax.broadcasted_iota(jnp.int32, (nw, r), 0) + win_pos0
    p_w = _softmax_cols(_dot(win_rows, q_w.T.astype(BF16)), (kpos <= qpos_l) & (kpos >= qpos_l - WINDOW))
    o_wt2 = _dot(win_t, p_w.astype(BF16))
    o_wt = jnp.where(kvh == 0, o_wt2[2 * HEAD_DIM:3 * HEAD_DIM], o_wt2[3 * HEAD_DIM:4 * HEAD_DIM])

    gates_t = gates.T
    outs = []
    for g in range(GQA_GROUP):
        cols = slice(g * tq, (g + 1) * tq)
        g0 = _sigmoid(gates_t[3 * g:3 * g + 1, :])
        g1 = _sigmoid(gates_t[3 * g + 1:3 * g + 2, :])
        g2 = _sigmoid(gates_t[3 * g + 2:3 * g + 3, :])
        outs.append(g0 * o_ct[HEAD_DIM:2 * HEAD_DIM, cols] + g1 * o_st[HEAD_DIM:2 * HEAD_DIM, cols]
                    + g2 * o_wt[:, cols])
    return jnp.concatenate(outs, axis=0).T


def _nsa_core_rows(q_blk, t0, ckv, n_half, sel_chunk, n_full, kc, n_sel, win_rows, win_pos0, kvh, gates):
    tq = q_blk.shape[0]
    r = GQA_GROUP * tq
    q_all = jnp.concatenate([q_blk[:, g * HEAD_DIM:(g + 1) * HEAD_DIM] for g in range(GQA_GROUP)], axis=0)
    qpos_t = lax.broadcasted_iota(jnp.int32, (tq, 1), 0) + t0
    qpos = jnp.concatenate([qpos_t] * GQA_GROUP, axis=0)
    zeros64 = jnp.zeros((r, HEAD_DIM), F32)
    q_c = jnp.concatenate([q_all, zeros64], axis=1).astype(BF16)
    ckv16 = ckv.astype(BF16)
    o_c, picked_fn = _cmp_branch_rows(q_c, ckv16, qpos, qpos_t, n_half, n_sel, tq)

    j_i = lax.broadcasted_iota(jnp.int32, (N_SCORE, kc), 0)
    k_i = lax.broadcasted_iota(jnp.int32, (N_SCORE, kc), 1) >> HEAD_SHIFT
    kcol = lax.broadcasted_iota(jnp.int32, (r, kc), 1)

    def body(c, carry, causal):
        m, l, acc = carry
        rows = sel_chunk(c)
        s = _dot_nt(q_c, rows)
        expand = jnp.where(j_i == k_i + c * (kc // SEL_BLOCK), 1.0, 0.0).astype(BF16)
        mask = picked_fn(expand) > 0.5
        if causal:
            mask = mask & (kcol + c * kc <= qpos)
        s = jnp.where(mask, s, -jnp.inf)
        m_new = jnp.maximum(m, jnp.max(s, axis=-1, keepdims=True))
        m_safe = jnp.where(m_new > -jnp.inf, m_new, 0.0)
        alpha = jnp.exp(m - m_safe)
        p = jnp.exp(s - m_safe)
        l = alpha * l + jnp.sum(p, axis=-1, keepdims=True)
        acc = alpha * acc + _dot(p.astype(BF16), rows)
        return m_new, l, acc

    carry = (jnp.full((r, 1), -jnp.inf, F32), jnp.zeros((r, 1), F32), jnp.zeros((r, LANES), F32))
    if not isinstance(n_full, int) or n_full > 0:
        carry = lax.fori_loop(0, n_full, lambda c, cr: body(c, cr, False), carry)
    _, l_s, acc_s = body(n_full, carry, True)
    o_s = acc_s[:, HEAD_DIM:2 * HEAD_DIM] / jnp.maximum(l_s, 1e-30)

    lane = lax.broadcasted_iota(jnp.int32, (r, ROW_W), 1)
    q_w = jnp.where((lane >> HEAD_SHIFT) == kvh, jnp.concatenate([q_all, q_all, zeros64, zeros64], axis=1), 0.0)
    q_w = q_w.astype(BF16)
    nw = win_rows.shape[0]
    kpos = lax.broadcasted_iota(jnp.int32, (r, nw), 1) + win_pos0
    p_w = _softmax_rows(_dot_nt(q_w, win_rows), (kpos <= qpos) & (kpos >= qpos - WINDOW))
    o_w2 = _dot(p_w.astype(BF16), win_rows)
    o_w = jnp.where(kvh == 0, o_w2[:, 2 * HEAD_DIM:3 * HEAD_DIM], o_w2[:, 3 * HEAD_DIM:4 * HEAD_DIM])

    outs = []
    for g in range(GQA_GROUP):
        rows_g = slice(g * tq, (g + 1) * tq)
        g0 = _sigmoid(gates[:, 3 * g:3 * g + 1])
        g1 = _sigmoid(gates[:, 3 * g + 1:3 * g + 2])
        g2 = _sigmoid(gates[:, 3 * g + 2:3 * g + 3])
        outs.append(g0 * o_c[rows_g] + g1 * o_s[rows_g] + g2 * o_w[rows_g])
    return outs


def _nsa_prompt_kernel(q_ref, kc_ref, ks_ref, kst_ref, win_ref, wint_ref, gate_ref, wbd_ref, pe_ref, cc_ref, cs_ref,
                       o_ref, ckv_ref, sel_ref, *, tq, t, kc):
    kvh = pl.program_id(1)
    i = pl.program_id(2)
    n_half = t // (2 * CMP_BLOCK)

    @pl.when(i == 0)
    def _():
        read = lambda parity, c: kc_ref[0, pl.ds(parity * CMP_BLOCK + c, n_half, stride=2 * CMP_BLOCK), :]
        ckv_ref[...] = _compress([read], pe_ref, wbd_ref, cc_ref[...], cs_ref[...])

    t0 = i * tq
    ks_rows = lambda c: ks_ref[0, pl.ds(pl.multiple_of(c * kc, kc), kc), :].astype(BF16)
    ks_t = lambda c: kst_ref[0, c].astype(BF16)
    span = WINDOW + tq
    start = pl.multiple_of(jnp.maximum(t0 - WINDOW, 0), tq)
    win_rows = win_ref[pl.ds(start, span), :].astype(BF16)
    blk0 = start // LANES
    win_t = jnp.concatenate([wint_ref[blk0 + j] for j in range(span // LANES)], axis=1).astype(BF16)
    o_ref[...] = _nsa_core_lanes(q_ref[...], t0, ckv_ref[...], n_half, t // SEL_BLOCK, ks_rows, ks_t, t0 // kc, kc,
                                 win_rows, win_t, start, kvh, gate_ref[...], sel_ref)


def _nsa_prompt(q, kvc, kvs, kst, win, wint, gate, wbd, pe, cc, cs, b, t, tq=128):
    n = t // tq
    nc = t // CMP_BLOCK
    kc = kst.shape[3]
    assert tq % LANES == 0 and kc % tq == 0 and t % kc == 0
    kern = functools.partial(_nsa_prompt_kernel, tq=tq, t=t, kc=kc)
    fix2 = lambda bi, h, i: (0, 0)
    return pl.pallas_call(
        kern,
        grid=(b, N_KV_HEADS, n),
        in_specs=[pl.BlockSpec((tq, ROW_W), lambda bi, h, i: (bi * n + i, h)),
                  pl.BlockSpec((1, t, LANES), lambda bi, h, i: (h, bi, 0)),
                  pl.BlockSpec((1, t, LANES), lambda bi, h, i: (h, bi, 0)),
                  pl.BlockSpec((1, t // kc, LANES, kc), lambda bi, h, i: (h, bi, 0, 0)),
                  pl.BlockSpec((t, ROW_W), lambda bi, h, i: (bi, 0)),
                  pl.BlockSpec((t // LANES, ROW_W, LANES), lambda bi, h, i: (bi, 0, 0)),
                  pl.BlockSpec((tq, LANES), lambda bi, h, i: (bi * n + i, h)),
                  pl.BlockSpec((CMP_BLOCK, LANES, LANES), lambda bi, h, i: (0, 0, 0)),
                  pl.BlockSpec((CMP_BLOCK, LANES), fix2),
                  pl.BlockSpec((nc, LANES), fix2),
                  pl.BlockSpec((nc, LANES), fix2)],
        out_specs=pl.BlockSpec((tq, ROW_W), lambda bi, h, i: (bi * n + i, h)),
        out_shape=jax.ShapeDtypeStruct((b * t, NSA_W), F32),
        scratch_shapes=[pltpu.VMEM((nc, LANES), F32), pltpu.VMEM((N_SCORE, GQA_GROUP * tq), F32)],
        compiler_params=_cparams(("parallel", "parallel", "arbitrary")),
        name="nsa_prompt",
    )(q, kvc, kvs, kst, win, wint, gate, wbd, pe, cc, cs)


def _nsa_sample_kernel(pt_ref, *refs, n_steps, pages_per_step, past, tq, ts, n_pad, nw):
    page_refs = refs[:pages_per_step]
    (perm_ref, q_ref, kvnew_ref, wbuf_ref, wnew_ref, gate_ref, wbd_ref, pe_ref, cc_ref, cs_ref,
     o_ref, wout_ref, pge_ref, pgo_ref, pgs_ref, wn_ref) = refs[pages_per_step:]
    j = pl.program_id(1)
    half = perm_ref.shape[0] // 2
    per_parity = half // CMP_BLOCK
    slot = pl.ds(pl.multiple_of(j * per_parity, per_parity), per_parity)
    for kvh in range(N_KV_HEADS):
        for k, page_ref in enumerate(page_refs):
            rows = pl.ds(pl.multiple_of((j * pages_per_step + k) * PAGE_SIZE, PAGE_SIZE), PAGE_SIZE)
            pgs_ref[kvh, rows, :] = page_ref[0, 0, kvh, LANES:ROW_W, :].T
        cmp_t = jnp.concatenate([page_ref[0, 0, kvh, 0:LANES, :] for page_ref in page_refs], axis=1)
        grouped = _dot_nt(perm_ref[...], cmp_t.astype(BF16))
        for c in range(CMP_BLOCK):
            pge_ref[kvh, c, slot, :] = grouped[c * per_parity:(c + 1) * per_parity]
            pgo_ref[kvh, c, slot, :] = grouped[half + c * per_parity:half + (c + 1) * per_parity]

    @pl.when(j == n_steps - 1)
    def _():
        n_half = past // (2 * CMP_BLOCK)
        wbuf_t = wbuf_ref[0, 0]
        wnew = wnew_ref[0]
        wn_ref[0:WINDOW, :] = wbuf_t.T
        wn_ref[WINDOW:WINDOW + tq, :] = wnew
        wn_ref[WINDOW + tq:nw, :] = jnp.zeros((nw - WINDOW - tq, ROW_W), F32)
        win_rows = wn_ref[...].astype(BF16)

        shifted = pltpu.roll(wbuf_t, WINDOW - ts, 1)
        new_t = jnp.concatenate([wnew, jnp.zeros((LANES - tq, ROW_W), F32)], axis=0).T
        lane = lax.broadcasted_iota(jnp.int32, (ROW_W, LANES), 1)
        tail = jnp.where(lane >= LANES - ts, pltpu.roll(new_t, LANES - ts, 1), shifted[:, WINDOW - LANES:WINDOW])
        wout_ref[0, :, 0:WINDOW - LANES] = shifted[:, 0:WINDOW - LANES]
        wout_ref[0, :, WINDOW - LANES:WINDOW] = tail

        readers = []
        for kvh in range(N_KV_HEADS):
            pgs_ref[kvh, past:past + tq, :] = kvnew_ref[kvh, 0]
            pgs_ref[kvh, past + tq:n_pad, :] = jnp.zeros((n_pad - past - tq, LANES), F32)
            readers.append(lambda parity, c, kvh=kvh: (pgo_ref if parity else pge_ref)[kvh, c])
        ckv2 = _compress(readers, pe_ref, wbd_ref, cc_ref[...], cs_ref[...])
        for kvh in range(N_KV_HEADS):
            ckv = ckv2[kvh * 2 * n_half:(kvh + 1) * 2 * n_half]
            sel_chunk = lambda c, kvh=kvh: pgs_ref[kvh].astype(BF16)
            outs = _nsa_core_rows(q_ref[0, :, kvh * ROW_W:(kvh + 1) * ROW_W], past, ckv, n_half, sel_chunk, 0,
                             n_pad, past // SEL_BLOCK + 1, win_rows, past - WINDOW, kvh,
                             gate_ref[0, :, kvh * LANES:(kvh + 1) * LANES])
            for g in range(GQA_GROUP):
                col = kvh * ROW_W + g * HEAD_DIM
                o_ref[0, :, col:col + HEAD_DIM] = outs[g]


def _page_group_permutation(pages_per_step):
    n = pages_per_step * PAGE_SIZE
    half = n // 2
    per_parity = half // CMP_BLOCK
    r = np.arange(n)
    parity, c, i = r // half, (r % half) // per_parity, r % per_parity
    perm = np.zeros((n, n), np.float32)
    perm[r, (2 * i + parity) * CMP_BLOCK + c] = 1.0
    return jnp.asarray(perm, dtype=BF16)


def _nsa_sample(page_table, pool_t, layer, ts, q8, kvnew8, win_t, wnew8, gate8, wbd, pe, cc, cs, pages_per_step=4):
    b, n_pages = page_table.shape
    past = n_pages * PAGE_SIZE
    tq = q8.shape[1]
    n_pad = past + SEL_BLOCK
    nw = WINDOW + LANES
    nc2 = N_KV_HEADS * (past // CMP_BLOCK)
    n_steps = n_pages // pages_per_step
    n_half = past // (2 * CMP_BLOCK)
    perm = _page_group_permutation(pages_per_step)
    assert (pages_per_step * PAGE_SIZE // (2 * CMP_BLOCK)) % SUBLANES == 0
    kern = functools.partial(_nsa_sample_kernel, n_steps=n_steps, pages_per_step=pages_per_step, past=past,
                             tq=tq, ts=ts, n_pad=n_pad, nw=nw)
    per_b = lambda bi, j, pt: (bi, 0, 0)
    fix2 = lambda bi, j, pt: (0, 0)
    page_spec = lambda k: pl.BlockSpec((1, 1, N_KV_HEADS, ROW_W, PAGE_SIZE),
                                       lambda bi, j, pt: (layer, pt[bi, j * pages_per_step + k], 0, 0, 0))
    grid_spec = pltpu.PrefetchScalarGridSpec(
        num_scalar_prefetch=1,
        grid=(b, n_steps),
        in_specs=[page_spec(k) for k in range(pages_per_step)] + [
                  pl.BlockSpec(perm.shape, fix2),
                  pl.BlockSpec((1, tq, NSA_W), per_b),
                  pl.BlockSpec((N_KV_HEADS, 1, tq, LANES), lambda bi, j, pt: (0, bi, 0, 0)),
                  pl.BlockSpec((1, 1, ROW_W, WINDOW), lambda bi, j, pt: (layer, bi, 0, 0)),
                  pl.BlockSpec((1, tq, ROW_W), per_b),
                  pl.BlockSpec((1, tq, N_KV_HEADS * LANES), per_b),
                  pl.BlockSpec((CMP_BLOCK, LANES, LANES), lambda bi, j, pt: (0, 0, 0)),
                  pl.BlockSpec((CMP_BLOCK, LANES), fix2),
                  pl.BlockSpec((nc2, LANES), fix2),
                  pl.BlockSpec((nc2, LANES), fix2)],
        out_specs=(pl.BlockSpec((1, tq, NSA_W), per_b), pl.BlockSpec((1, ROW_W, WINDOW), per_b)),
        scratch_shapes=[pltpu.VMEM((N_KV_HEADS, CMP_BLOCK, n_half, LANES), F32),
                        pltpu.VMEM((N_KV_HEADS, CMP_BLOCK, n_half, LANES), F32),
                        pltpu.VMEM((N_KV_HEADS, n_pad, LANES), F32), pltpu.VMEM((nw, ROW_W), F32)],
    )
    return pl.pallas_call(
        kern,
        grid_spec=grid_spec,
        out_shape=(jax.ShapeDtypeStruct((b, tq, NSA_W), F32), jax.ShapeDtypeStruct((b, ROW_W, WINDOW), F32)),
        compiler_params=_cparams(("parallel", "arbitrary")),
        name="nsa_sample",
    )(page_table, *([pool_t] * pages_per_step), perm, q8, kvnew8, win_t, wnew8, gate8, wbd, pe, cc, cs)


def _mlp_kernel(x_ref, og_ref, on_ref, wo_ref, gm_ref, wu_ref, wd_ref, gf_ref, y_ref, x1_ref, h_ref, acc_ref,
                *, final_norm):
    f = pl.program_id(1)

    @pl.when(f == 0)
    def _():
        x1 = (x_ref[...] + _dot(og_ref[...].astype(BF16), wo_ref[0:GDN_W, :])
              + _dot(on_ref[...].astype(BF16), wo_ref[GDN_W:GDN_W + NSA_W, :]))
        x1_ref[...] = x1
        var = jnp.mean(x1 * x1, axis=-1, keepdims=True)
        h_ref[...] = (x1 * lax.rsqrt(var + NORM_EPS) * gm_ref[...]).astype(BF16)
        acc_ref[...] = jnp.zeros(acc_ref.shape, F32)

    up = jnp.maximum(_dot(h_ref[...], wu_ref[...]), 0.0)
    acc_ref[...] += _dot((up * up).astype(BF16), wd_ref[...])

    @pl.when(f == pl.num_programs(1) - 1)
    def _():
        y = x1_ref[...] + acc_ref[...]
        if final_norm:
            var = jnp.mean(y * y, axis=-1, keepdims=True)
            y = y * lax.rsqrt(var + NORM_EPS) * gf_ref[...]
        y_ref[...] = y


def _mlp(x, og, on, wo, gm, wu, wd, gf, final_norm, tm, tf=1024):
    m = x.shape[0]
    kern = functools.partial(_mlp_kernel, final_norm=final_norm)
    row = lambda i, f: (i, 0)
    fix = lambda i, f: (0, 0)
    return pl.pallas_call(
        kern,
        grid=(m // tm, D_FF // tf),
        in_specs=[pl.BlockSpec((tm, D_MODEL), row), pl.BlockSpec((tm, GDN_W), row), pl.BlockSpec((tm, NSA_W), row),
                  pl.BlockSpec((D_MODEL, D_MODEL), fix), pl.BlockSpec((1, D_MODEL), fix),
                  pl.BlockSpec((D_MODEL, tf), lambda i, f: (0, f)), pl.BlockSpec((tf, D_MODEL), lambda i, f: (f, 0)),
                  pl.BlockSpec((1, D_MODEL), fix)],
        out_specs=pl.BlockSpec((tm, D_MODEL), row),
        out_shape=jax.ShapeDtypeStruct((m, D_MODEL), F32),
        scratch_shapes=[pltpu.VMEM((tm, D_MODEL), F32), pltpu.VMEM((tm, D_MODEL), BF16),
                        pltpu.VMEM((tm, D_MODEL), F32)],
        compiler_params=_cparams(("parallel", "arbitrary")),
        name="mlp",
    )(x, og, on, wo, gm, wu, wd, gf)


def _proj_columns():
    src = np.full((N_PROJ,), -1, np.int64)
    o_z, o_a, o_b = 3 * GDN_W, 4 * GDN_W, 4 * GDN_W + N_GDN_HEADS
    o_q = o_b + N_GDN_HEADS
    o_kv = o_q + NSA_W
    o_g = o_kv + 6 * KV_W
    src[C_QKV:C_Z] = np.arange(0, 3 * GDN_W)
    src[C_Z:C_Q] = np.arange(o_z, o_z + GDN_W)
    src[C_Q:C_KV] = np.arange(o_q, o_q + NSA_W)
    for kvh in range(N_KV_HEADS):
        for s in range(4):
            dst = C_KV + kvh * ROW_W + s * HEAD_DIM
            src[dst:dst + HEAD_DIM] = o_kv + s * KV_W + kvh * HEAD_DIM + np.arange(HEAD_DIM)
    src[C_WIN:C_WIN + 2 * KV_W] = o_kv + 4 * KV_W + np.arange(2 * KV_W)
    src[C_AB:C_AB + N_GDN_HEADS] = o_a + np.arange(N_GDN_HEADS)
    src[C_AB + N_GDN_HEADS:C_AB + 2 * N_GDN_HEADS] = o_b + np.arange(N_GDN_HEADS)
    for kvh in range(N_KV_HEADS):
        dst = C_GATE + kvh * LANES
        src[dst:dst + 3 * GQA_GROUP] = o_g + kvh * 3 * GQA_GROUP + np.arange(3 * GQA_GROUP)
    return src


def _rope_tables(pos):
    half = HEAD_DIM // 2
    inv_freq = ROPE_THETA ** (-jnp.arange(half, dtype=F32) / half)
    ang = pos.astype(F32)[:, None] * inv_freq[None, :]
    cos, sin = jnp.cos(ang), jnp.sin(ang)
    c64 = jnp.concatenate([cos, cos], axis=1)
    s64 = jnp.concatenate([-sin, sin], axis=1)
    one, zero = jnp.ones_like(c64), jnp.zeros_like(s64)
    return (jnp.concatenate([c64, c64], axis=1), jnp.concatenate([s64, s64], axis=1),
            jnp.concatenate([c64, one], axis=1), jnp.concatenate([s64, zero], axis=1))


def _cmp_tables(n_blocks):
    n_half = n_blocks // 2
    r = np.arange(n_blocks)
    blk = 2 * (r % n_half) + r // n_half
    end_pos = jnp.asarray((blk + 1) * CMP_BLOCK - 1)
    _, _, cc, cs = _rope_tables(end_pos)
    return cc, cs


def _layer_params(l, norm_mix, w_in, conv_w, a_log, dt_bias, gdn_norm, cmp_pe_k, cmp_w_k, cmp_pe_v, cmp_w_v,
                  w_out, norm_mlp, w_up, w_down):
    src = _proj_columns()
    w_ext = jnp.concatenate([w_in[l], jnp.zeros((D_MODEL, 1), F32)], axis=1)
    w_p = jnp.take(w_ext, jnp.asarray(np.where(src < 0, w_in.shape[2], src)), axis=1).astype(BF16)
    zeros = jnp.zeros((CMP_BLOCK, HEAD_DIM, HEAD_DIM), F32)
    wbd = jnp.concatenate([jnp.concatenate([cmp_w_k[l], zeros], axis=2),
                           jnp.concatenate([zeros, cmp_w_v[l]], axis=2)], axis=1).astype(BF16)
    pad_row = lambda v: jnp.pad(v.astype(F32), (0, LANES - v.shape[0]))[None, :]
    return dict(
        norm_mix=norm_mix[l][None, :], w_p=w_p, conv_w=conv_w[l], alog=pad_row(a_log[l]), dtb=pad_row(dt_bias[l]),
        gdn_norm=gdn_norm[l][None, :], wbd=wbd, pe=jnp.concatenate([cmp_pe_k[l], cmp_pe_v[l]], axis=1),
        w_out=w_out[l].astype(BF16), norm_mlp=norm_mlp[l][None, :], w_up=w_up[l].astype(BF16),
        w_down=w_down[l].astype(BF16))


def _prompt_layer(x, lp, tabs, cmp_tabs, gf, final_norm, b, t):
    qkv_pre, z, q, kvc, kvs, win, ab, gate, kst, wint = _proj(x, lp['norm_mix'], lp['w_p'], tabs, PROJ_ROWS)
    w3 = 3 * GDN_W
    cb8 = jnp.zeros((b, SUBLANES, w3), F32)
    qkv3, gb3 = _gdn_prep(qkv_pre.reshape(b, t, w3), cb8, ab.reshape(b, t, LANES), lp['conv_w'], lp['alog'],
                          lp['dtb'], 256, t, 256)
    s0 = jnp.zeros((b, N_GDN_HEADS, HEAD_DIM, HEAD_DIM), F32)
    o_gdn, s_new = _gdn_scan(qkv3, gb3, z.reshape(b, t, GDN_W), s0, lp['gdn_norm'], GDN_CHUNK, 4)
    o_nsa = _nsa_prompt(q, kvc, kvs, kst, win, wint, gate, lp['wbd'], lp['pe'], cmp_tabs[0], cmp_tabs[1], b, t)
    y = _mlp(x, o_gdn.reshape(b * t, GDN_W), o_nsa, lp['w_out'], lp['norm_mlp'], lp['w_up'], lp['w_down'], gf,
             final_norm, 512)
    rows = jnp.concatenate([kvc, kvs], axis=2).reshape(N_KV_HEADS, b, t, 4, HEAD_DIM).transpose(1, 0, 2, 3, 4)
    win_new = win.reshape(b, t, 2, N_KV_HEADS, HEAD_DIM)[:, t - min(WINDOW, t):]
    conv_new = qkv_pre.reshape(b, t, w3)[:, t - (CONV_W - 1):]
    return y, rows, win_new, s_new, conv_new


def _pad_rows(a, n):
    return jnp.pad(a, ((0, 0), (0, n - a.shape[1]), (0, 0)))


def _sample_layer(x, lp, tabs, cmp_tabs, gf, final_norm, b, t, layer, pool_t, page_table, win_t, s0, conv_buf):
    m = b * t
    qkv_pre, z, q, kvc, kvs, win, ab, gate, _, _ = _proj(x, lp['norm_mix'], lp['w_p'], tabs, PROJ_ROWS)
    w3 = 3 * GDN_W
    tp = SUBLANES
    cb8 = jnp.pad(conv_buf, ((0, 0), (SUBLANES - (CONV_W - 1), 0), (0, 0)))
    qkv3, gb3 = _gdn_prep(_pad_rows(qkv_pre.reshape(b, t, w3), tp), cb8, _pad_rows(ab.reshape(b, t, LANES), tp),
                          lp['conv_w'], lp['alog'], lp['dtb'], tp, t, tp, bb=SAMPLE_SEQS_PER_STEP)
    o_gdn, s_new = _gdn_scan(qkv3, gb3, _pad_rows(z.reshape(b, t, GDN_W), tp), s0, lp['gdn_norm'], tp,
                             N_GDN_HEADS, bb=SAMPLE_SEQS_PER_STEP // 2)
    o_gdn = o_gdn[:, :t].reshape(m, GDN_W)
    kvnew8 = jnp.pad(kvs.reshape(N_KV_HEADS, b, t, LANES), ((0, 0), (0, 0), (0, tp - t), (0, 0)))
    o_nsa, win_out_t = _nsa_sample(page_table, pool_t, layer, t, _pad_rows(q.reshape(b, t, NSA_W), tp), kvnew8,
                                   win_t, _pad_rows(win.reshape(b, t, ROW_W), tp),
                                   _pad_rows(gate.reshape(b, t, N_KV_HEADS * LANES), tp), lp['wbd'], lp['pe'],
                                   cmp_tabs[0], cmp_tabs[1])
    o_nsa = o_nsa[:, :t].reshape(m, NSA_W)
    y = _mlp(x, o_gdn, o_nsa, lp['w_out'], lp['norm_mlp'], lp['w_up'], lp['w_down'], gf, final_norm, 512)
    rows = jnp.concatenate([kvc, kvs], axis=2).reshape(N_KV_HEADS, b, t, 4, HEAD_DIM).transpose(1, 0, 2, 3, 4)
    win_new = win_out_t.reshape(b, 2, N_KV_HEADS, HEAD_DIM, WINDOW).transpose(0, 4, 1, 2, 3)
    conv_new = jnp.concatenate([conv_buf, qkv_pre.reshape(b, t, w3)], axis=1)[:, t:]
    return y, rows, win_new, s_new, conv_new


def kernel(x_prompt, x_sample, cache_kv, page_table, state_win, state_gdn, state_conv, norm_mix, w_in, conv_w, a_log, dt_bias, gdn_norm, cmp_pe_k, cmp_w_k, cmp_pe_v, cmp_w_v, w_out, norm_mlp, w_up, w_down, norm_final):
    bp, tp_, _ = x_prompt.shape
    bs, ts, _ = x_sample.shape
    depth = cache_kv.shape[0]
    n_pages = page_table.shape[1]
    past = n_pages * PAGE_SIZE
    assert state_win.shape[2] == WINDOW and tp_ % 512 == 0 and ts <= SUBLANES and past % (2 * CMP_BLOCK) == 0

    tabs_p = _rope_tables(jnp.arange(tp_))
    tabs_s = tuple(jnp.tile(tb, (bs, 1)) for tb in _rope_tables(past + jnp.arange(ts)))
    cmp_p = _cmp_tables(tp_ // CMP_BLOCK)
    cmp_s = tuple(jnp.tile(tb, (N_KV_HEADS, 1)) for tb in _cmp_tables((past + ts) // CMP_BLOCK))
    pool_t = cache_kv.transpose(0, 1, 2, 4, 5, 3).reshape(depth, cache_kv.shape[1], N_KV_HEADS, ROW_W, PAGE_SIZE)
    win_t = state_win.transpose(0, 1, 3, 4, 5, 2).reshape(depth, bs, ROW_W, WINDOW)
    gf = norm_final[None, :]

    xp = x_prompt.reshape(bp * tp_, D_MODEL)
    xs = x_sample.reshape(bs * ts, D_MODEL)
    outs = [[] for _ in range(8)]
    for l in range(depth):
        lp = _layer_params(l, norm_mix, w_in, conv_w, a_log, dt_bias, gdn_norm, cmp_pe_k, cmp_w_k, cmp_pe_v,
                           cmp_w_v, w_out, norm_mlp, w_up, w_down)
        last = l == depth - 1
        xp, r, w, s, c = _prompt_layer(xp, lp, tabs_p, cmp_p, gf, last, bp, tp_)
        for k, v in zip((0, 2, 4, 6), (r, w, s, c)):
            outs[k].append(v)
        xs, r, w, s, c = _sample_layer(xs, lp, tabs_s, cmp_s, gf, last, bs, ts, l, pool_t, page_table,
                                       win_t, state_gdn[l], state_conv[l])
        for k, v in zip((1, 3, 5, 7), (r, w, s, c)):
            outs[k].append(v)
    return (xp.reshape(bp, tp_, D_MODEL), xs.reshape(bs, ts, D_MODEL)) + tuple(jnp.stack(o) for o in outs)
```

```python
import functools
import math

import numpy as np
import jax
import jax.numpy as jnp
from jax import lax
from jax.experimental import pallas as pl
from jax.experimental.pallas import tpu as pltpu

F32 = jnp.float32
BF16 = jnp.bfloat16
HIGHEST = lax.Precision.HIGHEST

D_MODEL = 1024
HEAD_DIM = 64
N_GDN_HEADS = 8
GDN_W = N_GDN_HEADS * HEAD_DIM
N_NSA_HEADS = 8
NSA_W = N_NSA_HEADS * HEAD_DIM
N_KV_HEADS = 2
GQA_GROUP = N_NSA_HEADS // N_KV_HEADS
KV_W = N_KV_HEADS * HEAD_DIM
CONV_W = 4
GDN_CHUNK = 64
CMP_BLOCK = 32
SEL_BLOCK = 64
TOP_K = 16
WINDOW = 512
D_FF = 4 * D_MODEL
ROPE_THETA = 10000.0
NORM_EPS = 1e-6
PAGE_SIZE = 128
N_SCORE = 64
HEAD_SHIFT = 6
INV_BASE_SHIFT = 3
QK_SCALE = HEAD_DIM ** -0.5

LANES = 128
SUBLANES = 8
VMEM_LIMIT = 56 * 1024 * 1024

C_QKV = 0
C_Z = 3 * GDN_W
C_Q = C_Z + GDN_W
C_KV = C_Q + NSA_W
C_WIN = C_KV + N_KV_HEADS * 4 * HEAD_DIM
C_AB = C_WIN + 2 * KV_W
C_GATE = C_AB + LANES
N_PROJ = C_GATE + N_KV_HEADS * LANES
ROW_W = 4 * HEAD_DIM
PROJ_ROWS = 512
SAMPLE_SEQS_PER_STEP = 8


def _cparams(sem):
    return pltpu.CompilerParams(dimension_semantics=sem, vmem_limit_bytes=VMEM_LIMIT)


def _sigmoid(x):
    return 1.0 / (1.0 + jnp.exp(-x))


def _dot(a, b):
    return jnp.dot(a, b, preferred_element_type=F32)


def _dot_nt(a, b):
    return lax.dot_general(a, b, (((1,), (1,)), ((), ())), preferred_element_type=F32)


def _dot_tn(a, b, precision=None):
    return lax.dot_general(a, b, (((0,), (0,)), ((), ())), preferred_element_type=F32, precision=precision)


def _rope128(v, cos, sin):
    lane = lax.broadcasted_iota(jnp.int32, v.shape, 1)
    first = (lane & (HEAD_DIM - 1)) < (HEAD_DIM // 2)
    swapped = jnp.where(first, pltpu.roll(v, LANES - HEAD_DIM // 2, 1), pltpu.roll(v, HEAD_DIM // 2, 1))
    return v * cos + swapped * sin


def _proj_kernel(x_ref, g_ref, w_ref, cf_ref, sf_ref, ch_ref, sh_ref,
                 qkv_ref, z_ref, q_ref, kc_ref, ks_ref, win_ref, ab_ref, gate_ref, kst_ref, wint_ref):
    x = x_ref[...]
    var = jnp.mean(x * x, axis=-1, keepdims=True)
    h = (x * lax.rsqrt(var + NORM_EPS) * g_ref[...]).astype(BF16)
    p = _dot(h, w_ref[...])
    qkv_ref[...] = p[:, C_QKV:C_Z]
    z_ref[...] = p[:, C_Z:C_Q]
    cf, sf, ch, sh = cf_ref[...], sf_ref[...], ch_ref[...], sh_ref[...]
    for j in range(NSA_W // LANES):
        q_ref[:, j * LANES:(j + 1) * LANES] = _rope128(p[:, C_Q + j * LANES:C_Q + (j + 1) * LANES], cf, sf) * QK_SCALE
    for kvh in range(N_KV_HEADS):
        base = C_KV + kvh * ROW_W
        kc_ref[kvh] = p[:, base:base + LANES]
        ks = _rope128(p[:, base + LANES:base + ROW_W], ch, sh)
        ks_ref[kvh] = ks
        kst_ref[kvh, 0] = ks.T
    wk = _rope128(p[:, C_WIN:C_WIN + LANES], cf, sf)
    wv = p[:, C_WIN + LANES:C_WIN + ROW_W]
    win_ref[:, 0:LANES] = wk
    win_ref[:, LANES:ROW_W] = wv
    for j in range(x.shape[0] // LANES):
        rows = slice(j * LANES, (j + 1) * LANES)
        wint_ref[j, 0:LANES, :] = wk[rows].T
        wint_ref[j, LANES:ROW_W, :] = wv[rows].T
    ab_ref[...] = p[:, C_AB:C_AB + LANES]
    gate_ref[...] = p[:, C_GATE:C_GATE + N_KV_HEADS * LANES]


def _proj(x, g, w, tabs, tm):
    m = x.shape[0]
    nt = tabs[0].shape[0] // tm
    row = lambda i: (i, 0)
    tab = lambda i: (i % nt, 0)
    fix = lambda i: (0, 0)
    out_shapes = (
        jax.ShapeDtypeStruct((m, 3 * GDN_W), F32),
        jax.ShapeDtypeStruct((m, GDN_W), F32),
        jax.ShapeDtypeStruct((m, NSA_W), F32),
        jax.ShapeDtypeStruct((N_KV_HEADS, m, LANES), F32),
        jax.ShapeDtypeStruct((N_KV_HEADS, m, LANES), F32),
        jax.ShapeDtypeStruct((m, ROW_W), F32),
        jax.ShapeDtypeStruct((m, LANES), F32),
        jax.ShapeDtypeStruct((m, N_KV_HEADS * LANES), F32),
        jax.ShapeDtypeStruct((N_KV_HEADS, m // tm, LANES, tm), F32),
        jax.ShapeDtypeStruct((m // LANES, ROW_W, LANES), F32),
    )
    return pl.pallas_call(
        _proj_kernel,
        grid=(m // tm,),
        in_specs=[pl.BlockSpec((tm, D_MODEL), row), pl.BlockSpec((1, D_MODEL), fix),
                  pl.BlockSpec((D_MODEL, N_PROJ), fix)] + [pl.BlockSpec((tm, LANES), tab)] * 4,
        out_specs=(pl.BlockSpec((tm, 3 * GDN_W), row), pl.BlockSpec((tm, GDN_W), row),
                   pl.BlockSpec((tm, NSA_W), row), pl.BlockSpec((N_KV_HEADS, tm, LANES), lambda i: (0, i, 0)),
                   pl.BlockSpec((N_KV_HEADS, tm, LANES), lambda i: (0, i, 0)),
                   pl.BlockSpec((tm, ROW_W), row), pl.BlockSpec((tm, LANES), row),
                   pl.BlockSpec((tm, N_KV_HEADS * LANES), row),
                   pl.BlockSpec((N_KV_HEADS, 1, LANES, tm), lambda i: (0, i, 0, 0)),
                   pl.BlockSpec((tm // LANES, ROW_W, LANES), lambda i: (i, 0, 0))),
        out_shape=out_shapes,
        compiler_params=_cparams(("parallel",)),
        name="proj",
    )(x, g, w, *tabs)


def _gdn_prep_kernel(x_ref, prev_ref, cb_ref, ab_ref, cw_ref, alog_ref, dtb_ref,
                     qkv_ref, gb_ref, xs_ref, *, tt, t_valid, t_out):
    for s in range(x_ref.shape[0]):
        _gdn_prep_one(x_ref.at[s], prev_ref.at[s], cb_ref.at[s], ab_ref.at[s], cw_ref, alog_ref, dtb_ref,
                      qkv_ref.at[s], gb_ref.at[s], xs_ref, tt=tt, t_valid=t_valid, t_out=t_out)


def _gdn_prep_one(x_ref, prev_ref, cb_ref, ab_ref, cw_ref, alog_ref, dtb_ref, qkv_ref, gb_ref, xs_ref,
                  *, tt, t_valid, t_out):
    i = pl.program_id(1)
    xs_ref[SUBLANES:SUBLANES + tt, :] = x_ref[...]

    @pl.when(i == 0)
    def _():
        xs_ref[0:SUBLANES, :] = cb_ref[...]

    @pl.when(i > 0)
    def _():
        xs_ref[0:SUBLANES, :] = prev_ref[...]

    conv = xs_ref[SUBLANES:SUBLANES + tt, :] * cw_ref[CONV_W - 1:CONV_W, :]
    for k in range(1, CONV_W):
        conv = conv + xs_ref[SUBLANES - k:SUBLANES - k + tt, :] * cw_ref[CONV_W - 1 - k:CONV_W - k, :]
    c = conv * _sigmoid(conv)

    rows = lax.broadcasted_iota(jnp.int32, (tt, LANES), 0) + i * tt
    live = rows < t_valid
    r_i = lax.broadcasted_iota(jnp.int32, (LANES, LANES), 0)
    c_i = lax.broadcasted_iota(jnp.int32, (LANES, LANES), 1)
    head_ones = ((r_i >> HEAD_SHIFT) == (c_i >> HEAD_SHIFT)).astype(BF16)

    if t_out != tt:
        qkv_ref[...] = jnp.zeros(qkv_ref.shape, F32)
        gb_ref[...] = jnp.zeros(gb_ref.shape, F32)
    for j in range(3 * GDN_W // LANES):
        blk = c[:, j * LANES:(j + 1) * LANES]
        if j < 2 * GDN_W // LANES:
            sq = blk * blk
            hi = sq.astype(BF16)
            lo = (sq - hi.astype(F32)).astype(BF16)
            ss = _dot(hi, head_ones) + _dot(lo, head_ones)
            blk = blk * lax.rsqrt(ss + NORM_EPS)
            if j < GDN_W // LANES:
                blk = blk * QK_SCALE
        qkv_ref[0:tt, j * LANES:(j + 1) * LANES] = jnp.where(live, blk, 0.0)

    ab = ab_ref[...]
    za = ab + dtb_ref[...]
    softplus = jnp.maximum(za, 0.0) + jnp.log(1.0 + jnp.exp(-jnp.abs(za)))
    gdec = -jnp.exp(alog_ref[...]) * softplus
    lane = lax.broadcasted_iota(jnp.int32, (tt, LANES), 1)
    gb = jnp.where(lane < N_GDN_HEADS, gdec, jnp.where(lane < 2 * N_GDN_HEADS, _sigmoid(ab), 0.0))
    gb_ref[0:tt, :] = jnp.where(live, gb, 0.0)


def _gdn_prep(x3, cb8, ab3, cw, alog_row, dtb_row, tt, t_valid, t_out, bb=1):
    b, t_in, _ = x3.shape
    n = t_in // tt
    blocks8 = tt // SUBLANES
    w3 = 3 * GDN_W
    kern = functools.partial(_gdn_prep_kernel, tt=tt, t_valid=t_valid, t_out=t_out)
    return pl.pallas_call(
        kern,
        grid=(b // bb, n),
        in_specs=[pl.BlockSpec((bb, tt, w3), lambda bi, i: (bi, i, 0)),
                  pl.BlockSpec((bb, SUBLANES, w3), lambda bi, i: (bi, jnp.maximum(i * blocks8 - 1, 0), 0)),
                  pl.BlockSpec((bb, SUBLANES, w3), lambda bi, i: (bi, 0, 0)),
                  pl.BlockSpec((bb, tt, LANES), lambda bi, i: (bi, i, 0)),
                  pl.BlockSpec((CONV_W, w3), lambda bi, i: (0, 0)),
                  pl.BlockSpec((1, LANES), lambda bi, i: (0, 0)),
                  pl.BlockSpec((1, LANES), lambda bi, i: (0, 0))],
        out_specs=(pl.BlockSpec((bb, t_out, w3), lambda bi, i: (bi, i, 0)),
                   pl.BlockSpec((bb, t_out, LANES), lambda bi, i: (bi, i, 0))),
        out_shape=(jax.ShapeDtypeStruct((b, n * t_out, w3), F32),
                   jax.ShapeDtypeStruct((b, n * t_out, LANES), F32)),
        scratch_shapes=[pltpu.VMEM((tt + SUBLANES, w3), F32)],
        compiler_params=_cparams(("parallel", "arbitrary")),
        name="gdn_prep",
    )(x3, x3, cb8, ab3, cw, alog_row, dtb_row)


def _gdn_scan_kernel(qkv_ref, gb_ref, z_ref, s0_ref, gn_ref, o_ref, s_ref, *, c, g):
    ci = pl.program_id(1)
    gc = g * c
    gs = g * HEAD_DIM
    n_groups = N_GDN_HEADS // g
    c_shift = int(math.log2(c))

    @pl.when(ci == 0)
    def _():
        s_ref[...] = s0_ref[...]

    def iota(shape, dim):
        return lax.broadcasted_iota(jnp.int32, shape, dim)

    r_i, c_i = iota((gc, gc), 0), iota((gc, gc), 1)
    same = (r_i >> c_shift) == (c_i >> c_shift)
    incl = same & (r_i >= c_i)
    strict = same & (r_i > c_i)
    eye = jnp.where(r_i == c_i, 1.0, 0.0)
    base_shift = min(INV_BASE_SHIFT, c_shift)
    base_blk = (r_i >> base_shift) == (c_i >> base_shift)
    tri = iota((c, gc), 0) <= (iota((c, gc), 1) & (c - 1))
    same_s = (iota((gc, gs), 0) >> c_shift) == (iota((gc, gs), 1) >> HEAD_SHIFT)
    same_s2 = jnp.concatenate([same_s, same_s], axis=0)
    gnorm = gn_ref[...]

    def stack(pieces):
        return jnp.concatenate(pieces, axis=0)

    groups = []
    for sq, gi in [(sq, gi) for sq in range(qkv_ref.shape[0]) for gi in range(n_groups)]:
        gb = gb_ref[sq]
        gcum = jnp.dot(jnp.where(iota((c, c), 0) >= iota((c, c), 1), 1.0, 0.0), gb,
                       preferred_element_type=F32, precision=HIGHEST)
        heads = range(gi * g, (gi + 1) * g)
        lane = lambda base, h: slice(base + h * HEAD_DIM, base + (h + 1) * HEAD_DIM)
        q_st = stack([qkv_ref[sq, :, lane(0, h)] for h in heads])
        k_st = stack([qkv_ref[sq, :, lane(GDN_W, h)] for h in heads])
        v_st = stack([qkv_ref[sq, :, lane(2 * GDN_W, h)] for h in heads])
        gc_st = stack([gcum[:, h:h + 1] for h in heads])
        bt_st = stack([gb[:, N_GDN_HEADS + h:N_GDN_HEADS + h + 1] for h in heads])
        gl_st = stack([jnp.broadcast_to(gcum[c - 1:c, h:h + 1], (c, 1)) for h in heads])
        gl_s = stack([jnp.broadcast_to(gcum[c - 1:c, h:h + 1], (HEAD_DIM, 1)) for h in heads])
        expand = jnp.where(iota((LANES, gc), 0) == (iota((LANES, gc), 1) >> c_shift) + gi * g, 1.0, 0.0)
        spread = jnp.dot(gb, expand, preferred_element_type=F32, precision=HIGHEST)
        gr_st = jnp.sum(jnp.where(tri, spread, 0.0), axis=0, keepdims=True)
        decay = jnp.exp(jnp.where(incl, gc_st - gr_st, -jnp.inf))
        k16 = k_st.astype(BF16)
        qk_kk = _dot_nt(jnp.concatenate([q_st, k_st], axis=0).astype(BF16), k16)
        a_mat = jnp.where(strict, bt_st * qk_kk[gc:2 * gc] * decay, 0.0)
        a_base = jnp.where(base_blk, a_mat, 0.0)
        groups.append(dict(q=q_st, k=k_st, v=v_st, gc=gc_st, bt=bt_st, gl=gl_st, gl_s=gl_s,
                           qk=qk_kk[0:gc] * decay, a=a_mat, t=eye - a_base, p=a_base, heads=heads, sq=sq, gi=gi))

    for _ in range(base_shift - 1):
        for gr in groups:
            p16 = gr['p'].astype(BF16)
            gr['p'] = _dot(p16, p16)
        for gr in groups:
            gr['t'] = gr['t'] + _dot(gr['t'].astype(BF16), gr['p'].astype(BF16))
    for lvl in range(base_shift, c_shift):
        off = ((r_i >> (lvl + 1)) == (c_i >> (lvl + 1))) & ((r_i >> lvl) != (c_i >> lvl))
        for gr in groups:
            t16 = gr['t'].astype(BF16)
            gr['at'] = _dot(jnp.where(off, gr['a'], 0.0).astype(BF16), t16)
        for gr in groups:
            gr['t'] = gr['t'] - _dot(gr['t'].astype(BF16), gr['at'].astype(BF16))

    for gr in groups:
        eg = jnp.exp(gr['gc'])
        rhs = jnp.concatenate([gr['k'] * (gr['bt'] * eg), gr['v'] * gr['bt']], axis=1).astype(BF16)
        gr['wu'] = _dot(gr['t'].astype(BF16), rhs)
        gr['qe'] = gr['q'] * eg
    for gr in groups:
        sq = gr['sq']
        s_rows = slice(gr['gi'] * gs, (gr['gi'] + 1) * gs)
        s = s_ref[sq, s_rows, :]
        wq = jnp.concatenate([gr['wu'][:, 0:HEAD_DIM], gr['qe']], axis=0)
        wq_bd = jnp.where(same_s2, jnp.concatenate([wq] * g, axis=1), 0.0).astype(BF16)
        ws_qs = _dot(wq_bd, s.astype(BF16))
        u = gr['wu'][:, HEAD_DIM:2 * HEAD_DIM] - ws_qs[0:gc]
        u16 = u.astype(BF16)
        o = ws_qs[gc:2 * gc] + _dot(gr['qk'].astype(BF16), u16)
        kd = gr['k'] * jnp.exp(gr['gl'] - gr['gc'])
        kd_bd = jnp.where(same_s, jnp.concatenate([kd] * g, axis=1), 0.0).astype(BF16)
        s_ref[sq, s_rows, :] = s * jnp.exp(gr['gl_s']) + _dot_tn(kd_bd, u16)
        on = o * lax.rsqrt(jnp.mean(o * o, axis=-1, keepdims=True) + NORM_EPS) * gnorm
        for j, h in enumerate(gr['heads']):
            sl = slice(h * HEAD_DIM, (h + 1) * HEAD_DIM)
            zh = z_ref[sq, :, sl]
            o_ref[sq, :, sl] = on[j * c:(j + 1) * c] * (zh * _sigmoid(zh))


def _gdn_scan(qkv3, gb3, z3, s0, gnorm, c, g, bb=1):
    b, tp, _ = qkv3.shape
    n = tp // c
    kern = functools.partial(_gdn_scan_kernel, c=c, g=g)
    state_rows = N_GDN_HEADS * HEAD_DIM
    state_spec = pl.BlockSpec((bb, state_rows, HEAD_DIM), lambda bi, i: (bi, 0, 0))
    o, s_new = pl.pallas_call(
        kern,
        grid=(b // bb, n),
        in_specs=[pl.BlockSpec((bb, c, 3 * GDN_W), lambda bi, i: (bi, i, 0)),
                  pl.BlockSpec((bb, c, LANES), lambda bi, i: (bi, i, 0)),
                  pl.BlockSpec((bb, c, GDN_W), lambda bi, i: (bi, i, 0)),
                  state_spec,
                  pl.BlockSpec((1, HEAD_DIM), lambda bi, i: (0, 0))],
        out_specs=(pl.BlockSpec((bb, c, GDN_W), lambda bi, i: (bi, i, 0)), state_spec),
        out_shape=(jax.ShapeDtypeStruct((b, tp, GDN_W), F32),
                   jax.ShapeDtypeStruct((b, state_rows, HEAD_DIM), F32)),
        compiler_params=_cparams(("parallel", "arbitrary")),
        name="gdn_scan",
    )(qkv3, gb3, z3, s0.reshape(b, state_rows, HEAD_DIM), gnorm)
    return o, s_new.reshape(b, N_GDN_HEADS, HEAD_DIM, HEAD_DIM)


def _compress(readers, pe_ref, wbd_ref, cc, cs):
    acc = None
    for c in range(CMP_BLOCK):
        rows = jnp.concatenate([rd(parity, c) for rd in readers for parity in range(2)], axis=0)
        part = _dot((rows + pe_ref[c:c + 1, :]).astype(BF16), wbd_ref[c])
        acc = part if acc is None else acc + part
    return _rope128(acc, cc, cs)


def _softmax_rows(s, mask):
    s = jnp.where(mask, s, -jnp.inf)
    m = jnp.max(s, axis=-1, keepdims=True)
    m = jnp.where(m > -jnp.inf, m, 0.0)
    e = jnp.exp(s - m)
    return e / jnp.maximum(jnp.sum(e, axis=-1, keepdims=True), 1e-30)


def _select_blocks(imp, qpos, n_sel, blk_axis):
    blk = lax.broadcasted_iota(jnp.int32, imp.shape, blk_axis)
    cur = qpos >> HEAD_SHIFT
    forced = (blk == 0) | (blk == cur) | (blk == cur - 1)
    valid = (blk * SEL_BLOCK <= qpos) & (blk < n_sel)
    score = jnp.where(forced, jnp.inf, jnp.where(valid, imp, -jnp.inf))
    rank = jnp.zeros(imp.shape, F32)
    for i in range(min(n_sel, N_SCORE)):
        si = score[i:i + 1, :] if blk_axis == 0 else score[:, i:i + 1]
        ahead = (si > score) | ((si == score) & (blk > i))
        rank = rank + jnp.where(ahead, 1.0, 0.0)
    return ((rank < TOP_K) & (blk < n_sel)).astype(BF16)


def _cmp_branch_rows(q_c, ckv16, qpos, qpos_t, n_half, n_sel, tq):
    r = q_c.shape[0]
    nc = 2 * n_half
    col = lax.broadcasted_iota(jnp.int32, (r, nc), 1)
    cmp_end = (2 * (col & (n_half - 1)) + (col >> int(math.log2(n_half))) + 1) * CMP_BLOCK - 1
    p_c = _softmax_rows(_dot_nt(q_c, ckv16), cmp_end <= qpos)
    o_c = _dot(p_c.astype(BF16), ckv16)[:, HEAD_DIM:2 * HEAD_DIM]
    pair = p_c[:, 0:n_half] + p_c[:, n_half:nc]
    imp = pair[0:tq]
    for g in range(1, GQA_GROUP):
        imp = imp + pair[g * tq:(g + 1) * tq]
    if n_half < N_SCORE:
        imp = jnp.concatenate([imp, jnp.zeros((tq, N_SCORE - n_half), F32)], axis=1)
    sel = _select_blocks(imp, qpos_t, n_sel, 1)
    sel_r = jnp.concatenate([sel] * GQA_GROUP, axis=0)
    return o_c, lambda expand: _dot(sel_r, expand)


def _softmax_cols(s, mask):
    s = jnp.where(mask, s, -jnp.inf)
    m = jnp.max(s, axis=0, keepdims=True)
    m = jnp.where(m > -jnp.inf, m, 0.0)
    e = jnp.exp(s - m)
    return e / jnp.maximum(jnp.sum(e, axis=0, keepdims=True), 1e-30)


def _nsa_core_lanes(q_blk, t0, ckv, n_half, n_sel, ks_rows, ks_t, n_full, kc, win_rows, win_t, win_pos0, kvh,
                    gates, sel_ref):
    tq = q_blk.shape[0]
    r = GQA_GROUP * tq
    nc = 2 * n_half
    q_all = jnp.concatenate([q_blk[:, g * HEAD_DIM:(g + 1) * HEAD_DIM] for g in range(GQA_GROUP)], axis=0)
    zeros64 = jnp.zeros((r, HEAD_DIM), F32)
    q_c = jnp.concatenate([q_all, zeros64], axis=1)
    q_c16 = q_c.astype(BF16)
    q_ct = q_c.T.astype(BF16)
    ckv16 = ckv.astype(BF16)
    qpos_l = (lax.broadcasted_iota(jnp.int32, (1, r), 1) & (tq - 1)) + t0

    row = lax.broadcasted_iota(jnp.int32, (nc, r), 0)
    cmp_end = (2 * (row & (n_half - 1)) + (row >> int(math.log2(n_half))) + 1) * CMP_BLOCK - 1
    p_c = _softmax_cols(_dot_nt(ckv16, q_c16), cmp_end <= qpos_l)
    o_ct = _dot_tn(ckv16, p_c.astype(BF16))
    pair = p_c[0:n_half] + p_c[n_half:nc]
    imp = pair[:, 0:tq]
    for g in range(1, GQA_GROUP):
        imp = imp + pair[:, g * tq:(g + 1) * tq]
    if n_half < N_SCORE:
        imp = jnp.concatenate([imp, jnp.zeros((N_SCORE - n_half, tq), F32)], axis=0)
    sel = _select_blocks(imp, qpos_l[:, 0:tq], n_sel, 0).astype(F32)
    sel_ref[...] = jnp.concatenate([sel] * GQA_GROUP, axis=1)

    blocks_per_chunk = kc // SEL_BLOCK
    krow = lax.broadcasted_iota(jnp.int32, (SEL_BLOCK, r), 0)

    def body(c, carry, causal):
        m, l, acc = carry
        s = _dot(ks_rows(c), q_ct)
        sel_c = sel_ref[pl.ds(pl.multiple_of(c * blocks_per_chunk, blocks_per_chunk), blocks_per_chunk), :]
        parts = []
        for j in range(blocks_per_chunk):
            mask = jnp.broadcast_to(sel_c[j:j + 1, :], (SEL_BLOCK, r)) > 0.5
            if causal:
                mask = mask & (krow + (c * kc + j * SEL_BLOCK) <= qpos_l)
            parts.append(jnp.where(mask, s[j * SEL_BLOCK:(j + 1) * SEL_BLOCK], -jnp.inf))
        s = jnp.concatenate(parts, axis=0)
        m_new = jnp.maximum(m, jnp.max(s, axis=0, keepdims=True))
        m_safe = jnp.where(m_new > -jnp.inf, m_new, 0.0)
        alpha = jnp.exp(m - m_safe)
        p = jnp.exp(s - m_safe)
        l = alpha * l + jnp.sum(p, axis=0, keepdims=True)
        acc = alpha * acc + _dot(ks_t(c), p.astype(BF16))
        return m_new, l, acc

    carry = (jnp.full((1, r), -jnp.inf, F32), jnp.zeros((1, r), F32), jnp.zeros((LANES, r), F32))
    carry = lax.fori_loop(0, n_full, lambda c, cr: body(c, cr, False), carry)
    _, l_s, acc_s = body(n_full, carry, True)
    o_st = acc_s / jnp.maximum(l_s, 1e-30)

    lane = lax.broadcasted_iota(jnp.int32, (r, ROW_W), 1)
    q_w = jnp.where((lane >> HEAD_SHIFT) == kvh, jnp.concatenate([q_all, q_all, zeros64, zeros64], axis=1), 0.0)
    nw = win_rows.shape[0]
    kpos = lax.broadcasted_iota(jnp.int32, (nw, r), 0) + win_pos0
    p_w = _softmax_cols(_dot(win_rows, q_w.T.astype(BF16)), (kpos <= qpos_l) & (kpos >= qpos_l - WINDOW))
    o_wt2 = _dot(win_t, p_w.astype(BF16))
    o_wt = jnp.where(kvh == 0, o_wt2[2 * HEAD_DIM:3 * HEAD_DIM], o_wt2[3 * HEAD_DIM:4 * HEAD_DIM])

    gates_t = gates.T
    outs = []
    for g in range(GQA_GROUP):
        cols = slice(g * tq, (g + 1) * tq)
        g0 = _sigmoid(gates_t[3 * g:3 * g + 1, :])
        g1 = _sigmoid(gates_t[3 * g + 1:3 * g + 2, :])
        g2 = _sigmoid(gates_t[3 * g + 2:3 * g + 3, :])
        outs.append(g0 * o_ct[HEAD_DIM:2 * HEAD_DIM, cols] + g1 * o_st[HEAD_DIM:2 * HEAD_DIM, cols]
                    + g2 * o_wt[:, cols])
    return jnp.concatenate(outs, axis=0).T


def _nsa_core_rows(q_blk, t0, ckv, n_half, sel_chunk, n_full, kc, n_sel, win_rows, win_pos0, kvh, gates):
    tq = q_blk.shape[0]
    r = GQA_GROUP * tq
    q_all = jnp.concatenate([q_blk[:, g * HEAD_DIM:(g + 1) * HEAD_DIM] for g in range(GQA_GROUP)], axis=0)
    qpos_t = lax.broadcasted_iota(jnp.int32, (tq, 1), 0) + t0
    qpos = jnp.concatenate([qpos_t] * GQA_GROUP, axis=0)
    zeros64 = jnp.zeros((r, HEAD_DIM), F32)
    q_c = jnp.concatenate([q_all, zeros64], axis=1).astype(BF16)
    ckv16 = ckv.astype(BF16)
    o_c, picked_fn = _cmp_branch_rows(q_c, ckv16, qpos, qpos_t, n_half, n_sel, tq)

    j_i = lax.broadcasted_iota(jnp.int32, (N_SCORE, kc), 0)
    k_i = lax.broadcasted_iota(jnp.int32, (N_SCORE, kc), 1) >> HEAD_SHIFT
    kcol = lax.broadcasted_iota(jnp.int32, (r, kc), 1)

    def body(c, carry, causal):
        m, l, acc = carry
        rows = sel_chunk(c)
        s = _dot_nt(q_c, rows)
        expand = jnp.where(j_i == k_i + c * (kc // SEL_BLOCK), 1.0, 0.0).astype(BF16)
        mask = picked_fn(expand) > 0.5
        if causal:
            mask = mask & (kcol + c * kc <= qpos)
        s = jnp.where(mask, s, -jnp.inf)
        m_new = jnp.maximum(m, jnp.max(s, axis=-1, keepdims=True))
        m_safe = jnp.where(m_new > -jnp.inf, m_new, 0.0)
        alpha = jnp.exp(m - m_safe)
        p = jnp.exp(s - m_safe)
        l = alpha * l + jnp.sum(p, axis=-1, keepdims=True)
        acc = alpha * acc + _dot(p.astype(BF16), rows)
        return m_new, l, acc

    carry = (jnp.full((r, 1), -jnp.inf, F32), jnp.zeros((r, 1), F32), jnp.zeros((r, LANES), F32))
    if not isinstance(n_full, int) or n_full > 0:
        carry = lax.fori_loop(0, n_full, lambda c, cr: body(c, cr, False), carry)
    _, l_s, acc_s = body(n_full, carry, True)
    o_s = acc_s[:, HEAD_DIM:2 * HEAD_DIM] / jnp.maximum(l_s, 1e-30)

    lane = lax.broadcasted_iota(jnp.int32, (r, ROW_W), 1)
    q_w = jnp.where((lane >> HEAD_SHIFT) == kvh, jnp.concatenate([q_all, q_all, zeros64, zeros64], axis=1), 0.0)
    q_w = q_w.astype(BF16)
    nw = win_rows.shape[0]
    kpos = lax.broadcasted_iota(jnp.int32, (r, nw), 1) + win_pos0
    p_w = _softmax_rows(_dot_nt(q_w, win_rows), (kpos <= qpos) & (kpos >= qpos - WINDOW))
    o_w2 = _dot(p_w.astype(BF16), win_rows)
    o_w = jnp.where(kvh == 0, o_w2[:, 2 * HEAD_DIM:3 * HEAD_DIM], o_w2[:, 3 * HEAD_DIM:4 * HEAD_DIM])

    outs = []
    for g in range(GQA_GROUP):
        rows_g = slice(g * tq, (g + 1) * tq)
        g0 = _sigmoid(gates[:, 3 * g:3 * g + 1])
        g1 = _sigmoid(gates[:, 3 * g + 1:3 * g + 2])
        g2 = _sigmoid(gates[:, 3 * g + 2:3 * g + 3])
        outs.append(g0 * o_c[rows_g] + g1 * o_s[rows_g] + g2 * o_w[rows_g])
    return outs


def _nsa_prompt_kernel(q_ref, kc_ref, ks_ref, kst_ref, win_ref, wint_ref, gate_ref, wbd_ref, pe_ref, cc_ref, cs_ref,
                       o_ref, ckv_ref, sel_ref, *, tq, t, kc):
    kvh = pl.program_id(1)
    i = pl.program_id(2)
    n_half = t // (2 * CMP_BLOCK)

    @pl.when(i == 0)
    def _():
        read = lambda parity, c: kc_ref[0, pl.ds(parity * CMP_BLOCK + c, n_half, stride=2 * CMP_BLOCK), :]
        ckv_ref[...] = _compress([read], pe_ref, wbd_ref, cc_ref[...], cs_ref[...])

    t0 = i * tq
    ks_rows = lambda c: ks_ref[0, pl.ds(pl.multiple_of(c * kc, kc), kc), :].astype(BF16)
    ks_t = lambda c: kst_ref[0, c].astype(BF16)
    span = WINDOW + tq
    start = pl.multiple_of(jnp.maximum(t0 - WINDOW, 0), tq)
    win_rows = win_ref[pl.ds(start, span), :].astype(BF16)
    blk0 = start // LANES
    win_t = jnp.concatenate([wint_ref[blk0 + j] for j in range(span // LANES)], axis=1).astype(BF16)
    o_ref[...] = _nsa_core_lanes(q_ref[...], t0, ckv_ref[...], n_half, t // SEL_BLOCK, ks_rows, ks_t, t0 // kc, kc,
                                 win_rows, win_t, start, kvh, gate_ref[...], sel_ref)


def _nsa_prompt(q, kvc, kvs, kst, win, wint, gate, wbd, pe, cc, cs, b, t, tq=2 * LANES):
    n = t // tq
    nc = t // CMP_BLOCK
    kc = kst.shape[3]
    assert tq % LANES == 0 and kc % tq == 0 and t % kc == 0
    kern = functools.partial(_nsa_prompt_kernel, tq=tq, t=t, kc=kc)
    fix2 = lambda bi, h, i: (0, 0)
    return pl.pallas_call(
        kern,
        grid=(b, N_KV_HEADS, n),
        in_specs=[pl.BlockSpec((tq, ROW_W), lambda bi, h, i: (bi * n + i, h)),
                  pl.BlockSpec((1, t, LANES), lambda bi, h, i: (h, bi, 0)),
                  pl.BlockSpec((1, t, LANES), lambda bi, h, i: (h, bi, 0)),
                  pl.BlockSpec((1, t // kc, LANES, kc), lambda bi, h, i: (h, bi, 0, 0)),
                  pl.BlockSpec((t, ROW_W), lambda bi, h, i: (bi, 0)),
                  pl.BlockSpec((t // LANES, ROW_W, LANES), lambda bi, h, i: (bi, 0, 0)),
                  pl.BlockSpec((tq, LANES), lambda bi, h, i: (bi * n + i, h)),
                  pl.BlockSpec((CMP_BLOCK, LANES, LANES), lambda bi, h, i: (0, 0, 0)),
                  pl.BlockSpec((CMP_BLOCK, LANES), fix2),
                  pl.BlockSpec((nc, LANES), fix2),
                  pl.BlockSpec((nc, LANES), fix2)],
        out_specs=pl.BlockSpec((tq, ROW_W), lambda bi, h, i: (bi * n + i, h)),
        out_shape=jax.ShapeDtypeStruct((b * t, NSA_W), F32),
        scratch_shapes=[pltpu.VMEM((nc, LANES), F32), pltpu.VMEM((N_SCORE, GQA_GROUP * tq), F32)],
        compiler_params=_cparams(("parallel", "parallel", "arbitrary")),
        name="nsa_prompt",
    )(q, kvc, kvs, kst, win, wint, gate, wbd, pe, cc, cs)


def _nsa_sample_kernel(pt_ref, *refs, n_steps, pages_per_step, past, tq, ts, n_pad, nw):
    page_refs = refs[:pages_per_step]
    (perm_ref, q_ref, kvnew_ref, wbuf_ref, wnew_ref, gate_ref, wbd_ref, pe_ref, cc_ref, cs_ref,
     o_ref, wout_ref, pge_ref, pgo_ref, pgs_ref, wn_ref) = refs[pages_per_step:]
    j = pl.program_id(1)
    half = perm_ref.shape[0] // 2
    per_parity = half // CMP_BLOCK
    slot = pl.ds(pl.multiple_of(j * per_parity, per_parity), per_parity)
    for kvh in range(N_KV_HEADS):
        for k, page_ref in enumerate(page_refs):
            rows = pl.ds(pl.multiple_of((j * pages_per_step + k) * PAGE_SIZE, PAGE_SIZE), PAGE_SIZE)
            pgs_ref[kvh, rows, :] = page_ref[0, 0, kvh, LANES:ROW_W, :].T
        cmp_t = jnp.concatenate([page_ref[0, 0, kvh, 0:LANES, :] for page_ref in page_refs], axis=1)
        grouped = _dot_nt(perm_ref[...], cmp_t.astype(BF16))
        for c in range(CMP_BLOCK):
            pge_ref[kvh, c, slot, :] = grouped[c * per_parity:(c + 1) * per_parity]
            pgo_ref[kvh, c, slot, :] = grouped[half + c * per_parity:half + (c + 1) * per_parity]

    @pl.when(j == n_steps - 1)
    def _():
        n_half = past // (2 * CMP_BLOCK)
        wbuf_t = wbuf_ref[0, 0]
        wnew = wnew_ref[0]
        wn_ref[0:WINDOW, :] = wbuf_t.T
        wn_ref[WINDOW:WINDOW + tq, :] = wnew
        wn_ref[WINDOW + tq:nw, :] = jnp.zeros((nw - WINDOW - tq, ROW_W), F32)
        win_rows = wn_ref[...].astype(BF16)

        shifted = pltpu.roll(wbuf_t, WINDOW - ts, 1)
        new_t = jnp.concatenate([wnew, jnp.zeros((LANES - tq, ROW_W), F32)], axis=0).T
        lane = lax.broadcasted_iota(jnp.int32, (ROW_W, LANES), 1)
        tail = jnp.where(lane >= LANES - ts, pltpu.roll(new_t, LANES - ts, 1), shifted[:, WINDOW - LANES:WINDOW])
        wout_ref[0, :, 0:WINDOW - LANES] = shifted[:, 0:WINDOW - LANES]
        wout_ref[0, :, WINDOW - LANES:WINDOW] = tail

        readers = []
        for kvh in range(N_KV_HEADS):
            pgs_ref[kvh, past:past + tq, :] = kvnew_ref[kvh, 0]
            pgs_ref[kvh, past + tq:n_pad, :] = jnp.zeros((n_pad - past - tq, LANES), F32)
            readers.append(lambda parity, c, kvh=kvh: (pgo_ref if parity else pge_ref)[kvh, c])
        ckv2 = _compress(readers, pe_ref, wbd_ref, cc_ref[...], cs_ref[...])
        for kvh in range(N_KV_HEADS):
            ckv = ckv2[kvh * 2 * n_half:(kvh + 1) * 2 * n_half]
            sel_chunk = lambda c, kvh=kvh: pgs_ref[kvh].astype(BF16)
            outs = _nsa_core_rows(q_ref[0, :, kvh * ROW_W:(kvh + 1) * ROW_W], past, ckv, n_half, sel_chunk, 0,
                             n_pad, past // SEL_BLOCK + 1, win_rows, past - WINDOW, kvh,
                             gate_ref[0, :, kvh * LANES:(kvh + 1) * LANES])
            for g in range(GQA_GROUP):
                col = kvh * ROW_W + g * HEAD_DIM
                o_ref[0, :, col:col + HEAD_DIM] = outs[g]


def _page_group_permutation(pages_per_step):
    n = pages_per_step * PAGE_SIZE
    half = n // 2
    per_parity = half // CMP_BLOCK
    r = np.arange(n)
    parity, c, i = r // half, (r % half) // per_parity, r % per_parity
    perm = np.zeros((n, n), np.float32)
    perm[r, (2 * i + parity) * CMP_BLOCK + c] = 1.0
    return jnp.asarray(perm, dtype=BF16)


def _nsa_sample(page_table, pool_t, layer, ts, q8, kvnew8, win_t, wnew8, gate8, wbd, pe, cc, cs, pages_per_step=4):
    b, n_pages = page_table.shape
    past = n_pages * PAGE_SIZE
    tq = q8.shape[1]
    n_pad = past + SEL_BLOCK
    nw = WINDOW + LANES
    nc2 = N_KV_HEADS * (past // CMP_BLOCK)
    n_steps = n_pages // pages_per_step
    n_half = past // (2 * CMP_BLOCK)
    perm = _page_group_permutation(pages_per_step)
    assert (pages_per_step * PAGE_SIZE // (2 * CMP_BLOCK)) % SUBLANES == 0
    kern = functools.partial(_nsa_sample_kernel, n_steps=n_steps, pages_per_step=pages_per_step, past=past,
                             tq=tq, ts=ts, n_pad=n_pad, nw=nw)
    per_b = lambda bi, j, pt: (bi, 0, 0)
    fix2 = lambda bi, j, pt: (0, 0)
    page_spec = lambda k: pl.BlockSpec((1, 1, N_KV_HEADS, ROW_W, PAGE_SIZE),
                                       lambda bi, j, pt: (layer, pt[bi, j * pages_per_step + k], 0, 0, 0))
    grid_spec = pltpu.PrefetchScalarGridSpec(
        num_scalar_prefetch=1,
        grid=(b, n_steps),
        in_specs=[page_spec(k) for k in range(pages_per_step)] + [
                  pl.BlockSpec(perm.shape, fix2),
                  pl.BlockSpec((1, tq, NSA_W), per_b),
                  pl.BlockSpec((N_KV_HEADS, 1, tq, LANES), lambda bi, j, pt: (0, bi, 0, 0)),
                  pl.BlockSpec((1, 1, ROW_W, WINDOW), lambda bi, j, pt: (layer, bi, 0, 0)),
                  pl.BlockSpec((1, tq, ROW_W), per_b),
                  pl.BlockSpec((1, tq, N_KV_HEADS * LANES), per_b),
                  pl.BlockSpec((CMP_BLOCK, LANES, LANES), lambda bi, j, pt: (0, 0, 0)),
                  pl.BlockSpec((CMP_BLOCK, LANES), fix2),
                  pl.BlockSpec((nc2, LANES), fix2),
                  pl.BlockSpec((nc2, LANES), fix2)],
        out_specs=(pl.BlockSpec((1, tq, NSA_W), per_b), pl.BlockSpec((1, ROW_W, WINDOW), per_b)),
        scratch_shapes=[pltpu.VMEM((N_KV_HEADS, CMP_BLOCK, n_half, LANES), F32),
                        pltpu.VMEM((N_KV_HEADS, CMP_BLOCK, n_half, LANES), F32),
                        pltpu.VMEM((N_KV_HEADS, n_pad, LANES), F32), pltpu.VMEM((nw, ROW_W), F32)],
    )
    return pl.pallas_call(
        kern,
        grid_spec=grid_spec,
        out_shape=(jax.ShapeDtypeStruct((b, tq, NSA_W), F32), jax.ShapeDtypeStruct((b, ROW_W, WINDOW), F32)),
        compiler_params=_cparams(("parallel", "arbitrary")),
        name="nsa_sample",
    )(page_table, *([pool_t] * pages_per_step), perm, q8, kvnew8, win_t, wnew8, gate8, wbd, pe, cc, cs)


def _mlp_kernel(x_ref, og_ref, on_ref, wo_ref, gm_ref, wu_ref, wd_ref, gf_ref, y_ref, x1_ref, h_ref, acc_ref,
                *, final_norm):
    f = pl.program_id(1)

    @pl.when(f == 0)
    def _():
        x1 = (x_ref[...] + _dot(og_ref[...].astype(BF16), wo_ref[0:GDN_W, :])
              + _dot(on_ref[...].astype(BF16), wo_ref[GDN_W:GDN_W + NSA_W, :]))
        x1_ref[...] = x1
        var = jnp.mean(x1 * x1, axis=-1, keepdims=True)
        h_ref[...] = (x1 * lax.rsqrt(var + NORM_EPS) * gm_ref[...]).astype(BF16)
        acc_ref[...] = jnp.zeros(acc_ref.shape, F32)

    up = jnp.maximum(_dot(h_ref[...], wu_ref[...]), 0.0)
    acc_ref[...] += _dot((up * up).astype(BF16), wd_ref[...])

    @pl.when(f == pl.num_programs(1) - 1)
    def _():
        y = x1_ref[...] + acc_ref[...]
        if final_norm:
            var = jnp.mean(y * y, axis=-1, keepdims=True)
            y = y * lax.rsqrt(var + NORM_EPS) * gf_ref[...]
        y_ref[...] = y


def _mlp(x, og, on, wo, gm, wu, wd, gf, final_norm, tm, tf=D_FF):
    m = x.shape[0]
    kern = functools.partial(_mlp_kernel, final_norm=final_norm)
    row = lambda i, f: (i, 0)
    fix = lambda i, f: (0, 0)
    return pl.pallas_call(
        kern,
        grid=(m // tm, D_FF // tf),
        in_specs=[pl.BlockSpec((tm, D_MODEL), row), pl.BlockSpec((tm, GDN_W), row), pl.BlockSpec((tm, NSA_W), row),
                  pl.BlockSpec((D_MODEL, D_MODEL), fix), pl.BlockSpec((1, D_MODEL), fix),
                  pl.BlockSpec((D_MODEL, tf), lambda i, f: (0, f)), pl.BlockSpec((tf, D_MODEL), lambda i, f: (f, 0)),
                  pl.BlockSpec((1, D_MODEL), fix)],
        out_specs=pl.BlockSpec((tm, D_MODEL), row),
        out_shape=jax.ShapeDtypeStruct((m, D_MODEL), F32),
        scratch_shapes=[pltpu.VMEM((tm, D_MODEL), F32), pltpu.VMEM((tm, D_MODEL), BF16),
                        pltpu.VMEM((tm, D_MODEL), F32)],
        compiler_params=_cparams(("parallel", "arbitrary")),
        name="mlp",
    )(x, og, on, wo, gm, wu, wd, gf)


def _proj_columns():
    src = np.full((N_PROJ,), -1, np.int64)
    o_z, o_a, o_b = 3 * GDN_W, 4 * GDN_W, 4 * GDN_W + N_GDN_HEADS
    o_q = o_b + N_GDN_HEADS
    o_kv = o_q + NSA_W
    o_g = o_kv + 6 * KV_W
    src[C_QKV:C_Z] = np.arange(0, 3 * GDN_W)
    src[C_Z:C_Q] = np.arange(o_z, o_z + GDN_W)
    src[C_Q:C_KV] = np.arange(o_q, o_q + NSA_W)
    for kvh in range(N_KV_HEADS):
        for s in range(4):
            dst = C_KV + kvh * ROW_W + s * HEAD_DIM
            src[dst:dst + HEAD_DIM] = o_kv + s * KV_W + kvh * HEAD_DIM + np.arange(HEAD_DIM)
    src[C_WIN:C_WIN + 2 * KV_W] = o_kv + 4 * KV_W + np.arange(2 * KV_W)
    src[C_AB:C_AB + N_GDN_HEADS] = o_a + np.arange(N_GDN_HEADS)
    src[C_AB + N_GDN_HEADS:C_AB + 2 * N_GDN_HEADS] = o_b + np.arange(N_GDN_HEADS)
    for kvh in range(N_KV_HEADS):
        dst = C_GATE + kvh * LANES
        src[dst:dst + 3 * GQA_GROUP] = o_g + kvh * 3 * GQA_GROUP + np.arange(3 * GQA_GROUP)
    return src


def _rope_tables(pos):
    half = HEAD_DIM // 2
    inv_freq = ROPE_THETA ** (-jnp.arange(half, dtype=F32) / half)
    ang = pos.astype(F32)[:, None] * inv_freq[None, :]
    cos, sin = jnp.cos(ang), jnp.sin(ang)
    c64 = jnp.concatenate([cos, cos], axis=1)
    s64 = jnp.concatenate([-sin, sin], axis=1)
    one, zero = jnp.ones_like(c64), jnp.zeros_like(s64)
    return (jnp.concatenate([c64, c64], axis=1), jnp.concatenate([s64, s64], axis=1),
            jnp.concatenate([c64, one], axis=1), jnp.concatenate([s64, zero], axis=1))


def _cmp_tables(n_blocks):
    n_half = n_blocks // 2
    r = np.arange(n_blocks)
    blk = 2 * (r % n_half) + r // n_half
    end_pos = jnp.asarray((blk + 1) * CMP_BLOCK - 1)
    _, _, cc, cs = _rope_tables(end_pos)
    return cc, cs


def _layer_params(l, norm_mix, w_in, conv_w, a_log, dt_bias, gdn_norm, cmp_pe_k, cmp_w_k, cmp_pe_v, cmp_w_v,
                  w_out, norm_mlp, w_up, w_down):
    src = _proj_columns()
    w_ext = jnp.concatenate([w_in[l], jnp.zeros((D_MODEL, 1), F32)], axis=1)
    w_p = jnp.take(w_ext, jnp.asarray(np.where(src < 0, w_in.shape[2], src)), axis=1).astype(BF16)
    zeros = jnp.zeros((CMP_BLOCK, HEAD_DIM, HEAD_DIM), F32)
    wbd = jnp.concatenate([jnp.concatenate([cmp_w_k[l], zeros], axis=2),
                           jnp.concatenate([zeros, cmp_w_v[l]], axis=2)], axis=1).astype(BF16)
    pad_row = lambda v: jnp.pad(v.astype(F32), (0, LANES - v.shape[0]))[None, :]
    return dict(
        norm_mix=norm_mix[l][None, :], w_p=w_p, conv_w=conv_w[l], alog=pad_row(a_log[l]), dtb=pad_row(dt_bias[l]),
        gdn_norm=gdn_norm[l][None, :], wbd=wbd, pe=jnp.concatenate([cmp_pe_k[l], cmp_pe_v[l]], axis=1),
        w_out=w_out[l].astype(BF16), norm_mlp=norm_mlp[l][None, :], w_up=w_up[l].astype(BF16),
        w_down=w_down[l].astype(BF16))


def _prompt_layer(x, lp, tabs, cmp_tabs, gf, final_norm, b, t):
    qkv_pre, z, q, kvc, kvs, win, ab, gate, kst, wint = _proj(x, lp['norm_mix'], lp['w_p'], tabs, PROJ_ROWS)
    w3 = 3 * GDN_W
    cb8 = jnp.zeros((b, SUBLANES, w3), F32)
    qkv3, gb3 = _gdn_prep(qkv_pre.reshape(b, t, w3), cb8, ab.reshape(b, t, LANES), lp['conv_w'], lp['alog'],
                          lp['dtb'], 256, t, 256)
    s0 = jnp.zeros((b, N_GDN_HEADS, HEAD_DIM, HEAD_DIM), F32)
    o_gdn, s_new = _gdn_scan(qkv3, gb3, z.reshape(b, t, GDN_W), s0, lp['gdn_norm'], GDN_CHUNK, 4)
    o_nsa = _nsa_prompt(q, kvc, kvs, kst, win, wint, gate, lp['wbd'], lp['pe'], cmp_tabs[0], cmp_tabs[1], b, t)
    y = _mlp(x, o_gdn.reshape(b * t, GDN_W), o_nsa, lp['w_out'], lp['norm_mlp'], lp['w_up'], lp['w_down'], gf,
             final_norm, 512)
    rows = jnp.concatenate([kvc, kvs], axis=2).reshape(N_KV_HEADS, b, t, 4, HEAD_DIM).transpose(1, 0, 2, 3, 4)
    win_new = win.reshape(b, t, 2, N_KV_HEADS, HEAD_DIM)[:, t - min(WINDOW, t):]
    conv_new = qkv_pre.reshape(b, t, w3)[:, t - (CONV_W - 1):]
    return y, rows, win_new, s_new, conv_new


def _pad_rows(a, n):
    return jnp.pad(a, ((0, 0), (0, n - a.shape[1]), (0, 0)))


def _sample_layer(x, lp, tabs, cmp_tabs, gf, final_norm, b, t, layer, pool_t, page_table, win_t, s0, conv_buf):
    m = b * t
    qkv_pre, z, q, kvc, kvs, win, ab, gate, _, _ = _proj(x, lp['norm_mix'], lp['w_p'], tabs, PROJ_ROWS)
    w3 = 3 * GDN_W
    tp = SUBLANES
    cb8 = jnp.pad(conv_buf, ((0, 0), (SUBLANES - (CONV_W - 1), 0), (0, 0)))
    qkv3, gb3 = _gdn_prep(_pad_rows(qkv_pre.reshape(b, t, w3), tp), cb8, _pad_rows(ab.reshape(b, t, LANES), tp),
                          lp['conv_w'], lp['alog'], lp['dtb'], tp, t, tp, bb=SAMPLE_SEQS_PER_STEP)
    o_gdn, s_new = _gdn_scan(qkv3, gb3, _pad_rows(z.reshape(b, t, GDN_W), tp), s0, lp['gdn_norm'], tp,
                             N_GDN_HEADS, bb=SAMPLE_SEQS_PER_STEP // 2)
    o_gdn = o_gdn[:, :t].reshape(m, GDN_W)
    kvnew8 = jnp.pad(kvs.reshape(N_KV_HEADS, b, t, LANES), ((0, 0), (0, 0), (0, tp - t), (0, 0)))
    o_nsa, win_out_t = _nsa_sample(page_table, pool_t, layer, t, _pad_rows(q.reshape(b, t, NSA_W), tp), kvnew8,
                                   win_t, _pad_rows(win.reshape(b, t, ROW_W), tp),
                                   _pad_rows(gate.reshape(b, t, N_KV_HEADS * LANES), tp), lp['wbd'], lp['pe'],
                                   cmp_tabs[0], cmp_tabs[1])
    o_nsa = o_nsa[:, :t].reshape(m, NSA_W)
    y = _mlp(x, o_gdn, o_nsa, lp['w_out'], lp['norm_mlp'], lp['w_up'], lp['w_down'], gf, final_norm, 512)
    rows = jnp.concatenate([kvc, kvs], axis=2).reshape(N_KV_HEADS, b, t, 4, HEAD_DIM).transpose(1, 0, 2, 3, 4)
    win_new = win_out_t.reshape(b, 2, N_KV_HEADS, HEAD_DIM, WINDOW).transpose(0, 4, 1, 2, 3)
    conv_new = jnp.concatenate([conv_buf, qkv_pre.reshape(b, t, w3)], axis=1)[:, t:]
    return y, rows, win_new, s_new, conv_new


def kernel(x_prompt, x_sample, cache_kv, page_table, state_win, state_gdn, state_conv, norm_mix, w_in, conv_w, a_log, dt_bias, gdn_norm, cmp_pe_k, cmp_w_k, cmp_pe_v, cmp_w_v, w_out, norm_mlp, w_up, w_down, norm_final):
    bp, tp_, _ = x_prompt.shape
    bs, ts, _ = x_sample.shape
    depth = cache_kv.shape[0]
    n_pages = page_table.shape[1]
    past = n_pages * PAGE_SIZE
    assert state_win.shape[2] == WINDOW and tp_ % 512 == 0 and ts <= SUBLANES and past % (2 * CMP_BLOCK) == 0

    tabs_p = _rope_tables(jnp.arange(tp_))
    tabs_s = tuple(jnp.tile(tb, (bs, 1)) for tb in _rope_tables(past + jnp.arange(ts)))
    cmp_p = _cmp_tables(tp_ // CMP_BLOCK)
    cmp_s = tuple(jnp.tile(tb, (N_KV_HEADS, 1)) for tb in _cmp_tables((past + ts) // CMP_BLOCK))
    pool_t = cache_kv.transpose(0, 1, 2, 4, 5, 3).reshape(depth, cache_kv.shape[1], N_KV_HEADS, ROW_W, PAGE_SIZE)
    win_t = state_win.transpose(0, 1, 3, 4, 5, 2).reshape(depth, bs, ROW_W, WINDOW)
    gf = norm_final[None, :]

    xp = x_prompt.reshape(bp * tp_, D_MODEL)
    xs = x_sample.reshape(bs * ts, D_MODEL)
    outs = [[] for _ in range(8)]
    for l in range(depth):
        lp = _layer_params(l, norm_mix, w_in, conv_w, a_log, dt_bias, gdn_norm, cmp_pe_k, cmp_w_k, cmp_pe_v,
                           cmp_w_v, w_out, norm_mlp, w_up, w_down)
        last = l == depth - 1
        xp, r, w, s, c = _prompt_layer(xp, lp, tabs_p, cmp_p, gf, last, bp, tp_)
        for k, v in zip((0, 2, 4, 6), (r, w, s, c)):
            outs[k].append(v)
        xs, r, w, s, c = _sample_layer(xs, lp, tabs_s, cmp_s, gf, last, bs, ts, l, pool_t, page_table,
                                       win_t, state_gdn[l], state_conv[l])
        for k, v in zip((1, 3, 5, 7), (r, w, s, c)):
            outs[k].append(v)
    return (xp.reshape(bp, tp_, D_MODEL), xs.reshape(bs, ts, D_MODEL)) + tuple(jnp.stack(o) for o in outs)
```

```python
import functools
import math

import numpy as np
import jax
import jax.numpy as jnp
from jax import lax
from jax.experimental import pallas as pl
from jax.experimental.pallas import tpu as pltpu

F32 = jnp.float32
BF16 = jnp.bfloat16
HIGHEST = lax.Precision.HIGHEST

D_MODEL = 1024
HEAD_DIM = 64
N_GDN_HEADS = 8
GDN_W = N_GDN_HEADS * HEAD_DIM
N_NSA_HEADS = 8
NSA_W = N_NSA_HEADS * HEAD_DIM
N_KV_HEADS = 2
GQA_GROUP = N_NSA_HEADS // N_KV_HEADS
KV_W = N_KV_HEADS * HEAD_DIM
CONV_W = 4
GDN_CHUNK = 64
CMP_BLOCK = 32
SEL_BLOCK = 64
TOP_K = 16
WINDOW = 512
D_FF = 4 * D_MODEL
ROPE_THETA = 10000.0
NORM_EPS = 1e-6
PAGE_SIZE = 128
N_SCORE = 64
HEAD_SHIFT = 6
INV_BASE_SHIFT = 3
QK_SCALE = HEAD_DIM ** -0.5

LANES = 128
SUBLANES = 8
VMEM_LIMIT = 56 * 1024 * 1024

C_QKV = 0
C_Z = 3 * GDN_W
C_Q = C_Z + GDN_W
C_KV = C_Q + NSA_W
C_WIN = C_KV + N_KV_HEADS * 4 * HEAD_DIM
C_AB = C_WIN + 2 * KV_W
C_GATE = C_AB + LANES
N_PROJ = C_GATE + N_KV_HEADS * LANES
ROW_W = 4 * HEAD_DIM
PROJ_ROWS = 512
SAMPLE_SEQS_PER_STEP = 8


def _cparams(sem):
    return pltpu.CompilerParams(dimension_semantics=sem, vmem_limit_bytes=VMEM_LIMIT)


def _sigmoid(x):
    return 1.0 / (1.0 + jnp.exp(-x))


def _dot(a, b):
    return jnp.dot(a, b, preferred_element_type=F32)


def _dot_nt(a, b):
    return lax.dot_general(a, b, (((1,), (1,)), ((), ())), preferred_element_type=F32)


def _dot_tn(a, b, precision=None):
    return lax.dot_general(a, b, (((0,), (0,)), ((), ())), preferred_element_type=F32, precision=precision)


def _rope128(v, cos, sin):
    lane = lax.broadcasted_iota(jnp.int32, v.shape, 1)
    first = (lane & (HEAD_DIM - 1)) < (HEAD_DIM // 2)
    swapped = jnp.where(first, pltpu.roll(v, LANES - HEAD_DIM // 2, 1), pltpu.roll(v, HEAD_DIM // 2, 1))
    return v * cos + swapped * sin


def _proj_kernel(x_ref, g_ref, w_ref, cf_ref, sf_ref, ch_ref, sh_ref,
                 qkv_ref, z_ref, q_ref, kc_ref, ks_ref, win_ref, ab_ref, gate_ref, kst_ref, wint_ref):
    x = x_ref[...]
    var = jnp.mean(x * x, axis=-1, keepdims=True)
    h = (x * lax.rsqrt(var + NORM_EPS) * g_ref[...]).astype(BF16)
    p = _dot(h, w_ref[...])
    qkv_ref[...] = p[:, C_QKV:C_Z]
    z_ref[...] = p[:, C_Z:C_Q]
    cf, sf, ch, sh = cf_ref[...], sf_ref[...], ch_ref[...], sh_ref[...]
    for j in range(NSA_W // LANES):
        q_ref[:, j * LANES:(j + 1) * LANES] = _rope128(p[:, C_Q + j * LANES:C_Q + (j + 1) * LANES], cf, sf) * QK_SCALE
    for kvh in range(N_KV_HEADS):
        base = C_KV + kvh * ROW_W
        kc_ref[kvh] = p[:, base:base + LANES]
        ks = _rope128(p[:, base + LANES:base + ROW_W], ch, sh)
        ks_ref[kvh] = ks
        kst_ref[kvh, 0] = ks.T
    wk = _rope128(p[:, C_WIN:C_WIN + LANES], cf, sf)
    wv = p[:, C_WIN + LANES:C_WIN + ROW_W]
    win_ref[:, 0:LANES] = wk
    win_ref[:, LANES:ROW_W] = wv
    for j in range(x.shape[0] // LANES):
        rows = slice(j * LANES, (j + 1) * LANES)
        wint_ref[j, 0:LANES, :] = wk[rows].T
        wint_ref[j, LANES:ROW_W, :] = wv[rows].T
    ab_ref[...] = p[:, C_AB:C_AB + LANES]
    gate_ref[...] = p[:, C_GATE:C_GATE + N_KV_HEADS * LANES]


def _proj(x, g, w, tabs, tm):
    m = x.shape[0]
    nt = tabs[0].shape[0] // tm
    row = lambda i: (i, 0)
    tab = lambda i: (i % nt, 0)
    fix = lambda i: (0, 0)
    out_shapes = (
        jax.ShapeDtypeStruct((m, 3 * GDN_W), F32),
        jax.ShapeDtypeStruct((m, GDN_W), F32),
        jax.ShapeDtypeStruct((m, NSA_W), F32),
        jax.ShapeDtypeStruct((N_KV_HEADS, m, LANES), F32),
        jax.ShapeDtypeStruct((N_KV_HEADS, m, LANES), F32),
        jax.ShapeDtypeStruct((m, ROW_W), F32),
        jax.ShapeDtypeStruct((m, LANES), F32),
        jax.ShapeDtypeStruct((m, N_KV_HEADS * LANES), F32),
        jax.ShapeDtypeStruct((N_KV_HEADS, m // tm, LANES, tm), F32),
        jax.ShapeDtypeStruct((m // LANES, ROW_W, LANES), F32),
    )
    return pl.pallas_call(
        _proj_kernel,
        grid=(m // tm,),
        in_specs=[pl.BlockSpec((tm, D_MODEL), row), pl.BlockSpec((1, D_MODEL), fix),
                  pl.BlockSpec((D_MODEL, N_PROJ), fix)] + [pl.BlockSpec((tm, LANES), tab)] * 4,
        out_specs=(pl.BlockSpec((tm, 3 * GDN_W), row), pl.BlockSpec((tm, GDN_W), row),
                   pl.BlockSpec((tm, NSA_W), row), pl.BlockSpec((N_KV_HEADS, tm, LANES), lambda i: (0, i, 0)),
                   pl.BlockSpec((N_KV_HEADS, tm, LANES), lambda i: (0, i, 0)),
                   pl.BlockSpec((tm, ROW_W), row), pl.BlockSpec((tm, LANES), row),
                   pl.BlockSpec((tm, N_KV_HEADS * LANES), row),
                   pl.BlockSpec((N_KV_HEADS, 1, LANES, tm), lambda i: (0, i, 0, 0)),
                   pl.BlockSpec((tm // LANES, ROW_W, LANES), lambda i: (i, 0, 0))),
        out_shape=out_shapes,
        compiler_params=_cparams(("parallel",)),
        name="proj",
    )(x, g, w, *tabs)


def _gdn_prep_kernel(x_ref, prev_ref, cb_ref, ab_ref, cw_ref, alog_ref, dtb_ref,
                     qkv_ref, gb_ref, xs_ref, *, tt, t_valid, t_out):
    for s in range(x_ref.shape[0]):
        _gdn_prep_one(x_ref.at[s], prev_ref.at[s], cb_ref.at[s], ab_ref.at[s], cw_ref, alog_ref, dtb_ref,
                      qkv_ref.at[s], gb_ref.at[s], xs_ref, tt=tt, t_valid=t_valid, t_out=t_out)


def _gdn_prep_one(x_ref, prev_ref, cb_ref, ab_ref, cw_ref, alog_ref, dtb_ref, qkv_ref, gb_ref, xs_ref,
                  *, tt, t_valid, t_out):
    i = pl.program_id(1)
    xs_ref[SUBLANES:SUBLANES + tt, :] = x_ref[...]

    @pl.when(i == 0)
    def _():
        xs_ref[0:SUBLANES, :] = cb_ref[...]

    @pl.when(i > 0)
    def _():
        xs_ref[0:SUBLANES, :] = prev_ref[...]

    conv = xs_ref[SUBLANES:SUBLANES + tt, :] * cw_ref[CONV_W - 1:CONV_W, :]
    for k in range(1, CONV_W):
        conv = conv + xs_ref[SUBLANES - k:SUBLANES - k + tt, :] * cw_ref[CONV_W - 1 - k:CONV_W - k, :]
    c = conv * _sigmoid(conv)

    rows = lax.broadcasted_iota(jnp.int32, (tt, LANES), 0) + i * tt
    live = rows < t_valid
    r_i = lax.broadcasted_iota(jnp.int32, (LANES, LANES), 0)
    c_i = lax.broadcasted_iota(jnp.int32, (LANES, LANES), 1)
    head_ones = ((r_i >> HEAD_SHIFT) == (c_i >> HEAD_SHIFT)).astype(BF16)

    if t_out != tt:
        qkv_ref[...] = jnp.zeros(qkv_ref.shape, F32)
        gb_ref[...] = jnp.zeros(gb_ref.shape, F32)
    for j in range(3 * GDN_W // LANES):
        blk = c[:, j * LANES:(j + 1) * LANES]
        if j < 2 * GDN_W // LANES:
            sq = blk * blk
            hi = sq.astype(BF16)
            lo = (sq - hi.astype(F32)).astype(BF16)
            ss = _dot(hi, head_ones) + _dot(lo, head_ones)
            blk = blk * lax.rsqrt(ss + NORM_EPS)
            if j < GDN_W // LANES:
                blk = blk * QK_SCALE
        qkv_ref[0:tt, j * LANES:(j + 1) * LANES] = jnp.where(live, blk, 0.0)

    ab = ab_ref[...]
    za = ab + dtb_ref[...]
    softplus = jnp.maximum(za, 0.0) + jnp.log(1.0 + jnp.exp(-jnp.abs(za)))
    gdec = -jnp.exp(alog_ref[...]) * softplus
    lane = lax.broadcasted_iota(jnp.int32, (tt, LANES), 1)
    gb = jnp.where(lane < N_GDN_HEADS, gdec, jnp.where(lane < 2 * N_GDN_HEADS, _sigmoid(ab), 0.0))
    gb_ref[0:tt, :] = jnp.where(live, gb, 0.0)


def _gdn_prep(x3, cb8, ab3, cw, alog_row, dtb_row, tt, t_valid, t_out, bb=1):
    b, t_in, _ = x3.shape
    n = t_in // tt
    blocks8 = tt // SUBLANES
    w3 = 3 * GDN_W
    kern = functools.partial(_gdn_prep_kernel, tt=tt, t_valid=t_valid, t_out=t_out)
    return pl.pallas_call(
        kern,
        grid=(b // bb, n),
        in_specs=[pl.BlockSpec((bb, tt, w3), lambda bi, i: (bi, i, 0)),
                  pl.BlockSpec((bb, SUBLANES, w3), lambda bi, i: (bi, jnp.maximum(i * blocks8 - 1, 0), 0)),
                  pl.BlockSpec((bb, SUBLANES, w3), lambda bi, i: (bi, 0, 0)),
                  pl.BlockSpec((bb, tt, LANES), lambda bi, i: (bi, i, 0)),
                  pl.BlockSpec((CONV_W, w3), lambda bi, i: (0, 0)),
                  pl.BlockSpec((1, LANES), lambda bi, i: (0, 0)),
                  pl.BlockSpec((1, LANES), lambda bi, i: (0, 0))],
        out_specs=(pl.BlockSpec((bb, t_out, w3), lambda bi, i: (bi, i, 0)),
                   pl.BlockSpec((bb, t_out, LANES), lambda bi, i: (bi, i, 0))),
        out_shape=(jax.ShapeDtypeStruct((b, n * t_out, w3), F32),
                   jax.ShapeDtypeStruct((b, n * t_out, LANES), F32)),
        scratch_shapes=[pltpu.VMEM((tt + SUBLANES, w3), F32)],
        compiler_params=_cparams(("parallel", "arbitrary")),
        name="gdn_prep",
    )(x3, x3, cb8, ab3, cw, alog_row, dtb_row)


def _gdn_scan_kernel(qkv_ref, gb_ref, z_ref, s0_ref, gn_ref, o_ref, s_ref, *, c, g):
    ci = pl.program_id(1)
    gc = g * c
    gs = g * HEAD_DIM
    n_groups = N_GDN_HEADS // g
    c_shift = int(math.log2(c))

    @pl.when(ci == 0)
    def _():
        s_ref[...] = s0_ref[...]

    def iota(shape, dim):
        return lax.broadcasted_iota(jnp.int32, shape, dim)

    r_i, c_i = iota((gc, gc), 0), iota((gc, gc), 1)
    same = (r_i >> c_shift) == (c_i >> c_shift)
    incl = same & (r_i >= c_i)
    strict = same & (r_i > c_i)
    eye = jnp.where(r_i == c_i, 1.0, 0.0)
    base_shift = min(INV_BASE_SHIFT, c_shift)
    base_blk = (r_i >> base_shift) == (c_i >> base_shift)
    same_s = (iota((gc, gs), 0) >> c_shift) == (iota((gc, gs), 1) >> HEAD_SHIFT)
    same_s2 = jnp.concatenate([same_s, same_s], axis=0)
    gnorm = gn_ref[...]

    def stack(pieces):
        return jnp.concatenate(pieces, axis=0)

    groups = []
    for sq, gi in [(sq, gi) for sq in range(qkv_ref.shape[0]) for gi in range(n_groups)]:
        gb = gb_ref[sq]
        gcum = jnp.dot(jnp.where(iota((c, c), 0) >= iota((c, c), 1), 1.0, 0.0), gb,
                       preferred_element_type=F32, precision=HIGHEST)
        heads = range(gi * g, (gi + 1) * g)
        lane = lambda base, h: slice(base + h * HEAD_DIM, base + (h + 1) * HEAD_DIM)
        q_st = stack([qkv_ref[sq, :, lane(0, h)] for h in heads])
        k_st = stack([qkv_ref[sq, :, lane(GDN_W, h)] for h in heads])
        v_st = stack([qkv_ref[sq, :, lane(2 * GDN_W, h)] for h in heads])
        gc_st = stack([gcum[:, h:h + 1] for h in heads])
        bt_st = stack([gb[:, N_GDN_HEADS + h:N_GDN_HEADS + h + 1] for h in heads])
        gl_st = stack([jnp.broadcast_to(gcum[c - 1:c, h:h + 1], (c, 1)) for h in heads])
        gl_s = stack([jnp.broadcast_to(gcum[c - 1:c, h:h + 1], (HEAD_DIM, 1)) for h in heads])
        gr_st = jnp.broadcast_to(gc_st, (gc, LANES)).T[0:1, :]
        decay = jnp.exp(jnp.where(incl, gc_st - gr_st, -jnp.inf))
        k16 = k_st.astype(BF16)
        qk_kk = _dot_nt(jnp.concatenate([q_st, k_st], axis=0).astype(BF16), k16)
        a_mat = jnp.where(strict, bt_st * qk_kk[gc:2 * gc] * decay, 0.0)
        a_base = jnp.where(base_blk, a_mat, 0.0)
        groups.append(dict(q=q_st, k=k_st, v=v_st, gc=gc_st, bt=bt_st, gl=gl_st, gl_s=gl_s,
                           qk=qk_kk[0:gc] * decay, a=a_mat, t=eye - a_base, p=a_base, heads=heads, sq=sq, gi=gi))

    for _ in range(base_shift - 1):
        for gr in groups:
            p16 = gr['p'].astype(BF16)
            gr['p'] = _dot(p16, p16)
        for gr in groups:
            gr['t'] = gr['t'] + _dot(gr['t'].astype(BF16), gr['p'].astype(BF16))
    for lvl in range(base_shift, c_shift):
        off = ((r_i >> (lvl + 1)) == (c_i >> (lvl + 1))) & ((r_i >> lvl) != (c_i >> lvl))
        for gr in groups:
            t16 = gr['t'].astype(BF16)
            gr['at'] = _dot(jnp.where(off, gr['a'], 0.0).astype(BF16), t16)
        for gr in groups:
            gr['t'] = gr['t'] - _dot(gr['t'].astype(BF16), gr['at'].astype(BF16))

    for gr in groups:
        eg = jnp.exp(gr['gc'])
        rhs = jnp.concatenate([gr['k'] * (gr['bt'] * eg), gr['v'] * gr['bt']], axis=1).astype(BF16)
        gr['wu'] = _dot(gr['t'].astype(BF16), rhs)
        gr['qe'] = gr['q'] * eg
    for gr in groups:
        sq = gr['sq']
        s_rows = slice(gr['gi'] * gs, (gr['gi'] + 1) * gs)
        s = s_ref[sq, s_rows, :]
        wq = jnp.concatenate([gr['wu'][:, 0:HEAD_DIM], gr['qe']], axis=0)
        wq_bd = jnp.where(same_s2, jnp.concatenate([wq] * g, axis=1), 0.0).astype(BF16)
        ws_qs = _dot(wq_bd, s.astype(BF16))
        u = gr['wu'][:, HEAD_DIM:2 * HEAD_DIM] - ws_qs[0:gc]
        u16 = u.astype(BF16)
        o = ws_qs[gc:2 * gc] + _dot(gr['qk'].astype(BF16), u16)
        kd = gr['k'] * jnp.exp(gr['gl'] - gr['gc'])
        kd_bd = jnp.where(same_s, jnp.concatenate([kd] * g, axis=1), 0.0).astype(BF16)
        s_ref[sq, s_rows, :] = s * jnp.exp(gr['gl_s']) + _dot_tn(kd_bd, u16)
        on = o * lax.rsqrt(jnp.mean(o * o, axis=-1, keepdims=True) + NORM_EPS) * gnorm
        for j, h in enumerate(gr['heads']):
            sl = slice(h * HEAD_DIM, (h + 1) * HEAD_DIM)
            zh = z_ref[sq, :, sl]
            o_ref[sq, :, sl] = on[j * c:(j + 1) * c] * (zh * _sigmoid(zh))


def _gdn_scan(qkv3, gb3, z3, s0, gnorm, c, g, bb=1):
    b, tp, _ = qkv3.shape
    n = tp // c
    kern = functools.partial(_gdn_scan_kernel, c=c, g=g)
    state_rows = N_GDN_HEADS * HEAD_DIM
    state_spec = pl.BlockSpec((bb, state_rows, HEAD_DIM), lambda bi, i: (bi, 0, 0))
    o, s_new = pl.pallas_call(
        kern,
        grid=(b // bb, n),
        in_specs=[pl.BlockSpec((bb, c, 3 * GDN_W), lambda bi, i: (bi, i, 0)),
                  pl.BlockSpec((bb, c, LANES), lambda bi, i: (bi, i, 0)),
                  pl.BlockSpec((bb, c, GDN_W), lambda bi, i: (bi, i, 0)),
                  state_spec,
                  pl.BlockSpec((1, HEAD_DIM), lambda bi, i: (0, 0))],
        out_specs=(pl.BlockSpec((bb, c, GDN_W), lambda bi, i: (bi, i, 0)), state_spec),
        out_shape=(jax.ShapeDtypeStruct((b, tp, GDN_W), F32),
                   jax.ShapeDtypeStruct((b, state_rows, HEAD_DIM), F32)),
        compiler_params=_cparams(("parallel", "arbitrary")),
        name="gdn_scan",
    )(qkv3, gb3, z3, s0.reshape(b, state_rows, HEAD_DIM), gnorm)
    return o, s_new.reshape(b, N_GDN_HEADS, HEAD_DIM, HEAD_DIM)


def _compress(readers, pe_ref, wbd_ref, cc, cs):
    acc = None
    for c in range(CMP_BLOCK):
        rows = jnp.concatenate([rd(parity, c) for rd in readers for parity in range(2)], axis=0)
        part = _dot((rows + pe_ref[c:c + 1, :]).astype(BF16), wbd_ref[c])
        acc = part if acc is None else acc + part
    return _rope128(acc, cc, cs)


def _softmax_rows(s, mask):
    s = jnp.where(mask, s, -jnp.inf)
    m = jnp.max(s, axis=-1, keepdims=True)
    m = jnp.where(m > -jnp.inf, m, 0.0)
    e = jnp.exp(s - m)
    return e / jnp.maximum(jnp.sum(e, axis=-1, keepdims=True), 1e-30)


def _select_blocks(imp, qpos, n_sel, blk_axis):
    blk = lax.broadcasted_iota(jnp.int32, imp.shape, blk_axis)
    cur = qpos >> HEAD_SHIFT
    forced = (blk == 0) | (blk == cur) | (blk == cur - 1)
    valid = (blk * SEL_BLOCK <= qpos) & (blk < n_sel)
    score = jnp.where(forced, jnp.inf, jnp.where(valid, imp, -jnp.inf))
    rank = jnp.zeros(imp.shape, F32)
    for i in range(min(n_sel, N_SCORE)):
        si = score[i:i + 1, :] if blk_axis == 0 else score[:, i:i + 1]
        ahead = (si > score) | ((si == score) & (blk > i))
        rank = rank + jnp.where(ahead, 1.0, 0.0)
    return ((rank < TOP_K) & (blk < n_sel)).astype(BF16)


def _cmp_branch_rows(q_c, ckv16, qpos, qpos_t, n_half, n_sel, tq):
    r = q_c.shape[0]
    nc = 2 * n_half
    col = lax.broadcasted_iota(jnp.int32, (r, nc), 1)
    cmp_end = (2 * (col & (n_half - 1)) + (col >> int(math.log2(n_half))) + 1) * CMP_BLOCK - 1
    p_c = _softmax_rows(_dot_nt(q_c, ckv16), cmp_end <= qpos)
    o_c = _dot(p_c.astype(BF16), ckv16)[:, HEAD_DIM:2 * HEAD_DIM]
    pair = p_c[:, 0:n_half] + p_c[:, n_half:nc]
    imp = pair[0:tq]
    for g in range(1, GQA_GROUP):
        imp = imp + pair[g * tq:(g + 1) * tq]
    if n_half < N_SCORE:
        imp = jnp.concatenate([imp, jnp.zeros((tq, N_SCORE - n_half), F32)], axis=1)
    sel = _select_blocks(imp, qpos_t, n_sel, 1)
    sel_r = jnp.concatenate([sel] * GQA_GROUP, axis=0)
    return o_c, lambda expand: _dot(sel_r, expand)


def _softmax_cols(s, mask):
    s = jnp.where(mask, s, -jnp.inf)
    m = jnp.max(s, axis=0, keepdims=True)
    m = jnp.where(m > -jnp.inf, m, 0.0)
    e = jnp.exp(s - m)
    return e / jnp.maximum(jnp.sum(e, axis=0, keepdims=True), 1e-30)


def _nsa_core_lanes(q_blk, t0, ckv, n_half, n_sel, ks_rows, ks_t, n_full, kc, win_rows, win_t, win_pos0, kvh,
                    gates, sel_ref):
    tq = q_blk.shape[0]
    r = GQA_GROUP * tq
    nc = 2 * n_half
    q_all = jnp.concatenate([q_blk[:, g * HEAD_DIM:(g + 1) * HEAD_DIM] for g in range(GQA_GROUP)], axis=0)
    zeros64 = jnp.zeros((r, HEAD_DIM), F32)
    q_c = jnp.concatenate([q_all, zeros64], axis=1)
    q_c16 = q_c.astype(BF16)
    q_ct = q_c.T.astype(BF16)
    ckv16 = ckv.astype(BF16)
    qpos_l = (lax.broadcasted_iota(jnp.int32, (1, r), 1) & (tq - 1)) + t0

    row = lax.broadcasted_iota(jnp.int32, (nc, r), 0)
    cmp_end = (2 * (row & (n_half - 1)) + (row >> int(math.log2(n_half))) + 1) * CMP_BLOCK - 1
    p_c = _softmax_cols(_dot_nt(ckv16, q_c16), cmp_end <= qpos_l)
    o_ct = _dot_tn(ckv16, p_c.astype(BF16))
    pair = p_c[0:n_half] + p_c[n_half:nc]
    imp = pair[:, 0:tq]
    for g in range(1, GQA_GROUP):
        imp = imp + pair[:, g * tq:(g + 1) * tq]
    if n_half < N_SCORE:
        imp = jnp.concatenate([imp, jnp.zeros((N_SCORE - n_half, tq), F32)], axis=0)
    sel = _select_blocks(imp, qpos_l[:, 0:tq], n_sel, 0).astype(F32)
    sel_ref[...] = jnp.concatenate([sel] * GQA_GROUP, axis=1)

    blocks_per_chunk = kc // SEL_BLOCK
    krow = lax.broadcasted_iota(jnp.int32, (SEL_BLOCK, r), 0)

    def body(c, carry, causal):
        m, l, acc = carry
        s = _dot(ks_rows(c), q_ct)
        sel_c = sel_ref[pl.ds(pl.multiple_of(c * blocks_per_chunk, blocks_per_chunk), blocks_per_chunk), :]
        parts = []
        for j in range(blocks_per_chunk):
            mask = jnp.broadcast_to(sel_c[j:j + 1, :], (SEL_BLOCK, r)) > 0.5
            if causal:
                mask = mask & (krow + (c * kc + j * SEL_BLOCK) <= qpos_l)
            parts.append(jnp.where(mask, s[j * SEL_BLOCK:(j + 1) * SEL_BLOCK], -jnp.inf))
        s = jnp.concatenate(parts, axis=0)
        m_new = jnp.maximum(m, jnp.max(s, axis=0, keepdims=True))
        m_safe = jnp.where(m_new > -jnp.inf, m_new, 0.0)
        alpha = jnp.exp(m - m_safe)
        p = jnp.exp(s - m_safe)
        l = alpha * l + jnp.sum(p, axis=0, keepdims=True)
        acc = alpha * acc + _dot(ks_t(c), p.astype(BF16))
        return m_new, l, acc

    carry = (jnp.full((1, r), -jnp.inf, F32), jnp.zeros((1, r), F32), jnp.zeros((LANES, r), F32))
    carry = lax.fori_loop(0, n_full, lambda c, cr: body(c, cr, False), carry)
    _, l_s, acc_s = body(n_full, carry, True)
    o_st = acc_s / jnp.maximum(l_s, 1e-30)

    lane = lax.broadcasted_iota(jnp.int32, (r, ROW_W), 1)
    q_w = jnp.where((lane >> HEAD_SHIFT) == kvh, jnp.concatenate([q_all, q_all, zeros64, zeros64], axis=1), 0.0)
    nw = win_rows.shape[0]
    kpos = lax.broadcasted_iota(jnp.int32, (nw, r), 0) + win_pos0
    p_w = _softmax_cols(_dot(win_rows, q_w.T.astype(BF16)), (kpos <= qpos_l) & (kpos >= qpos_l - WINDOW))
    o_wt2 = _dot(win_t, p_w.astype(BF16))
    o_wt = jnp.where(kvh == 0, o_wt2[2 * HEAD_DIM:3 * HEAD_DIM], o_wt2[3 * HEAD_DIM:4 * HEAD_DIM])

    gates_t = gates.T
    outs = []
    for g in range(GQA_GROUP):
        cols = slice(g * tq, (g + 1) * tq)
        g0 = _sigmoid(gates_t[3 * g:3 * g + 1, :])
        g1 = _sigmoid(gates_t[3 * g + 1:3 * g + 2, :])
        g2 = _sigmoid(gates_t[3 * g + 2:3 * g + 3, :])
        outs.append(g0 * o_ct[HEAD_DIM:2 * HEAD_DIM, cols] + g1 * o_st[HEAD_DIM:2 * HEAD_DIM, cols]
                    + g2 * o_wt[:, cols])
    return jnp.concatenate(outs, axis=0).T


def _nsa_core_rows(q_blk, t0, ckv, n_half, sel_chunk, n_full, kc, n_sel, win_rows, win_pos0, kvh, gates):
    tq = q_blk.shape[0]
    r = GQA_GROUP * tq
    q_all = jnp.concatenate([q_blk[:, g * HEAD_DIM:(g + 1) * HEAD_DIM] for g in range(GQA_GROUP)], axis=0)
    qpos_t = lax.broadcasted_iota(jnp.int32, (tq, 1), 0) + t0
    qpos = jnp.concatenate([qpos_t] * GQA_GROUP, axis=0)
    zeros64 = jnp.zeros((r, HEAD_DIM), F32)
    q_c = jnp.concatenate([q_all, zeros64], axis=1).astype(BF16)
    ckv16 = ckv.astype(BF16)
    o_c, picked_fn = _cmp_branch_rows(q_c, ckv16, qpos, qpos_t, n_half, n_sel, tq)

    j_i = lax.broadcasted_iota(jnp.int32, (N_SCORE, kc), 0)
    k_i = lax.broadcasted_iota(jnp.int32, (N_SCORE, kc), 1) >> HEAD_SHIFT
    kcol = lax.broadcasted_iota(jnp.int32, (r, kc), 1)

    def body(c, carry, causal):
        m, l, acc = carry
        rows = sel_chunk(c)
        s = _dot_nt(q_c, rows)
        expand = jnp.where(j_i == k_i + c * (kc // SEL_BLOCK), 1.0, 0.0).astype(BF16)
        mask = picked_fn(expand) > 0.5
        if causal:
            mask = mask & (kcol + c * kc <= qpos)
        s = jnp.where(mask, s, -jnp.inf)
        m_new = jnp.maximum(m, jnp.max(s, axis=-1, keepdims=True))
        m_safe = jnp.where(m_new > -jnp.inf, m_new, 0.0)
        alpha = jnp.exp(m - m_safe)
        p = jnp.exp(s - m_safe)
        l = alpha * l + jnp.sum(p, axis=-1, keepdims=True)
        acc = alpha * acc + _dot(p.astype(BF16), rows)
        return m_new, l, acc

    carry = (jnp.full((r, 1), -jnp.inf, F32), jnp.zeros((r, 1), F32), jnp.zeros((r, LANES), F32))
    if not isinstance(n_full, int) or n_full > 0:
        carry = lax.fori_loop(0, n_full, lambda c, cr: body(c, cr, False), carry)
    _, l_s, acc_s = body(n_full, carry, True)
    o_s = acc_s[:, HEAD_DIM:2 * HEAD_DIM] / jnp.maximum(l_s, 1e-30)

    lane = lax.broadcasted_iota(jnp.int32, (r, ROW_W), 1)
    q_w = jnp.where((lane >> HEAD_SHIFT) == kvh, jnp.concatenate([q_all, q_all, zeros64, zeros64], axis=1), 0.0)
    q_w = q_w.astype(BF16)
    nw = win_rows.shape[0]
    kpos = lax.broadcasted_iota(jnp.int32, (r, nw), 1) + win_pos0
    p_w = _softmax_rows(_dot_nt(q_w, win_rows), (kpos <= qpos) & (kpos >= qpos - WINDOW))
    o_w2 = _dot(p_w.astype(BF16), win_rows)
    o_w = jnp.where(kvh == 0, o_w2[:, 2 * HEAD_DIM:3 * HEAD_DIM], o_w2[:, 3 * HEAD_DIM:4 * HEAD_DIM])

    outs = []
    for g in range(GQA_GROUP):
        rows_g = slice(g * tq, (g + 1) * tq)
        g0 = _sigmoid(gates[:, 3 * g:3 * g + 1])
        g1 = _sigmoid(gates[:, 3 * g + 1:3 * g + 2])
        g2 = _sigmoid(gates[:, 3 * g + 2:3 * g + 3])
        outs.append(g0 * o_c[rows_g] + g1 * o_s[rows_g] + g2 * o_w[rows_g])
    return outs


def _nsa_prompt_kernel(q_ref, kc_ref, ks_ref, kst_ref, win_ref, wint_ref, gate_ref, wbd_ref, pe_ref, cc_ref, cs_ref,
                       o_ref, ckv_ref, sel_ref, *, tq, t, kc):
    kvh = pl.program_id(1)
    i = pl.program_id(2)
    n_half = t // (2 * CMP_BLOCK)

    @pl.when(i == 0)
    def _():
        read = lambda parity, c: kc_ref[0, pl.ds(parity * CMP_BLOCK + c, n_half, stride=2 * CMP_BLOCK), :]
        ckv_ref[...] = _compress([read], pe_ref, wbd_ref, cc_ref[...], cs_ref[...])

    t0 = i * tq
    ks_rows = lambda c: ks_ref[0, pl.ds(pl.multiple_of(c * kc, kc), kc), :].astype(BF16)
    ks_t = lambda c: kst_ref[0, c].astype(BF16)
    span = WINDOW + tq
    start = pl.multiple_of(jnp.maximum(t0 - WINDOW, 0), tq)
    win_rows = win_ref[pl.ds(start, span), :].astype(BF16)
    blk0 = start // LANES
    win_t = jnp.concatenate([wint_ref[blk0 + j] for j in range(span // LANES)], axis=1).astype(BF16)
    o_ref[...] = _nsa_core_lanes(q_ref[...], t0, ckv_ref[...], n_half, t // SEL_BLOCK, ks_rows, ks_t, t0 // kc, kc,
                                 win_rows, win_t, start, kvh, gate_ref[...], sel_ref)


def _nsa_prompt(q, kvc, kvs, kst, win, wint, gate, wbd, pe, cc, cs, b, t, tq=4 * LANES):
    n = t // tq
    nc = t // CMP_BLOCK
    kc = kst.shape[3]
    assert tq % LANES == 0 and kc % tq == 0 and t % kc == 0
    kern = functools.partial(_nsa_prompt_kernel, tq=tq, t=t, kc=kc)
    fix2 = lambda bi, h, i: (0, 0)
    return pl.pallas_call(
        kern,
        grid=(b, N_KV_HEADS, n),
        in_specs=[pl.BlockSpec((tq, ROW_W), lambda bi, h, i: (bi * n + i, h)),
                  pl.BlockSpec((1, t, LANES), lambda bi, h, i: (h, bi, 0)),
                  pl.BlockSpec((1, t, LANES), lambda bi, h, i: (h, bi, 0)),
                  pl.BlockSpec((1, t // kc, LANES, kc), lambda bi, h, i: (h, bi, 0, 0)),
                  pl.BlockSpec((t, ROW_W), lambda bi, h, i: (bi, 0)),
                  pl.BlockSpec((t // LANES, ROW_W, LANES), lambda bi, h, i: (bi, 0, 0)),
                  pl.BlockSpec((tq, LANES), lambda bi, h, i: (bi * n + i, h)),
                  pl.BlockSpec((CMP_BLOCK, LANES, LANES), lambda bi, h, i: (0, 0, 0)),
                  pl.BlockSpec((CMP_BLOCK, LANES), fix2),
                  pl.BlockSpec((nc, LANES), fix2),
                  pl.BlockSpec((nc, LANES), fix2)],
        out_specs=pl.BlockSpec((tq, ROW_W), lambda bi, h, i: (bi * n + i, h)),
        out_shape=jax.ShapeDtypeStruct((b * t, NSA_W), F32),
        scratch_shapes=[pltpu.VMEM((nc, LANES), F32), pltpu.VMEM((N_SCORE, GQA_GROUP * tq), F32)],
        compiler_params=_cparams(("parallel", "parallel", "arbitrary")),
        name="nsa_prompt",
    )(q, kvc, kvs, kst, win, wint, gate, wbd, pe, cc, cs)


def _nsa_sample_kernel(pt_ref, *refs, n_steps, pages_per_step, past, tq, ts, n_pad, nw):
    page_refs = refs[:pages_per_step]
    (perm_ref, q_ref, kvnew_ref, wbuf_ref, wnew_ref, gate_ref, wbd_ref, pe_ref, cc_ref, cs_ref,
     o_ref, wout_ref, pge_ref, pgo_ref, pgs_ref, wn_ref) = refs[pages_per_step:]
    j = pl.program_id(1)
    half = perm_ref.shape[0] // 2
    per_parity = half // CMP_BLOCK
    slot = pl.ds(pl.multiple_of(j * per_parity, per_parity), per_parity)
    for kvh in range(N_KV_HEADS):
        for k, page_ref in enumerate(page_refs):
            rows = pl.ds(pl.multiple_of((j * pages_per_step + k) * PAGE_SIZE, PAGE_SIZE), PAGE_SIZE)
            pgs_ref[kvh, rows, :] = page_ref[0, 0, kvh, LANES:ROW_W, :].T
        cmp_t = jnp.concatenate([page_ref[0, 0, kvh, 0:LANES, :] for page_ref in page_refs], axis=1)
        grouped = _dot_nt(perm_ref[...], cmp_t.astype(BF16))
        for c in range(CMP_BLOCK):
            pge_ref[kvh, c, slot, :] = grouped[c * per_parity:(c + 1) * per_parity]
            pgo_ref[kvh, c, slot, :] = grouped[half + c * per_parity:half + (c + 1) * per_parity]

    @pl.when(j == n_steps - 1)
    def _():
        n_half = past // (2 * CMP_BLOCK)
        wbuf_t = wbuf_ref[0, 0]
        wnew = wnew_ref[0]
        wn_ref[0:WINDOW, :] = wbuf_t.T
        wn_ref[WINDOW:WINDOW + tq, :] = wnew
        wn_ref[WINDOW + tq:nw, :] = jnp.zeros((nw - WINDOW - tq, ROW_W), F32)
        win_rows = wn_ref[...].astype(BF16)

        shifted = pltpu.roll(wbuf_t, WINDOW - ts, 1)
        new_t = jnp.concatenate([wnew, jnp.zeros((LANES - tq, ROW_W), F32)], axis=0).T
        lane = lax.broadcasted_iota(jnp.int32, (ROW_W, LANES), 1)
        tail = jnp.where(lane >= LANES - ts, pltpu.roll(new_t, LANES - ts, 1), shifted[:, WINDOW - LANES:WINDOW])
        wout_ref[0, :, 0:WINDOW - LANES] = shifted[:, 0:WINDOW - LANES]
        wout_ref[0, :, WINDOW - LANES:WINDOW] = tail

        readers = []
        for kvh in range(N_KV_HEADS):
            pgs_ref[kvh, past:past + tq, :] = kvnew_ref[kvh, 0]
            pgs_ref[kvh, past + tq:n_pad, :] = jnp.zeros((n_pad - past - tq, LANES), F32)
            readers.append(lambda parity, c, kvh=kvh: (pgo_ref if parity else pge_ref)[kvh, c])
        ckv2 = _compress(readers, pe_ref, wbd_ref, cc_ref[...], cs_ref[...])
        for kvh in range(N_KV_HEADS):
            ckv = ckv2[kvh * 2 * n_half:(kvh + 1) * 2 * n_half]
            sel_chunk = lambda c, kvh=kvh: pgs_ref[kvh].astype(BF16)
            outs = _nsa_core_rows(q_ref[0, :, kvh * ROW_W:(kvh + 1) * ROW_W], past, ckv, n_half, sel_chunk, 0,
                             n_pad, past // SEL_BLOCK + 1, win_rows, past - WINDOW, kvh,
                             gate_ref[0, :, kvh * LANES:(kvh + 1) * LANES])
            for g in range(GQA_GROUP):
                col = kvh * ROW_W + g * HEAD_DIM
                o_ref[0, :, col:col + HEAD_DIM] = outs[g]


def _page_group_permutation(pages_per_step):
    n = pages_per_step * PAGE_SIZE
    half = n // 2
    per_parity = half // CMP_BLOCK
    r = np.arange(n)
    parity, c, i = r // half, (r % half) // per_parity, r % per_parity
    perm = np.zeros((n, n), np.float32)
    perm[r, (2 * i + parity) * CMP_BLOCK + c] = 1.0
    return jnp.asarray(perm, dtype=BF16)


def _nsa_sample(page_table, pool_t, layer, ts, q8, kvnew8, win_t, wnew8, gate8, wbd, pe, cc, cs, pages_per_step=4):
    b, n_pages = page_table.shape
    past = n_pages * PAGE_SIZE
    tq = q8.shape[1]
    n_pad = past + SEL_BLOCK
    nw = WINDOW + LANES
    nc2 = N_KV_HEADS * (past // CMP_BLOCK)
    n_steps = n_pages // pages_per_step
    n_half = past // (2 * CMP_BLOCK)
    perm = _page_group_permutation(pages_per_step)
    assert (pages_per_step * PAGE_SIZE // (2 * CMP_BLOCK)) % SUBLANES == 0
    kern = functools.partial(_nsa_sample_kernel, n_steps=n_steps, pages_per_step=pages_per_step, past=past,
                             tq=tq, ts=ts, n_pad=n_pad, nw=nw)
    per_b = lambda bi, j, pt: (bi, 0, 0)
    fix2 = lambda bi, j, pt: (0, 0)
    page_spec = lambda k: pl.BlockSpec((1, 1, N_KV_HEADS, ROW_W, PAGE_SIZE),
                                       lambda bi, j, pt: (layer, pt[bi, j * pages_per_step + k], 0, 0, 0))
    grid_spec = pltpu.PrefetchScalarGridSpec(
        num_scalar_prefetch=1,
        grid=(b, n_steps),
        in_specs=[page_spec(k) for k in range(pages_per_step)] + [
                  pl.BlockSpec(perm.shape, fix2),
                  pl.BlockSpec((1, tq, NSA_W), per_b),
                  pl.BlockSpec((N_KV_HEADS, 1, tq, LANES), lambda bi, j, pt: (0, bi, 0, 0)),
                  pl.BlockSpec((1, 1, ROW_W, WINDOW), lambda bi, j, pt: (layer, bi, 0, 0)),
                  pl.BlockSpec((1, tq, ROW_W), per_b),
                  pl.BlockSpec((1, tq, N_KV_HEADS * LANES), per_b),
                  pl.BlockSpec((CMP_BLOCK, LANES, LANES), lambda bi, j, pt: (0, 0, 0)),
                  pl.BlockSpec((CMP_BLOCK, LANES), fix2),
                  pl.BlockSpec((nc2, LANES), fix2),
                  pl.BlockSpec((nc2, LANES), fix2)],
        out_specs=(pl.BlockSpec((1, tq, NSA_W), per_b), pl.BlockSpec((1, ROW_W, WINDOW), per_b)),
        scratch_shapes=[pltpu.VMEM((N_KV_HEADS, CMP_BLOCK, n_half, LANES), F32),
                        pltpu.VMEM((N_KV_HEADS, CMP_BLOCK, n_half, LANES), F32),
                        pltpu.VMEM((N_KV_HEADS, n_pad, LANES), F32), pltpu.VMEM((nw, ROW_W), F32)],
    )
    return pl.pallas_call(
        kern,
        grid_spec=grid_spec,
        out_shape=(jax.ShapeDtypeStruct((b, tq, NSA_W), F32), jax.ShapeDtypeStruct((b, ROW_W, WINDOW), F32)),
        compiler_params=_cparams(("parallel", "arbitrary")),
        name="nsa_sample",
    )(page_table, *([pool_t] * pages_per_step), perm, q8, kvnew8, win_t, wnew8, gate8, wbd, pe, cc, cs)


def _mlp_kernel(x_ref, og_ref, on_ref, wo_ref, gm_ref, wu_ref, wd_ref, gf_ref, y_ref, x1_ref, h_ref, acc_ref,
                *, final_norm):
    f = pl.program_id(1)

    @pl.when(f == 0)
    def _():
        x1 = (x_ref[...] + _dot(og_ref[...].astype(BF16), wo_ref[0:GDN_W, :])
              + _dot(on_ref[...].astype(BF16), wo_ref[GDN_W:GDN_W + NSA_W, :]))
        x1_ref[...] = x1
        var = jnp.mean(x1 * x1, axis=-1, keepdims=True)
        h_ref[...] = (x1 * lax.rsqrt(var + NORM_EPS) * gm_ref[...]).astype(BF16)
        acc_ref[...] = jnp.zeros(acc_ref.shape, F32)

    up = jnp.maximum(_dot(h_ref[...], wu_ref[...]), 0.0)
    acc_ref[...] += _dot((up * up).astype(BF16), wd_ref[...])

    @pl.when(f == pl.num_programs(1) - 1)
    def _():
        y = x1_ref[...] + acc_ref[...]
        if final_norm:
            var = jnp.mean(y * y, axis=-1, keepdims=True)
            y = y * lax.rsqrt(var + NORM_EPS) * gf_ref[...]
        y_ref[...] = y


def _mlp(x, og, on, wo, gm, wu, wd, gf, final_norm, tm, tf=D_FF):
    m = x.shape[0]
    kern = functools.partial(_mlp_kernel, final_norm=final_norm)
    row = lambda i, f: (i, 0)
    fix = lambda i, f: (0, 0)
    return pl.pallas_call(
        kern,
        grid=(m // tm, D_FF // tf),
        in_specs=[pl.BlockSpec((tm, D_MODEL), row), pl.BlockSpec((tm, GDN_W), row), pl.BlockSpec((tm, NSA_W), row),
                  pl.BlockSpec((D_MODEL, D_MODEL), fix), pl.BlockSpec((1, D_MODEL), fix),
                  pl.BlockSpec((D_MODEL, tf), lambda i, f: (0, f)), pl.BlockSpec((tf, D_MODEL), lambda i, f: (f, 0)),
                  pl.BlockSpec((1, D_MODEL), fix)],
        out_specs=pl.BlockSpec((tm, D_MODEL), row),
        out_shape=jax.ShapeDtypeStruct((m, D_MODEL), F32),
        scratch_shapes=[pltpu.VMEM((tm, D_MODEL), F32), pltpu.VMEM((tm, D_MODEL), BF16),
                        pltpu.VMEM((tm, D_MODEL), F32)],
        compiler_params=_cparams(("parallel", "arbitrary")),
        name="mlp",
    )(x, og, on, wo, gm, wu, wd, gf)


def _proj_columns():
    src = np.full((N_PROJ,), -1, np.int64)
    o_z, o_a, o_b = 3 * GDN_W, 4 * GDN_W, 4 * GDN_W + N_GDN_HEADS
    o_q = o_b + N_GDN_HEADS
    o_kv = o_q + NSA_W
    o_g = o_kv + 6 * KV_W
    src[C_QKV:C_Z] = np.arange(0, 3 * GDN_W)
    src[C_Z:C_Q] = np.arange(o_z, o_z + GDN_W)
    src[C_Q:C_KV] = np.arange(o_q, o_q + NSA_W)
    for kvh in range(N_KV_HEADS):
        for s in range(4):
            dst = C_KV + kvh * ROW_W + s * HEAD_DIM
            src[dst:dst + HEAD_DIM] = o_kv + s * KV_W + kvh * HEAD_DIM + np.arange(HEAD_DIM)
    src[C_WIN:C_WIN + 2 * KV_W] = o_kv + 4 * KV_W + np.arange(2 * KV_W)
    src[C_AB:C_AB + N_GDN_HEADS] = o_a + np.arange(N_GDN_HEADS)
    src[C_AB + N_GDN_HEADS:C_AB + 2 * N_GDN_HEADS] = o_b + np.arange(N_GDN_HEADS)
    for kvh in range(N_KV_HEADS):
        dst = C_GATE + kvh * LANES
        src[dst:dst + 3 * GQA_GROUP] = o_g + kvh * 3 * GQA_GROUP + np.arange(3 * GQA_GROUP)
    return src


def _rope_tables(pos):
    half = HEAD_DIM // 2
    inv_freq = ROPE_THETA ** (-jnp.arange(half, dtype=F32) / half)
    ang = pos.astype(F32)[:, None] * inv_freq[None, :]
    cos, sin = jnp.cos(ang), jnp.sin(ang)
    c64 = jnp.concatenate([cos, cos], axis=1)
    s64 = jnp.concatenate([-sin, sin], axis=1)
    one, zero = jnp.ones_like(c64), jnp.zeros_like(s64)
    return (jnp.concatenate([c64, c64], axis=1), jnp.concatenate([s64, s64], axis=1),
            jnp.concatenate([c64, one], axis=1), jnp.concatenate([s64, zero], axis=1))


def _cmp_tables(n_blocks):
    n_half = n_blocks // 2
    r = np.arange(n_blocks)
    blk = 2 * (r % n_half) + r // n_half
    end_pos = jnp.asarray((blk + 1) * CMP_BLOCK - 1)
    _, _, cc, cs = _rope_tables(end_pos)
    return cc, cs


def _layer_params(l, norm_mix, w_in, conv_w, a_log, dt_bias, gdn_norm, cmp_pe_k, cmp_w_k, cmp_pe_v, cmp_w_v,
                  w_out, norm_mlp, w_up, w_down):
    src = _proj_columns()
    w_ext = jnp.concatenate([w_in[l], jnp.zeros((D_MODEL, 1), F32)], axis=1)
    w_p = jnp.take(w_ext, jnp.asarray(np.where(src < 0, w_in.shape[2], src)), axis=1).astype(BF16)
    zeros = jnp.zeros((CMP_BLOCK, HEAD_DIM, HEAD_DIM), F32)
    wbd = jnp.concatenate([jnp.concatenate([cmp_w_k[l], zeros], axis=2),
                           jnp.concatenate([zeros, cmp_w_v[l]], axis=2)], axis=1).astype(BF16)
    pad_row = lambda v: jnp.pad(v.astype(F32), (0, LANES - v.shape[0]))[None, :]
    return dict(
        norm_mix=norm_mix[l][None, :], w_p=w_p, conv_w=conv_w[l], alog=pad_row(a_log[l]), dtb=pad_row(dt_bias[l]),
        gdn_norm=gdn_norm[l][None, :], wbd=wbd, pe=jnp.concatenate([cmp_pe_k[l], cmp_pe_v[l]], axis=1),
        w_out=w_out[l].astype(BF16), norm_mlp=norm_mlp[l][None, :], w_up=w_up[l].astype(BF16),
        w_down=w_down[l].astype(BF16))


def _prompt_layer(x, lp, tabs, cmp_tabs, gf, final_norm, b, t):
    qkv_pre, z, q, kvc, kvs, win, ab, gate, kst, wint = _proj(x, lp['norm_mix'], lp['w_p'], tabs, PROJ_ROWS)
    w3 = 3 * GDN_W
    cb8 = jnp.zeros((b, SUBLANES, w3), F32)
    qkv3, gb3 = _gdn_prep(qkv_pre.reshape(b, t, w3), cb8, ab.reshape(b, t, LANES), lp['conv_w'], lp['alog'],
                          lp['dtb'], 256, t, 256)
    s0 = jnp.zeros((b, N_GDN_HEADS, HEAD_DIM, HEAD_DIM), F32)
    o_gdn, s_new = _gdn_scan(qkv3, gb3, z.reshape(b, t, GDN_W), s0, lp['gdn_norm'], GDN_CHUNK, 4)
    o_nsa = _nsa_prompt(q, kvc, kvs, kst, win, wint, gate, lp['wbd'], lp['pe'], cmp_tabs[0], cmp_tabs[1], b, t)
    y = _mlp(x, o_gdn.reshape(b * t, GDN_W), o_nsa, lp['w_out'], lp['norm_mlp'], lp['w_up'], lp['w_down'], gf,
             final_norm, 512)
    rows = jnp.concatenate([kvc, kvs], axis=2).reshape(N_KV_HEADS, b, t, 4, HEAD_DIM).transpose(1, 0, 2, 3, 4)
    win_new = win.reshape(b, t, 2, N_KV_HEADS, HEAD_DIM)[:, t - min(WINDOW, t):]
    conv_new = qkv_pre.reshape(b, t, w3)[:, t - (CONV_W - 1):]
    return y, rows, win_new, s_new, conv_new


def _pad_rows(a, n):
    return jnp.pad(a, ((0, 0), (0, n - a.shape[1]), (0, 0)))


def _sample_layer(x, lp, tabs, cmp_tabs, gf, final_norm, b, t, layer, pool_t, page_table, win_t, s0, conv_buf):
    m = b * t
    qkv_pre, z, q, kvc, kvs, win, ab, gate, _, _ = _proj(x, lp['norm_mix'], lp['w_p'], tabs, PROJ_ROWS)
    w3 = 3 * GDN_W
    tp = SUBLANES
    cb8 = jnp.pad(conv_buf, ((0, 0), (SUBLANES - (CONV_W - 1), 0), (0, 0)))
    qkv3, gb3 = _gdn_prep(_pad_rows(qkv_pre.reshape(b, t, w3), tp), cb8, _pad_rows(ab.reshape(b, t, LANES), tp),
                          lp['conv_w'], lp['alog'], lp['dtb'], tp, t, tp, bb=SAMPLE_SEQS_PER_STEP)
    o_gdn, s_new = _gdn_scan(qkv3, gb3, _pad_rows(z.reshape(b, t, GDN_W), tp), s0, lp['gdn_norm'], tp,
                             N_GDN_HEADS, bb=SAMPLE_SEQS_PER_STEP // 2)
    o_gdn = o_gdn[:, :t].reshape(m, GDN_W)
    kvnew8 = jnp.pad(kvs.reshape(N_KV_HEADS, b, t, LANES), ((0, 0), (0, 0), (0, tp - t), (0, 0)))
    o_nsa, win_out_t = _nsa_sample(page_table, pool_t, layer, t, _pad_rows(q.reshape(b, t, NSA_W), tp), kvnew8,
                                   win_t, _pad_rows(win.reshape(b, t, ROW_W), tp),
                                   _pad_rows(gate.reshape(b, t, N_KV_HEADS * LANES), tp), lp['wbd'], lp['pe'],
                                   cmp_tabs[0], cmp_tabs[1])
    o_nsa = o_nsa[:, :t].reshape(m, NSA_W)
    y = _mlp(x, o_gdn, o_nsa, lp['w_out'], lp['norm_mlp'], lp['w_up'], lp['w_down'], gf, final_norm, 512)
    rows = jnp.concatenate([kvc, kvs], axis=2).reshape(N_KV_HEADS, b, t, 4, HEAD_DIM).transpose(1, 0, 2, 3, 4)
    win_new = win_out_t.reshape(b, 2, N_KV_HEADS, HEAD_DIM, WINDOW).transpose(0, 4, 1, 2, 3)
    conv_new = jnp.concatenate([conv_buf, qkv_pre.reshape(b, t, w3)], axis=1)[:, t:]
    return y, rows, win_new, s_new, conv_new


def kernel(x_prompt, x_sample, cache_kv, page_table, state_win, state_gdn, state_conv, norm_mix, w_in, conv_w, a_log, dt_bias, gdn_norm, cmp_pe_k, cmp_w_k, cmp_pe_v, cmp_w_v, w_out, norm_mlp, w_up, w_down, norm_final):
    bp, tp_, _ = x_prompt.shape
    bs, ts, _ = x_sample.shape
    depth = cache_kv.shape[0]
    n_pages = page_table.shape[1]
    past = n_pages * PAGE_SIZE
    assert state_win.shape[2] == WINDOW and tp_ % 512 == 0 and ts <= SUBLANES and past % (2 * CMP_BLOCK) == 0

    tabs_p = _rope_tables(jnp.arange(tp_))
    tabs_s = tuple(jnp.tile(tb, (bs, 1)) for tb in _rope_tables(past + jnp.arange(ts)))
    cmp_p = _cmp_tables(tp_ // CMP_BLOCK)
    cmp_s = tuple(jnp.tile(tb, (N_KV_HEADS, 1)) for tb in _cmp_tables((past + ts) // CMP_BLOCK))
    pool_t = cache_kv.transpose(0, 1, 2, 4, 5, 3).reshape(depth, cache_kv.shape[1], N_KV_HEADS, ROW_W, PAGE_SIZE)
    win_t = state_win.transpose(0, 1, 3, 4, 5, 2).reshape(depth, bs, ROW_W, WINDOW)
    gf = norm_final[None, :]

    xp = x_prompt.reshape(bp * tp_, D_MODEL)
    xs = x_sample.reshape(bs * ts, D_MODEL)
    outs = [[] for _ in range(8)]
    for l in range(depth):
        lp = _layer_params(l, norm_mix, w_in, conv_w, a_log, dt_bias, gdn_norm, cmp_pe_k, cmp_w_k, cmp_pe_v,
                           cmp_w_v, w_out, norm_mlp, w_up, w_down)
        last = l == depth - 1
        xp, r, w, s, c = _prompt_layer(xp, lp, tabs_p, cmp_p, gf, last, bp, tp_)
        for k, v in zip((0, 2, 4, 6), (r, w, s, c)):
            outs[k].append(v)
        xs, r, w, s, c = _sample_layer(xs, lp, tabs_s, cmp_s, gf, last, bs, ts, l, pool_t, page_table,
                                       win_t, state_gdn[l], state_conv[l])
        for k, v in zip((1, 3, 5, 7), (r, w, s, c)):
            outs[k].append(v)
    return (xp.reshape(bp, tp_, D_MODEL), xs.reshape(bs, ts, D_MODEL)) + tuple(jnp.stack(o) for o in outs)
```

```python
import functools
import math

import numpy as np
import jax
import jax.numpy as jnp
from jax import lax
from jax.experimental import pallas as pl
from jax.experimental.pallas import tpu as pltpu

F32 = jnp.float32
BF16 = jnp.bfloat16
HIGHEST = lax.Precision.HIGHEST

D_MODEL = 1024
HEAD_DIM = 64
N_GDN_HEADS = 8
GDN_W = N_GDN_HEADS * HEAD_DIM
N_NSA_HEADS = 8
NSA_W = N_NSA_HEADS * HEAD_DIM
N_KV_HEADS = 2
GQA_GROUP = N_NSA_HEADS // N_KV_HEADS
KV_W = N_KV_HEADS * HEAD_DIM
CONV_W = 4
GDN_CHUNK = 64
CMP_BLOCK = 32
SEL_BLOCK = 64
TOP_K = 16
WINDOW = 512
D_FF = 4 * D_MODEL
ROPE_THETA = 10000.0
NORM_EPS = 1e-6
PAGE_SIZE = 128
N_SCORE = 64
HEAD_SHIFT = 6
INV_BASE_SHIFT = 4
QK_SCALE = HEAD_DIM ** -0.5

LANES = 128
SUBLANES = 8
VMEM_LIMIT = 56 * 1024 * 1024

C_QKV = 0
C_Z = 3 * GDN_W
C_Q = C_Z + GDN_W
C_KV = C_Q + NSA_W
C_WIN = C_KV + N_KV_HEADS * 4 * HEAD_DIM
C_AB = C_WIN + 2 * KV_W
C_GATE = C_AB + LANES
N_PROJ = C_GATE + N_KV_HEADS * LANES
ROW_W = 4 * HEAD_DIM
PROJ_ROWS = 512
SAMPLE_SEQS_PER_STEP = 8


def _cparams(sem):
    return pltpu.CompilerParams(dimension_semantics=sem, vmem_limit_bytes=VMEM_LIMIT)


def _sigmoid(x):
    return 1.0 / (1.0 + jnp.exp(-x))


def _dot(a, b):
    return jnp.dot(a, b, preferred_element_type=F32)


def _dot_nt(a, b):
    return lax.dot_general(a, b, (((1,), (1,)), ((), ())), preferred_element_type=F32)


def _dot_tn(a, b, precision=None):
    return lax.dot_general(a, b, (((0,), (0,)), ((), ())), preferred_element_type=F32, precision=precision)


def _rope128(v, cos, sin):
    lane = lax.broadcasted_iota(jnp.int32, v.shape, 1)
    first = (lane & (HEAD_DIM - 1)) < (HEAD_DIM // 2)
    swapped = jnp.where(first, pltpu.roll(v, LANES - HEAD_DIM // 2, 1), pltpu.roll(v, HEAD_DIM // 2, 1))
    return v * cos + swapped * sin


def _proj_kernel(x_ref, g_ref, w_ref, cf_ref, sf_ref, ch_ref, sh_ref,
                 qkv_ref, z_ref, q_ref, kc_ref, ks_ref, win_ref, ab_ref, gate_ref, kst_ref, wint_ref):
    x = x_ref[...]
    var = jnp.mean(x * x, axis=-1, keepdims=True)
    h = (x * lax.rsqrt(var + NORM_EPS) * g_ref[...]).astype(BF16)
    p = _dot(h, w_ref[...])
    qkv_ref[...] = p[:, C_QKV:C_Z]
    z_ref[...] = p[:, C_Z:C_Q]
    cf, sf, ch, sh = cf_ref[...], sf_ref[...], ch_ref[...], sh_ref[...]
    for j in range(NSA_W // LANES):
        q_ref[:, j * LANES:(j + 1) * LANES] = _rope128(p[:, C_Q + j * LANES:C_Q + (j + 1) * LANES], cf, sf) * QK_SCALE
    for kvh in range(N_KV_HEADS):
        base = C_KV + kvh * ROW_W
        kc_ref[kvh] = p[:, base:base + LANES]
        ks = _rope128(p[:, base + LANES:base + ROW_W], ch, sh)
        ks_ref[kvh] = ks
        kst_ref[kvh, 0] = ks.T
    wk = _rope128(p[:, C_WIN:C_WIN + LANES], cf, sf)
    wv = p[:, C_WIN + LANES:C_WIN + ROW_W]
    win_ref[:, 0:LANES] = wk
    win_ref[:, LANES:ROW_W] = wv
    for j in range(x.shape[0] // LANES):
        rows = slice(j * LANES, (j + 1) * LANES)
        wint_ref[j, 0:LANES, :] = wk[rows].T
        wint_ref[j, LANES:ROW_W, :] = wv[rows].T
    ab_ref[...] = p[:, C_AB:C_AB + LANES]
    gate_ref[...] = p[:, C_GATE:C_GATE + N_KV_HEADS * LANES]


def _proj(x, g, w, tabs, tm):
    m = x.shape[0]
    nt = tabs[0].shape[0] // tm
    row = lambda i: (i, 0)
    tab = lambda i: (i % nt, 0)
    fix = lambda i: (0, 0)
    out_shapes = (
        jax.ShapeDtypeStruct((m, 3 * GDN_W), F32),
        jax.ShapeDtypeStruct((m, GDN_W), F32),
        jax.ShapeDtypeStruct((m, NSA_W), F32),
        jax.ShapeDtypeStruct((N_KV_HEADS, m, LANES), F32),
        jax.ShapeDtypeStruct((N_KV_HEADS, m, LANES), F32),
        jax.ShapeDtypeStruct((m, ROW_W), F32),
        jax.ShapeDtypeStruct((m, LANES), F32),
        jax.ShapeDtypeStruct((m, N_KV_HEADS * LANES), F32),
        jax.ShapeDtypeStruct((N_KV_HEADS, m // tm, LANES, tm), F32),
        jax.ShapeDtypeStruct((m // LANES, ROW_W, LANES), F32),
    )
    return pl.pallas_call(
        _proj_kernel,
        grid=(m // tm,),
        in_specs=[pl.BlockSpec((tm, D_MODEL), row), pl.BlockSpec((1, D_MODEL), fix),
                  pl.BlockSpec((D_MODEL, N_PROJ), fix)] + [pl.BlockSpec((tm, LANES), tab)] * 4,
        out_specs=(pl.BlockSpec((tm, 3 * GDN_W), row), pl.BlockSpec((tm, GDN_W), row),
                   pl.BlockSpec((tm, NSA_W), row), pl.BlockSpec((N_KV_HEADS, tm, LANES), lambda i: (0, i, 0)),
                   pl.BlockSpec((N_KV_HEADS, tm, LANES), lambda i: (0, i, 0)),
                   pl.BlockSpec((tm, ROW_W), row), pl.BlockSpec((tm, LANES), row),
                   pl.BlockSpec((tm, N_KV_HEADS * LANES), row),
                   pl.BlockSpec((N_KV_HEADS, 1, LANES, tm), lambda i: (0, i, 0, 0)),
                   pl.BlockSpec((tm // LANES, ROW_W, LANES), lambda i: (i, 0, 0))),
        out_shape=out_shapes,
        compiler_params=_cparams(("parallel",)),
        name="proj",
    )(x, g, w, *tabs)


def _gdn_prep_kernel(x_ref, prev_ref, cb_ref, ab_ref, cw_ref, alog_ref, dtb_ref,
                     qkv_ref, gb_ref, xs_ref, *, tt, t_valid, t_out):
    for s in range(x_ref.shape[0]):
        _gdn_prep_one(x_ref.at[s], prev_ref.at[s], cb_ref.at[s], ab_ref.at[s], cw_ref, alog_ref, dtb_ref,
                      qkv_ref.at[s], gb_ref.at[s], xs_ref, tt=tt, t_valid=t_valid, t_out=t_out)


def _gdn_prep_one(x_ref, prev_ref, cb_ref, ab_ref, cw_ref, alog_ref, dtb_ref, qkv_ref, gb_ref, xs_ref,
                  *, tt, t_valid, t_out):
    i = pl.program_id(1)
    xs_ref[SUBLANES:SUBLANES + tt, :] = x_ref[...]

    @pl.when(i == 0)
    def _():
        xs_ref[0:SUBLANES, :] = cb_ref[...]

    @pl.when(i > 0)
    def _():
        xs_ref[0:SUBLANES, :] = prev_ref[...]

    conv = xs_ref[SUBLANES:SUBLANES + tt, :] * cw_ref[CONV_W - 1:CONV_W, :]
    for k in range(1, CONV_W):
        conv = conv + xs_ref[SUBLANES - k:SUBLANES - k + tt, :] * cw_ref[CONV_W - 1 - k:CONV_W - k, :]
    c = conv * _sigmoid(conv)

    rows = lax.broadcasted_iota(jnp.int32, (tt, LANES), 0) + i * tt
    live = rows < t_valid
    r_i = lax.broadcasted_iota(jnp.int32, (LANES, LANES), 0)
    c_i = lax.broadcasted_iota(jnp.int32, (LANES, LANES), 1)
    head_ones = ((r_i >> HEAD_SHIFT) == (c_i >> HEAD_SHIFT)).astype(BF16)

    if t_out != tt:
        qkv_ref[...] = jnp.zeros(qkv_ref.shape, F32)
        gb_ref[...] = jnp.zeros(gb_ref.shape, F32)
    for j in range(3 * GDN_W // LANES):
        blk = c[:, j * LANES:(j + 1) * LANES]
        if j < 2 * GDN_W // LANES:
            sq = blk * blk
            hi = sq.astype(BF16)
            lo = (sq - hi.astype(F32)).astype(BF16)
            ss = _dot(hi, head_ones) + _dot(lo, head_ones)
            blk = blk * lax.rsqrt(ss + NORM_EPS)
            if j < GDN_W // LANES:
                blk = blk * QK_SCALE
        qkv_ref[0:tt, j * LANES:(j + 1) * LANES] = jnp.where(live, blk, 0.0)

    ab = ab_ref[...]
    za = ab + dtb_ref[...]
    softplus = jnp.maximum(za, 0.0) + jnp.log(1.0 + jnp.exp(-jnp.abs(za)))
    gdec = -jnp.exp(alog_ref[...]) * softplus
    lane = lax.broadcasted_iota(jnp.int32, (tt, LANES), 1)
    gb = jnp.where(lane < N_GDN_HEADS, gdec, jnp.where(lane < 2 * N_GDN_HEADS, _sigmoid(ab), 0.0))
    gb_ref[0:tt, :] = jnp.where(live, gb, 0.0)


def _gdn_prep(x3, cb8, ab3, cw, alog_row, dtb_row, tt, t_valid, t_out, bb=1):
    b, t_in, _ = x3.shape
    n = t_in // tt
    blocks8 = tt // SUBLANES
    w3 = 3 * GDN_W
    kern = functools.partial(_gdn_prep_kernel, tt=tt, t_valid=t_valid, t_out=t_out)
    return pl.pallas_call(
        kern,
        grid=(b // bb, n),
        in_specs=[pl.BlockSpec((bb, tt, w3), lambda bi, i: (bi, i, 0)),
                  pl.BlockSpec((bb, SUBLANES, w3), lambda bi, i: (bi, jnp.maximum(i * blocks8 - 1, 0), 0)),
                  pl.BlockSpec((bb, SUBLANES, w3), lambda bi, i: (bi, 0, 0)),
                  pl.BlockSpec((bb, tt, LANES), lambda bi, i: (bi, i, 0)),
                  pl.BlockSpec((CONV_W, w3), lambda bi, i: (0, 0)),
                  pl.BlockSpec((1, LANES), lambda bi, i: (0, 0)),
                  pl.BlockSpec((1, LANES), lambda bi, i: (0, 0))],
        out_specs=(pl.BlockSpec((bb, t_out, w3), lambda bi, i: (bi, i, 0)),
                   pl.BlockSpec((bb, t_out, LANES), lambda bi, i: (bi, i, 0))),
        out_shape=(jax.ShapeDtypeStruct((b, n * t_out, w3), F32),
                   jax.ShapeDtypeStruct((b, n * t_out, LANES), F32)),
        scratch_shapes=[pltpu.VMEM((tt + SUBLANES, w3), F32)],
        compiler_params=_cparams(("parallel", "arbitrary")),
        name="gdn_prep",
    )(x3, x3, cb8, ab3, cw, alog_row, dtb_row)


def _gdn_scan_kernel(qkv_ref, gb_ref, z_ref, s0_ref, gn_ref, o_ref, s_ref, *, c, g):
    ci = pl.program_id(1)
    gc = g * c
    gs = g * HEAD_DIM
    n_groups = N_GDN_HEADS // g
    c_shift = int(math.log2(c))

    @pl.when(ci == 0)
    def _():
        s_ref[...] = s0_ref[...]

    def iota(shape, dim):
        return lax.broadcasted_iota(jnp.int32, shape, dim)

    r_i, c_i = iota((gc, gc), 0), iota((gc, gc), 1)
    same = (r_i >> c_shift) == (c_i >> c_shift)
    incl = same & (r_i >= c_i)
    strict = same & (r_i > c_i)
    eye = jnp.where(r_i == c_i, 1.0, 0.0)
    base_shift = min(INV_BASE_SHIFT, c_shift)
    base_blk = (r_i >> base_shift) == (c_i >> base_shift)
    same_s = (iota((gc, gs), 0) >> c_shift) == (iota((gc, gs), 1) >> HEAD_SHIFT)
    same_s2 = jnp.concatenate([same_s, same_s], axis=0)
    gnorm = gn_ref[...]

    def stack(pieces):
        return jnp.concatenate(pieces, axis=0)

    groups = []
    for sq, gi in [(sq, gi) for sq in range(qkv_ref.shape[0]) for gi in range(n_groups)]:
        gb = gb_ref[sq]
        gcum = jnp.dot(jnp.where(iota((c, c), 0) >= iota((c, c), 1), 1.0, 0.0), gb,
                       preferred_element_type=F32, precision=HIGHEST)
        heads = range(gi * g, (gi + 1) * g)
        lane = lambda base, h: slice(base + h * HEAD_DIM, base + (h + 1) * HEAD_DIM)
        q_st = stack([qkv_ref[sq, :, lane(0, h)] for h in heads])
        k_st = stack([qkv_ref[sq, :, lane(GDN_W, h)] for h in heads])
        v_st = stack([qkv_ref[sq, :, lane(2 * GDN_W, h)] for h in heads])
        gc_st = stack([gcum[:, h:h + 1] for h in heads])
        bt_st = stack([gb[:, N_GDN_HEADS + h:N_GDN_HEADS + h + 1] for h in heads])
        gl_st = stack([jnp.broadcast_to(gcum[c - 1:c, h:h + 1], (c, 1)) for h in heads])
        gl_s = stack([jnp.broadcast_to(gcum[c - 1:c, h:h + 1], (HEAD_DIM, 1)) for h in heads])
        gr_st = jnp.broadcast_to(gc_st, (gc, LANES)).T[0:1, :]
        decay = jnp.exp(jnp.where(incl, gc_st - gr_st, -jnp.inf))
        k16 = k_st.astype(BF16)
        qk_kk = _dot_nt(jnp.concatenate([q_st, k_st], axis=0).astype(BF16), k16)
        a_mat = jnp.where(strict, bt_st * qk_kk[gc:2 * gc] * decay, 0.0)
        a_base = jnp.where(base_blk, a_mat, 0.0)
        groups.append(dict(q=q_st, k=k_st, v=v_st, gc=gc_st, bt=bt_st, gl=gl_st, gl_s=gl_s,
                           qk=qk_kk[0:gc] * decay, a=a_mat, t=eye - a_base, p=a_base, heads=heads, sq=sq, gi=gi))

    for _ in range(base_shift - 1):
        for gr in groups:
            p16 = gr['p'].astype(BF16)
            gr['p'] = _dot(p16, p16)
        for gr in groups:
            gr['t'] = gr['t'] + _dot(gr['t'].astype(BF16), gr['p'].astype(BF16))
    for lvl in range(base_shift, c_shift):
        off = ((r_i >> (lvl + 1)) == (c_i >> (lvl + 1))) & ((r_i >> lvl) != (c_i >> lvl))
        for gr in groups:
            t16 = gr['t'].astype(BF16)
            gr['at'] = _dot(jnp.where(off, gr['a'], 0.0).astype(BF16), t16)
        for gr in groups:
            gr['t'] = gr['t'] - _dot(gr['t'].astype(BF16), gr['at'].astype(BF16))

    for gr in groups:
        eg = jnp.exp(gr['gc'])
        rhs = jnp.concatenate([gr['k'] * (gr['bt'] * eg), gr['v'] * gr['bt']], axis=1).astype(BF16)
        gr['wu'] = _dot(gr['t'].astype(BF16), rhs)
        gr['qe'] = gr['q'] * eg
    for gr in groups:
        sq = gr['sq']
        s_rows = slice(gr['gi'] * gs, (gr['gi'] + 1) * gs)
        s = s_ref[sq, s_rows, :]
        wq = jnp.concatenate([gr['wu'][:, 0:HEAD_DIM], gr['qe']], axis=0)
        wq_bd = jnp.where(same_s2, jnp.concatenate([wq] * g, axis=1), 0.0).astype(BF16)
        ws_qs = _dot(wq_bd, s.astype(BF16))
        u = gr['wu'][:, HEAD_DIM:2 * HEAD_DIM] - ws_qs[0:gc]
        u16 = u.astype(BF16)
        o = ws_qs[gc:2 * gc] + _dot(gr['qk'].astype(BF16), u16)
        kd = gr['k'] * jnp.exp(gr['gl'] - gr['gc'])
        kd_bd = jnp.where(same_s, jnp.concatenate([kd] * g, axis=1), 0.0).astype(BF16)
        s_ref[sq, s_rows, :] = s * jnp.exp(gr['gl_s']) + _dot_tn(kd_bd, u16)
        on = o * lax.rsqrt(jnp.mean(o * o, axis=-1, keepdims=True) + NORM_EPS) * gnorm
        for j, h in enumerate(gr['heads']):
            sl = slice(h * HEAD_DIM, (h + 1) * HEAD_DIM)
            zh = z_ref[sq, :, sl]
            o_ref[sq, :, sl] = on[j * c:(j + 1) * c] * (zh * _sigmoid(zh))


def _gdn_scan(qkv3, gb3, z3, s0, gnorm, c, g, bb=1):
    b, tp, _ = qkv3.shape
    n = tp // c
    kern = functools.partial(_gdn_scan_kernel, c=c, g=g)
    state_rows = N_GDN_HEADS * HEAD_DIM
    state_spec = pl.BlockSpec((bb, state_rows, HEAD_DIM), lambda bi, i: (bi, 0, 0))
    o, s_new = pl.pallas_call(
        kern,
        grid=(b // bb, n),
        in_specs=[pl.BlockSpec((bb, c, 3 * GDN_W), lambda bi, i: (bi, i, 0)),
                  pl.BlockSpec((bb, c, LANES), lambda bi, i: (bi, i, 0)),
                  pl.BlockSpec((bb, c, GDN_W), lambda bi, i: (bi, i, 0)),
                  state_spec,
                  pl.BlockSpec((1, HEAD_DIM), lambda bi, i: (0, 0))],
        out_specs=(pl.BlockSpec((bb, c, GDN_W), lambda bi, i: (bi, i, 0)), state_spec),
        out_shape=(jax.ShapeDtypeStruct((b, tp, GDN_W), F32),
                   jax.ShapeDtypeStruct((b, state_rows, HEAD_DIM), F32)),
        compiler_params=_cparams(("parallel", "arbitrary")),
        name="gdn_scan",
    )(qkv3, gb3, z3, s0.reshape(b, state_rows, HEAD_DIM), gnorm)
    return o, s_new.reshape(b, N_GDN_HEADS, HEAD_DIM, HEAD_DIM)


def _compress(readers, pe_ref, wbd_ref, cc, cs):
    acc = None
    for c in range(CMP_BLOCK):
        rows = jnp.concatenate([rd(parity, c) for rd in readers for parity in range(2)], axis=0)
        part = _dot((rows + pe_ref[c:c + 1, :]).astype(BF16), wbd_ref[c])
        acc = part if acc is None else acc + part
    return _rope128(acc, cc, cs)


def _softmax_rows(s, mask):
    s = jnp.where(mask, s, -jnp.inf)
    m = jnp.max(s, axis=-1, keepdims=True)
    m = jnp.where(m > -jnp.inf, m, 0.0)
    e = jnp.exp(s - m)
    return e / jnp.maximum(jnp.sum(e, axis=-1, keepdims=True), 1e-30)


def _select_blocks(imp, qpos, n_sel, blk_axis):
    blk = lax.broadcasted_iota(jnp.int32, imp.shape, blk_axis)
    cur = qpos >> HEAD_SHIFT
    forced = (blk == 0) | (blk == cur) | (blk == cur - 1)
    valid = (blk * SEL_BLOCK <= qpos) & (blk < n_sel)
    score = jnp.where(forced, jnp.inf, jnp.where(valid, imp, -jnp.inf))
    rank = jnp.zeros(imp.shape, F32)
    for i in range(min(n_sel, N_SCORE)):
        si = score[i:i + 1, :] if blk_axis == 0 else score[:, i:i + 1]
        ahead = (si > score) | ((si == score) & (blk > i))
        rank = rank + jnp.where(ahead, 1.0, 0.0)
    return ((rank < TOP_K) & (blk < n_sel)).astype(BF16)


def _cmp_branch_rows(q_c, ckv16, qpos, qpos_t, n_half, n_sel, tq):
    r = q_c.shape[0]
    nc = 2 * n_half
    col = lax.broadcasted_iota(jnp.int32, (r, nc), 1)
    cmp_end = (2 * (col & (n_half - 1)) + (col >> int(math.log2(n_half))) + 1) * CMP_BLOCK - 1
    p_c = _softmax_rows(_dot_nt(q_c, ckv16), cmp_end <= qpos)
    o_c = _dot(p_c.astype(BF16), ckv16)[:, HEAD_DIM:2 * HEAD_DIM]
    pair = p_c[:, 0:n_half] + p_c[:, n_half:nc]
    imp = pair[0:tq]
    for g in range(1, GQA_GROUP):
        imp = imp + pair[g * tq:(g + 1) * tq]
    if n_half < N_SCORE:
        imp = jnp.concatenate([imp, jnp.zeros((tq, N_SCORE - n_half), F32)], axis=1)
    sel = _select_blocks(imp, qpos_t, n_sel, 1)
    sel_r = jnp.concatenate([sel] * GQA_GROUP, axis=0)
    return o_c, lambda expand: _dot(sel_r, expand)


def _softmax_cols(s, mask):
    s = jnp.where(mask, s, -jnp.inf)
    m = jnp.max(s, axis=0, keepdims=True)
    m = jnp.where(m > -jnp.inf, m, 0.0)
    e = jnp.exp(s - m)
    return e / jnp.maximum(jnp.sum(e, axis=0, keepdims=True), 1e-30)


def _nsa_core_lanes(q_blk, t0, ckv, n_half, n_sel, ks_rows, ks_t, n_full, kc, win_rows, win_t, win_pos0, kvh,
                    gates, sel_ref):
    tq = q_blk.shape[0]
    r = GQA_GROUP * tq
    nc = 2 * n_half
    q_all = jnp.concatenate([q_blk[:, g * HEAD_DIM:(g + 1) * HEAD_DIM] for g in range(GQA_GROUP)], axis=0)
    zeros64 = jnp.zeros((r, HEAD_DIM), F32)
    q_c = jnp.concatenate([q_all, zeros64], axis=1)
    q_c16 = q_c.astype(BF16)
    q_ct = q_c.T.astype(BF16)
    ckv16 = ckv.astype(BF16)
    qpos_l = (lax.broadcasted_iota(jnp.int32, (1, r), 1) & (tq - 1)) + t0

    row = lax.broadcasted_iota(jnp.int32, (nc, r), 0)
    cmp_end = (2 * (row & (n_half - 1)) + (row >> int(math.log2(n_half))) + 1) * CMP_BLOCK - 1
    p_c = _softmax_cols(_dot_nt(ckv16, q_c16), cmp_end <= qpos_l)
    o_ct = _dot_tn(ckv16, p_c.astype(BF16))
    pair = p_c[0:n_half] + p_c[n_half:nc]
    imp = pair[:, 0:tq]
    for g in range(1, GQA_GROUP):
        imp = imp + pair[:, g * tq:(g + 1) * tq]
    if n_half < N_SCORE:
        imp = jnp.concatenate([imp, jnp.zeros((N_SCORE - n_half, tq), F32)], axis=0)
    sel = _select_blocks(imp, qpos_l[:, 0:tq], n_sel, 0).astype(F32)
    sel_ref[...] = jnp.concatenate([sel] * GQA_GROUP, axis=1)

    blocks_per_chunk = kc // SEL_BLOCK
    krow = lax.broadcasted_iota(jnp.int32, (SEL_BLOCK, r), 0)

    def body(c, carry, causal):
        m, l, acc = carry
        s = _dot(ks_rows(c), q_ct)
        sel_c = sel_ref[pl.ds(pl.multiple_of(c * blocks_per_chunk, blocks_per_chunk), blocks_per_chunk), :]
        parts = []
        for j in range(blocks_per_chunk):
            mask = jnp.broadcast_to(sel_c[j:j + 1, :], (SEL_BLOCK, r)) > 0.5
            if causal:
                mask = mask & (krow + (c * kc + j * SEL_BLOCK) <= qpos_l)
            parts.append(jnp.where(mask, s[j * SEL_BLOCK:(j + 1) * SEL_BLOCK], -jnp.inf))
        s = jnp.concatenate(parts, axis=0)
        m_new = jnp.maximum(m, jnp.max(s, axis=0, keepdims=True))
        m_safe = jnp.where(m_new > -jnp.inf, m_new, 0.0)
        alpha = jnp.exp(m - m_safe)
        p = jnp.exp(s - m_safe)
        l = alpha * l + jnp.sum(p, axis=0, keepdims=True)
        acc = alpha * acc + _dot(ks_t(c), p.astype(BF16))
        return m_new, l, acc

    carry = (jnp.full((1, r), -jnp.inf, F32), jnp.zeros((1, r), F32), jnp.zeros((LANES, r), F32))
    carry = lax.fori_loop(0, n_full, lambda c, cr: body(c, cr, False), carry)
    _, l_s, acc_s = body(n_full, carry, True)
    o_st = acc_s / jnp.maximum(l_s, 1e-30)

    lane = lax.broadcasted_iota(jnp.int32, (r, ROW_W), 1)
    q_w = jnp.where((lane >> HEAD_SHIFT) == kvh, jnp.concatenate([q_all, q_all, zeros64, zeros64], axis=1), 0.0)
    nw = win_rows.shape[0]
    kpos = lax.broadcasted_iota(jnp.int32, (nw, r), 0) + win_pos0
    p_w = _softmax_cols(_dot(win_rows, q_w.T.astype(BF16)), (kpos <= qpos_l) & (kpos >= qpos_l - WINDOW))
    o_wt2 = _dot(win_t, p_w.astype(BF16))
    o_wt = jnp.where(kvh == 0, o_wt2[2 * HEAD_DIM:3 * HEAD_DIM], o_wt2[3 * HEAD_DIM:4 * HEAD_DIM])

    gates_t = gates.T
    outs = []
    for g in range(GQA_GROUP):
        cols = slice(g * tq, (g + 1) * tq)
        g0 = _sigmoid(gates_t[3 * g:3 * g + 1, :])
        g1 = _sigmoid(gates_t[3 * g + 1:3 * g + 2, :])
        g2 = _sigmoid(gates_t[3 * g + 2:3 * g + 3, :])
        outs.append(g0 * o_ct[HEAD_DIM:2 * HEAD_DIM, cols] + g1 * o_st[HEAD_DIM:2 * HEAD_DIM, cols]
                    + g2 * o_wt[:, cols])
    return jnp.concatenate(outs, axis=0).T


def _nsa_core_rows(q_blk, t0, ckv, n_half, sel_chunk, n_full, kc, n_sel, win_rows, win_pos0, kvh, gates):
    tq = q_blk.shape[0]
    r = GQA_GROUP * tq
    q_all = jnp.concatenate([q_blk[:, g * HEAD_DIM:(g + 1) * HEAD_DIM] for g in range(GQA_GROUP)], axis=0)
    qpos_t = lax.broadcasted_iota(jnp.int32, (tq, 1), 0) + t0
    qpos = jnp.concatenate([qpos_t] * GQA_GROUP, axis=0)
    zeros64 = jnp.zeros((r, HEAD_DIM), F32)
    q_c = jnp.concatenate([q_all, zeros64], axis=1).astype(BF16)
    ckv16 = ckv.astype(BF16)
    o_c, picked_fn = _cmp_branch_rows(q_c, ckv16, qpos, qpos_t, n_half, n_sel, tq)

    j_i = lax.broadcasted_iota(jnp.int32, (N_SCORE, kc), 0)
    k_i = lax.broadcasted_iota(jnp.int32, (N_SCORE, kc), 1) >> HEAD_SHIFT
    kcol = lax.broadcasted_iota(jnp.int32, (r, kc), 1)

    def body(c, carry, causal):
        m, l, acc = carry
        rows = sel_chunk(c)
        s = _dot_nt(q_c, rows)
        expand = jnp.where(j_i == k_i + c * (kc // SEL_BLOCK), 1.0, 0.0).astype(BF16)
        mask = picked_fn(expand) > 0.5
        if causal:
            mask = mask & (kcol + c * kc <= qpos)
        s = jnp.where(mask, s, -jnp.inf)
        m_new = jnp.maximum(m, jnp.max(s, axis=-1, keepdims=True))
        m_safe = jnp.where(m_new > -jnp.inf, m_new, 0.0)
        alpha = jnp.exp(m - m_safe)
        p = jnp.exp(s - m_safe)
        l = alpha * l + jnp.sum(p, axis=-1, keepdims=True)
        acc = alpha * acc + _dot(p.astype(BF16), rows)
        return m_new, l, acc

    carry = (jnp.full((r, 1), -jnp.inf, F32), jnp.zeros((r, 1), F32), jnp.zeros((r, LANES), F32))
    if not isinstance(n_full, int) or n_full > 0:
        carry = lax.fori_loop(0, n_full, lambda c, cr: body(c, cr, False), carry)
    _, l_s, acc_s = body(n_full, carry, True)
    o_s = acc_s[:, HEAD_DIM:2 * HEAD_DIM] / jnp.maximum(l_s, 1e-30)

    lane = lax.broadcasted_iota(jnp.int32, (r, ROW_W), 1)
    q_w = jnp.where((lane >> HEAD_SHIFT) == kvh, jnp.concatenate([q_all, q_all, zeros64, zeros64], axis=1), 0.0)
    q_w = q_w.astype(BF16)
    nw = win_rows.shape[0]
    kpos = lax.broadcasted_iota(jnp.int32, (r, nw), 1) + win_pos0
    p_w = _softmax_rows(_dot_nt(q_w, win_rows), (kpos <= qpos) & (kpos >= qpos - WINDOW))
    o_w2 = _dot(p_w.astype(BF16), win_rows)
    o_w = jnp.where(kvh == 0, o_w2[:, 2 * HEAD_DIM:3 * HEAD_DIM], o_w2[:, 3 * HEAD_DIM:4 * HEAD_DIM])

    outs = []
    for g in range(GQA_GROUP):
        rows_g = slice(g * tq, (g + 1) * tq)
        g0 = _sigmoid(gates[:, 3 * g:3 * g + 1])
        g1 = _sigmoid(gates[:, 3 * g + 1:3 * g + 2])
        g2 = _sigmoid(gates[:, 3 * g + 2:3 * g + 3])
        outs.append(g0 * o_c[rows_g] + g1 * o_s[rows_g] + g2 * o_w[rows_g])
    return outs


def _nsa_prompt_kernel(q_ref, kc_ref, ks_ref, kst_ref, win_ref, wint_ref, gate_ref, wbd_ref, pe_ref, cc_ref, cs_ref,
                       o_ref, ckv_ref, sel_ref, *, tq, t, kc):
    kvh = pl.program_id(1)
    i = pl.program_id(2)
    n_half = t // (2 * CMP_BLOCK)

    @pl.when(i == 0)
    def _():
        read = lambda parity, c: kc_ref[0, pl.ds(parity * CMP_BLOCK + c, n_half, stride=2 * CMP_BLOCK), :]
        ckv_ref[...] = _compress([read], pe_ref, wbd_ref, cc_ref[...], cs_ref[...])

    t0 = i * tq
    ks_rows = lambda c: ks_ref[0, pl.ds(pl.multiple_of(c * kc, kc), kc), :].astype(BF16)
    ks_t = lambda c: kst_ref[0, c].astype(BF16)
    span = WINDOW + tq
    start = pl.multiple_of(jnp.maximum(t0 - WINDOW, 0), tq)
    win_rows = win_ref[pl.ds(start, span), :].astype(BF16)
    blk0 = start // LANES
    win_t = jnp.concatenate([wint_ref[blk0 + j] for j in range(span // LANES)], axis=1).astype(BF16)
    o_ref[...] = _nsa_core_lanes(q_ref[...], t0, ckv_ref[...], n_half, t // SEL_BLOCK, ks_rows, ks_t, t0 // kc, kc,
                                 win_rows, win_t, start, kvh, gate_ref[...], sel_ref)


def _nsa_prompt(q, kvc, kvs, kst, win, wint, gate, wbd, pe, cc, cs, b, t, tq=4 * LANES):
    n = t // tq
    nc = t // CMP_BLOCK
    kc = kst.shape[3]
    assert tq % LANES == 0 and kc % tq == 0 and t % kc == 0
    kern = functools.partial(_nsa_prompt_kernel, tq=tq, t=t, kc=kc)
    fix2 = lambda bi, h, i: (0, 0)
    return pl.pallas_call(
        kern,
        grid=(b, N_KV_HEADS, n),
        in_specs=[pl.BlockSpec((tq, ROW_W), lambda bi, h, i: (bi * n + i, h)),
                  pl.BlockSpec((1, t, LANES), lambda bi, h, i: (h, bi, 0)),
                  pl.BlockSpec((1, t, LANES), lambda bi, h, i: (h, bi, 0)),
                  pl.BlockSpec((1, t // kc, LANES, kc), lambda bi, h, i: (h, bi, 0, 0)),
                  pl.BlockSpec((t, ROW_W), lambda bi, h, i: (bi, 0)),
                  pl.BlockSpec((t // LANES, ROW_W, LANES), lambda bi, h, i: (bi, 0, 0)),
                  pl.BlockSpec((tq, LANES), lambda bi, h, i: (bi * n + i, h)),
                  pl.BlockSpec((CMP_BLOCK, LANES, LANES), lambda bi, h, i: (0, 0, 0)),
                  pl.BlockSpec((CMP_BLOCK, LANES), fix2),
                  pl.BlockSpec((nc, LANES), fix2),
                  pl.BlockSpec((nc, LANES), fix2)],
        out_specs=pl.BlockSpec((tq, ROW_W), lambda bi, h, i: (bi * n + i, h)),
        out_shape=jax.ShapeDtypeStruct((b * t, NSA_W), F32),
        scratch_shapes=[pltpu.VMEM((nc, LANES), F32), pltpu.VMEM((N_SCORE, GQA_GROUP * tq), F32)],
        compiler_params=_cparams(("parallel", "parallel", "arbitrary")),
        name="nsa_prompt",
    )(q, kvc, kvs, kst, win, wint, gate, wbd, pe, cc, cs)


def _nsa_sample_kernel(pt_ref, *refs, n_steps, pages_per_step, past, tq, ts, n_pad, nw):
    page_refs = refs[:pages_per_step]
    (perm_ref, q_ref, kvnew_ref, wbuf_ref, wnew_ref, gate_ref, wbd_ref, pe_ref, cc_ref, cs_ref,
     o_ref, wout_ref, pge_ref, pgo_ref, pgs_ref, wn_ref) = refs[pages_per_step:]
    j = pl.program_id(1)
    half = perm_ref.shape[0] // 2
    per_parity = half // CMP_BLOCK
    slot = pl.ds(pl.multiple_of(j * per_parity, per_parity), per_parity)
    for kvh in range(N_KV_HEADS):
        for k, page_ref in enumerate(page_refs):
            rows = pl.ds(pl.multiple_of((j * pages_per_step + k) * PAGE_SIZE, PAGE_SIZE), PAGE_SIZE)
            pgs_ref[kvh, rows, :] = page_ref[0, 0, kvh, LANES:ROW_W, :].T
        cmp_t = jnp.concatenate([page_ref[0, 0, kvh, 0:LANES, :] for page_ref in page_refs], axis=1)
        grouped = _dot_nt(perm_ref[...], cmp_t.astype(BF16))
        for c in range(CMP_BLOCK):
            pge_ref[kvh, c, slot, :] = grouped[c * per_parity:(c + 1) * per_parity]
            pgo_ref[kvh, c, slot, :] = grouped[half + c * per_parity:half + (c + 1) * per_parity]

    @pl.when(j == n_steps - 1)
    def _():
        n_half = past // (2 * CMP_BLOCK)
        wbuf_t = wbuf_ref[0, 0]
        wnew = wnew_ref[0]
        wn_ref[0:WINDOW, :] = wbuf_t.T
        wn_ref[WINDOW:WINDOW + tq, :] = wnew
        wn_ref[WINDOW + tq:nw, :] = jnp.zeros((nw - WINDOW - tq, ROW_W), F32)
        win_rows = wn_ref[...].astype(BF16)

        shifted = pltpu.roll(wbuf_t, WINDOW - ts, 1)
        new_t = jnp.concatenate([wnew, jnp.zeros((LANES - tq, ROW_W), F32)], axis=0).T
        lane = lax.broadcasted_iota(jnp.int32, (ROW_W, LANES), 1)
        tail = jnp.where(lane >= LANES - ts, pltpu.roll(new_t, LANES - ts, 1), shifted[:, WINDOW - LANES:WINDOW])
        wout_ref[0, :, 0:WINDOW - LANES] = shifted[:, 0:WINDOW - LANES]
        wout_ref[0, :, WINDOW - LANES:WINDOW] = tail

        readers = []
        for kvh in range(N_KV_HEADS):
            pgs_ref[kvh, past:past + tq, :] = kvnew_ref[kvh, 0]
            pgs_ref[kvh, past + tq:n_pad, :] = jnp.zeros((n_pad - past - tq, LANES), F32)
            readers.append(lambda parity, c, kvh=kvh: (pgo_ref if parity else pge_ref)[kvh, c])
        ckv2 = _compress(readers, pe_ref, wbd_ref, cc_ref[...], cs_ref[...])
        for kvh in range(N_KV_HEADS):
            ckv = ckv2[kvh * 2 * n_half:(kvh + 1) * 2 * n_half]
            sel_chunk = lambda c, kvh=kvh: pgs_ref[kvh].astype(BF16)
            outs = _nsa_core_rows(q_ref[0, :, kvh * ROW_W:(kvh + 1) * ROW_W], past, ckv, n_half, sel_chunk, 0,
                             n_pad, past // SEL_BLOCK + 1, win_rows, past - WINDOW, kvh,
                             gate_ref[0, :, kvh * LANES:(kvh + 1) * LANES])
            for g in range(GQA_GROUP):
                col = kvh * ROW_W + g * HEAD_DIM
                o_ref[0, :, col:col + HEAD_DIM] = outs[g]


def _page_group_permutation(pages_per_step):
    n = pages_per_step * PAGE_SIZE
    half = n // 2
    per_parity = half // CMP_BLOCK
    r = np.arange(n)
    parity, c, i = r // half, (r % half) // per_parity, r % per_parity
    perm = np.zeros((n, n), np.float32)
    perm[r, (2 * i + parity) * CMP_BLOCK + c] = 1.0
    return jnp.asarray(perm, dtype=BF16)


def _nsa_sample(page_table, pool_t, layer, ts, q8, kvnew8, win_t, wnew8, gate8, wbd, pe, cc, cs, pages_per_step=4):
    b, n_pages = page_table.shape
    past = n_pages * PAGE_SIZE
    tq = q8.shape[1]
    n_pad = past + SEL_BLOCK
    nw = WINDOW + LANES
    nc2 = N_KV_HEADS * (past // CMP_BLOCK)
    n_steps = n_pages // pages_per_step
    n_half = past // (2 * CMP_BLOCK)
    perm = _page_group_permutation(pages_per_step)
    assert (pages_per_step * PAGE_SIZE // (2 * CMP_BLOCK)) % SUBLANES == 0
    kern = functools.partial(_nsa_sample_kernel, n_steps=n_steps, pages_per_step=pages_per_step, past=past,
                             tq=tq, ts=ts, n_pad=n_pad, nw=nw)
    per_b = lambda bi, j, pt: (bi, 0, 0)
    fix2 = lambda bi, j, pt: (0, 0)
    page_spec = lambda k: pl.BlockSpec((1, 1, N_KV_HEADS, ROW_W, PAGE_SIZE),
                                       lambda bi, j, pt: (layer, pt[bi, j * pages_per_step + k], 0, 0, 0))
    grid_spec = pltpu.PrefetchScalarGridSpec(
        num_scalar_prefetch=1,
        grid=(b, n_steps),
        in_specs=[page_spec(k) for k in range(pages_per_step)] + [
                  pl.BlockSpec(perm.shape, fix2),
                  pl.BlockSpec((1, tq, NSA_W), per_b),
                  pl.BlockSpec((N_KV_HEADS, 1, tq, LANES), lambda bi, j, pt: (0, bi, 0, 0)),
                  pl.BlockSpec((1, 1, ROW_W, WINDOW), lambda bi, j, pt: (layer, bi, 0, 0)),
                  pl.BlockSpec((1, tq, ROW_W), per_b),
                  pl.BlockSpec((1, tq, N_KV_HEADS * LANES), per_b),
                  pl.BlockSpec((CMP_BLOCK, LANES, LANES), lambda bi, j, pt: (0, 0, 0)),
                  pl.BlockSpec((CMP_BLOCK, LANES), fix2),
                  pl.BlockSpec((nc2, LANES), fix2),
                  pl.BlockSpec((nc2, LANES), fix2)],
        out_specs=(pl.BlockSpec((1, tq, NSA_W), per_b), pl.BlockSpec((1, ROW_W, WINDOW), per_b)),
        scratch_shapes=[pltpu.VMEM((N_KV_HEADS, CMP_BLOCK, n_half, LANES), F32),
                        pltpu.VMEM((N_KV_HEADS, CMP_BLOCK, n_half, LANES), F32),
                        pltpu.VMEM((N_KV_HEADS, n_pad, LANES), F32), pltpu.VMEM((nw, ROW_W), F32)],
    )
    return pl.pallas_call(
        kern,
        grid_spec=grid_spec,
        out_shape=(jax.ShapeDtypeStruct((b, tq, NSA_W), F32), jax.ShapeDtypeStruct((b, ROW_W, WINDOW), F32)),
        compiler_params=_cparams(("parallel", "arbitrary")),
        name="nsa_sample",
    )(page_table, *([pool_t] * pages_per_step), perm, q8, kvnew8, win_t, wnew8, gate8, wbd, pe, cc, cs)


def _mlp_kernel(x_ref, og_ref, on_ref, wo_ref, gm_ref, wu_ref, wd_ref, gf_ref, y_ref, x1_ref, h_ref, acc_ref,
                *, final_norm):
    f = pl.program_id(1)

    @pl.when(f == 0)
    def _():
        x1 = (x_ref[...] + _dot(og_ref[...].astype(BF16), wo_ref[0:GDN_W, :])
              + _dot(on_ref[...].astype(BF16), wo_ref[GDN_W:GDN_W + NSA_W, :]))
        x1_ref[...] = x1
        var = jnp.mean(x1 * x1, axis=-1, keepdims=True)
        h_ref[...] = (x1 * lax.rsqrt(var + NORM_EPS) * gm_ref[...]).astype(BF16)
        acc_ref[...] = jnp.zeros(acc_ref.shape, F32)

    up = jnp.maximum(_dot(h_ref[...], wu_ref[...]), 0.0)
    acc_ref[...] += _dot((up * up).astype(BF16), wd_ref[...])

    @pl.when(f == pl.num_programs(1) - 1)
    def _():
        y = x1_ref[...] + acc_ref[...]
        if final_norm:
            var = jnp.mean(y * y, axis=-1, keepdims=True)
            y = y * lax.rsqrt(var + NORM_EPS) * gf_ref[...]
        y_ref[...] = y


def _mlp(x, og, on, wo, gm, wu, wd, gf, final_norm, tm, tf=D_FF):
    m = x.shape[0]
    kern = functools.partial(_mlp_kernel, final_norm=final_norm)
    row = lambda i, f: (i, 0)
    fix = lambda i, f: (0, 0)
    return pl.pallas_call(
        kern,
        grid=(m // tm, D_FF // tf),
        in_specs=[pl.BlockSpec((tm, D_MODEL), row), pl.BlockSpec((tm, GDN_W), row), pl.BlockSpec((tm, NSA_W), row),
                  pl.BlockSpec((D_MODEL, D_MODEL), fix), pl.BlockSpec((1, D_MODEL), fix),
                  pl.BlockSpec((D_MODEL, tf), lambda i, f: (0, f)), pl.BlockSpec((tf, D_MODEL), lambda i, f: (f, 0)),
                  pl.BlockSpec((1, D_MODEL), fix)],
        out_specs=pl.BlockSpec((tm, D_MODEL), row),
        out_shape=jax.ShapeDtypeStruct((m, D_MODEL), F32),
        scratch_shapes=[pltpu.VMEM((tm, D_MODEL), F32), pltpu.VMEM((tm, D_MODEL), BF16),
                        pltpu.VMEM((tm, D_MODEL), F32)],
        compiler_params=_cparams(("parallel", "arbitrary")),
        name="mlp",
    )(x, og, on, wo, gm, wu, wd, gf)


def _proj_columns():
    src = np.full((N_PROJ,), -1, np.int64)
    o_z, o_a, o_b = 3 * GDN_W, 4 * GDN_W, 4 * GDN_W + N_GDN_HEADS
    o_q = o_b + N_GDN_HEADS
    o_kv = o_q + NSA_W
    o_g = o_kv + 6 * KV_W
    src[C_QKV:C_Z] = np.arange(0, 3 * GDN_W)
    src[C_Z:C_Q] = np.arange(o_z, o_z + GDN_W)
    src[C_Q:C_KV] = np.arange(o_q, o_q + NSA_W)
    for kvh in range(N_KV_HEADS):
        for s in range(4):
            dst = C_KV + kvh * ROW_W + s * HEAD_DIM
            src[dst:dst + HEAD_DIM] = o_kv + s * KV_W + kvh * HEAD_DIM + np.arange(HEAD_DIM)
    src[C_WIN:C_WIN + 2 * KV_W] = o_kv + 4 * KV_W + np.arange(2 * KV_W)
    src[C_AB:C_AB + N_GDN_HEADS] = o_a + np.arange(N_GDN_HEADS)
    src[C_AB + N_GDN_HEADS:C_AB + 2 * N_GDN_HEADS] = o_b + np.arange(N_GDN_HEADS)
    for kvh in range(N_KV_HEADS):
        dst = C_GATE + kvh * LANES
        src[dst:dst + 3 * GQA_GROUP] = o_g + kvh * 3 * GQA_GROUP + np.arange(3 * GQA_GROUP)
    return src


def _rope_tables(pos):
    half = HEAD_DIM // 2
    inv_freq = ROPE_THETA ** (-jnp.arange(half, dtype=F32) / half)
    ang = pos.astype(F32)[:, None] * inv_freq[None, :]
    cos, sin = jnp.cos(ang), jnp.sin(ang)
    c64 = jnp.concatenate([cos, cos], axis=1)
    s64 = jnp.concatenate([-sin, sin], axis=1)
    one, zero = jnp.ones_like(c64), jnp.zeros_like(s64)
    return (jnp.concatenate([c64, c64], axis=1), jnp.concatenate([s64, s64], axis=1),
            jnp.concatenate([c64, one], axis=1), jnp.concatenate([s64, zero], axis=1))


def _cmp_tables(n_blocks):
    n_half = n_blocks // 2
    r = np.arange(n_blocks)
    blk = 2 * (r % n_half) + r // n_half
    end_pos = jnp.asarray((blk + 1) * CMP_BLOCK - 1)
    _, _, cc, cs = _rope_tables(end_pos)
    return cc, cs


def _layer_params(l, norm_mix, w_in, conv_w, a_log, dt_bias, gdn_norm, cmp_pe_k, cmp_w_k, cmp_pe_v, cmp_w_v,
                  w_out, norm_mlp, w_up, w_down):
    src = _proj_columns()
    w_ext = jnp.concatenate([w_in[l], jnp.zeros((D_MODEL, 1), F32)], axis=1)
    w_p = jnp.take(w_ext, jnp.asarray(np.where(src < 0, w_in.shape[2], src)), axis=1).astype(BF16)
    zeros = jnp.zeros((CMP_BLOCK, HEAD_DIM, HEAD_DIM), F32)
    wbd = jnp.concatenate([jnp.concatenate([cmp_w_k[l], zeros], axis=2),
                           jnp.concatenate([zeros, cmp_w_v[l]], axis=2)], axis=1).astype(BF16)
    pad_row = lambda v: jnp.pad(v.astype(F32), (0, LANES - v.shape[0]))[None, :]
    return dict(
        norm_mix=norm_mix[l][None, :], w_p=w_p, conv_w=conv_w[l], alog=pad_row(a_log[l]), dtb=pad_row(dt_bias[l]),
        gdn_norm=gdn_norm[l][None, :], wbd=wbd, pe=jnp.concatenate([cmp_pe_k[l], cmp_pe_v[l]], axis=1),
        w_out=w_out[l].astype(BF16), norm_mlp=norm_mlp[l][None, :], w_up=w_up[l].astype(BF16),
        w_down=w_down[l].astype(BF16))


def _prompt_layer(x, lp, tabs, cmp_tabs, gf, final_norm, b, t):
    qkv_pre, z, q, kvc, kvs, win, ab, gate, kst, wint = _proj(x, lp['norm_mix'], lp['w_p'], tabs, PROJ_ROWS)
    w3 = 3 * GDN_W
    cb8 = jnp.zeros((b, SUBLANES, w3), F32)
    qkv3, gb3 = _gdn_prep(qkv_pre.reshape(b, t, w3), cb8, ab.reshape(b, t, LANES), lp['conv_w'], lp['alog'],
                          lp['dtb'], 256, t, 256)
    s0 = jnp.zeros((b, N_GDN_HEADS, HEAD_DIM, HEAD_DIM), F32)
    o_gdn, s_new = _gdn_scan(qkv3, gb3, z.reshape(b, t, GDN_W), s0, lp['gdn_norm'], GDN_CHUNK, 4)
    o_nsa = _nsa_prompt(q, kvc, kvs, kst, win, wint, gate, lp['wbd'], lp['pe'], cmp_tabs[0], cmp_tabs[1], b, t)
    y = _mlp(x, o_gdn.reshape(b * t, GDN_W), o_nsa, lp['w_out'], lp['norm_mlp'], lp['w_up'], lp['w_down'], gf,
             final_norm, 512)
    rows = jnp.concatenate([kvc, kvs], axis=2).reshape(N_KV_HEADS, b, t, 4, HEAD_DIM).transpose(1, 0, 2, 3, 4)
    win_new = win.reshape(b, t, 2, N_KV_HEADS, HEAD_DIM)[:, t - min(WINDOW, t):]
    conv_new = qkv_pre.reshape(b, t, w3)[:, t - (CONV_W - 1):]
    return y, rows, win_new, s_new, conv_new


def _pad_rows(a, n):
    return jnp.pad(a, ((0, 0), (0, n - a.shape[1]), (0, 0)))


def _sample_layer(x, lp, tabs, cmp_tabs, gf, final_norm, b, t, layer, pool_t, page_table, win_t, s0, conv_buf):
    m = b * t
    qkv_pre, z, q, kvc, kvs, win, ab, gate, _, _ = _proj(x, lp['norm_mix'], lp['w_p'], tabs, PROJ_ROWS)
    w3 = 3 * GDN_W
    tp = SUBLANES
    cb8 = jnp.pad(conv_buf, ((0, 0), (SUBLANES - (CONV_W - 1), 0), (0, 0)))
    qkv3, gb3 = _gdn_prep(_pad_rows(qkv_pre.reshape(b, t, w3), tp), cb8, _pad_rows(ab.reshape(b, t, LANES), tp),
                          lp['conv_w'], lp['alog'], lp['dtb'], tp, t, tp, bb=SAMPLE_SEQS_PER_STEP)
    o_gdn, s_new = _gdn_scan(qkv3, gb3, _pad_rows(z.reshape(b, t, GDN_W), tp), s0, lp['gdn_norm'], tp,
                             N_GDN_HEADS, bb=SAMPLE_SEQS_PER_STEP // 2)
    o_gdn = o_gdn[:, :t].reshape(m, GDN_W)
    kvnew8 = jnp.pad(kvs.reshape(N_KV_HEADS, b, t, LANES), ((0, 0), (0, 0), (0, tp - t), (0, 0)))
    o_nsa, win_out_t = _nsa_sample(page_table, pool_t, layer, t, _pad_rows(q.reshape(b, t, NSA_W), tp), kvnew8,
                                   win_t, _pad_rows(win.reshape(b, t, ROW_W), tp),
                                   _pad_rows(gate.reshape(b, t, N_KV_HEADS * LANES), tp), lp['wbd'], lp['pe'],
                                   cmp_tabs[0], cmp_tabs[1])
    o_nsa = o_nsa[:, :t].reshape(m, NSA_W)
    y = _mlp(x, o_gdn, o_nsa, lp['w_out'], lp['norm_mlp'], lp['w_up'], lp['w_down'], gf, final_norm, 512)
    rows = jnp.concatenate([kvc, kvs], axis=2).reshape(N_KV_HEADS, b, t, 4, HEAD_DIM).transpose(1, 0, 2, 3, 4)
    win_new = win_out_t.reshape(b, 2, N_KV_HEADS, HEAD_DIM, WINDOW).transpose(0, 4, 1, 2, 3)
    conv_new = jnp.concatenate([conv_buf, qkv_pre.reshape(b, t, w3)], axis=1)[:, t:]
    return y, rows, win_new, s_new, conv_new


def kernel(x_prompt, x_sample, cache_kv, page_table, state_win, state_gdn, state_conv, norm_mix, w_in, conv_w, a_log, dt_bias, gdn_norm, cmp_pe_k, cmp_w_k, cmp_pe_v, cmp_w_v, w_out, norm_mlp, w_up, w_down, norm_final):
    bp, tp_, _ = x_prompt.shape
    bs, ts, _ = x_sample.shape
    depth = cache_kv.shape[0]
    n_pages = page_table.shape[1]
    past = n_pages * PAGE_SIZE
    assert state_win.shape[2] == WINDOW and tp_ % 512 == 0 and ts <= SUBLANES and past % (2 * CMP_BLOCK) == 0

    tabs_p = _rope_tables(jnp.arange(tp_))
    tabs_s = tuple(jnp.tile(tb, (bs, 1)) for tb in _rope_tables(past + jnp.arange(ts)))
    cmp_p = _cmp_tables(tp_ // CMP_BLOCK)
    cmp_s = tuple(jnp.tile(tb, (N_KV_HEADS, 1)) for tb in _cmp_tables((past + ts) // CMP_BLOCK))
    pool_t = cache_kv.transpose(0, 1, 2, 4, 5, 3).reshape(depth, cache_kv.shape[1], N_KV_HEADS, ROW_W, PAGE_SIZE)
    win_t = state_win.transpose(0, 1, 3, 4, 5, 2).reshape(depth, bs, ROW_W, WINDOW)
    gf = norm_final[None, :]

    xp = x_prompt.reshape(bp * tp_, D_MODEL)
    xs = x_sample.reshape(bs * ts, D_MODEL)
    outs = [[] for _ in range(8)]
    for l in range(depth):
        lp = _layer_params(l, norm_mix, w_in, conv_w, a_log, dt_bias, gdn_norm, cmp_pe_k, cmp_w_k, cmp_pe_v,
                           cmp_w_v, w_out, norm_mlp, w_up, w_down)
        last = l == depth - 1
        xp, r, w, s, c = _prompt_layer(xp, lp, tabs_p, cmp_p, gf, last, bp, tp_)
        for k, v in zip((0, 2, 4, 6), (r, w, s, c)):
            outs[k].append(v)
        xs, r, w, s, c = _sample_layer(xs, lp, tabs_s, cmp_s, gf, last, bs, ts, l, pool_t, page_table,
                                       win_t, state_gdn[l], state_conv[l])
        for k, v in zip((1, 3, 5, 7), (r, w, s, c)):
            outs[k].append(v)
    return (xp.reshape(bp, tp_, D_MODEL), xs.reshape(bs, ts, D_MODEL)) + tuple(jnp.stack(o) for o in outs)
```
